```python
import jax, jax.numpy as jnp
from jax import lax
import numpy as np

D_MODEL = 1024
BATCH = 8
SEQ = 4096
DEPTH = 2

N_HEADS = 8
QK_NOPE = 64
QK_ROPE = 32
QK_HEAD = QK_NOPE + QK_ROPE
V_HEAD = 64
Q_LORA = 384
KV_LORA = 256
ROPE_THETA = 10000.0
Q_BLOCK = 128
CHUNK = 128
SGU_GROUPS = 8
SGU_GROUP_DIM = 64
SGU_WIDTH = SGU_GROUPS * SGU_GROUP_DIM
FNET_GROUPS = 4
FNET_GROUP_DIM = 128
FNET_WIDTH = FNET_GROUPS * FNET_GROUP_DIM
N_BRANCH = 3
IN_SPLITS = (Q_LORA, KV_LORA, QK_ROPE, 2 * SGU_WIDTH, FNET_WIDTH, N_BRANCH * D_MODEL)
N_IN = sum(IN_SPLITS)
N_GROUPS = 4
EXPERTS_PER_GROUP = 8
N_EXPERTS = N_GROUPS * EXPERTS_PER_GROUP
TOP_K = 2
D_EXPERT = 256
EPS = 1e-6

kernel_name = "hybrid_mla_sgu_fnet_hiermoe_encoder"


def rms_norm(x, g):
    xf = x.astype(jnp.float32)
    y = xf * lax.rsqrt(jnp.mean(xf * xf, axis=-1, keepdims=True) + EPS)
    return (y * g.astype(jnp.float32)).astype(x.dtype)


def layer_norm(x, g, b):
    xf = x.astype(jnp.float32)
    mu = jnp.mean(xf, axis=-1, keepdims=True)
    xc = xf - mu
    var = jnp.mean(xc * xc, axis=-1, keepdims=True)
    y = xc * lax.rsqrt(var + EPS) * g.astype(jnp.float32) + b.astype(jnp.float32)
    return y.astype(x.dtype)


def rope_tables(positions):
    inv = 1.0 / (ROPE_THETA ** (jnp.arange(0, QK_ROPE, 2, dtype=jnp.float32) / QK_ROPE))
    ang = positions.astype(jnp.float32)[..., None] * inv
    return jnp.cos(ang), jnp.sin(ang)


def apply_rope(x, cos, sin):
    xf = x.astype(jnp.float32)
    x1, x2 = jnp.split(xf, 2, axis=-1)
    c = cos[:, :, None, :]
    s = sin[:, :, None, :]
    return jnp.concatenate([x1 * c - x2 * s, x1 * s + x2 * c], axis=-1).astype(x.dtype)


def block_attention(q, k, v):
    B, S, H, Dq = q.shape
    nb = S // Q_BLOCK
    scale = QK_HEAD ** -0.5
    qb = q.reshape(B, nb, Q_BLOCK, H, Dq).transpose(1, 0, 2, 3, 4)

    def one_block(q_blk):
        s = jnp.einsum('bqhd,bkhd->bhqk', q_blk, k, preferred_element_type=jnp.float32) * scale
        p = jax.nn.softmax(s, axis=-1)
        return jnp.einsum('bhqk,bkhd->bqhd', p.astype(v.dtype), v)

    o = lax.map(one_block, qb)
    return o.transpose(1, 0, 2, 3, 4).reshape(B, S, H * V_HEAD)


def mla_branch(c_q, c_kv, k_rope, cos, sin, q_a_norm, w_uq, kv_a_norm, w_ukv, q_norm, k_norm):
    B, S, _ = c_q.shape
    q = (rms_norm(c_q, q_a_norm) @ w_uq).reshape(B, S, N_HEADS, QK_HEAD)
    kv = (rms_norm(c_kv, kv_a_norm) @ w_ukv).reshape(B, S, N_HEADS, QK_NOPE + V_HEAD)
    k_nope, v = kv[..., :QK_NOPE], kv[..., QK_NOPE:]
    k_r = jnp.broadcast_to(k_rope[:, :, None, :], (B, S, N_HEADS, QK_ROPE))
    k = jnp.concatenate([k_nope, k_r], axis=-1)
    q = rms_norm(q, q_norm)
    k = rms_norm(k, k_norm)
    q = jnp.concatenate([q[..., :QK_NOPE], apply_rope(q[..., QK_NOPE:], cos, sin)], axis=-1)
    k = jnp.concatenate([k[..., :QK_NOPE], apply_rope(k[..., QK_NOPE:], cos, sin)], axis=-1)
    return block_attention(q, k, v)


def sgu_branch(uv_pre, ln_g, ln_b, w_s, b_s):
    B, S, _ = uv_pre.shape
    uv = jax.nn.gelu(uv_pre)
    u, v = uv[..., :SGU_WIDTH], uv[..., SGU_WIDTH:]
    v = layer_norm(v, ln_g, ln_b)
    v = v.reshape(B, S // CHUNK, CHUNK, SGU_GROUPS, SGU_GROUP_DIM)
    mixed = jnp.einsum('gpq,bcqgd->bcpgd', w_s, v) + b_s.T[None, None, :, :, None]
    return u * mixed.reshape(B, S, SGU_WIDTH)


def fourier_branch(z):
    B, S, _ = z.shape
    zg = z.reshape(B, S, FNET_GROUPS, FNET_GROUP_DIM).astype(jnp.float32)
    f = jnp.real(jnp.fft.fft2(zg, axes=(1, 3), norm='ortho'))
    return f.reshape(B, S, FNET_WIDTH).astype(z.dtype)


def mixer_sublayer(x, cos, sin, attn_norm, w_in, q_a_norm, w_uq, kv_a_norm, w_ukv, q_norm, k_norm,
                   sgu_ln_g, sgu_ln_b, sgu_w, sgu_b, w_proj_attn, w_proj_sgu, w_proj_fnet, w_out):
    B, S, _ = x.shape
    h = rms_norm(x, attn_norm)
    z = h @ w_in
    cuts = [int(c) for c in np.cumsum(IN_SPLITS)[:-1]]
    c_q, c_kv, k_rope, sgu_in, fnet_in, gate_in = jnp.split(z, cuts, axis=-1)
    y_attn = mla_branch(c_q, c_kv, k_rope, cos, sin, q_a_norm, w_uq, kv_a_norm, w_ukv, q_norm, k_norm)
    y_sgu = sgu_branch(sgu_in, sgu_ln_g, sgu_ln_b, sgu_w, sgu_b)
    y_fnet = fourier_branch(fnet_in)
    gates = jax.nn.sigmoid(gate_in.astype(jnp.float32)).astype(x.dtype).reshape(B, S, N_BRANCH, D_MODEL)
    merged = (gates[:, :, 0] * (y_attn @ w_proj_attn)
              + gates[:, :, 1] * (y_sgu @ w_proj_sgu)
              + gates[:, :, 2] * (y_fnet @ w_proj_fnet))
    return x + merged @ w_out


def moe_sublayer(x, ffn_norm, w_group, b_group, w_router, b_router, w1, w3, w2):
    B, S, D = x.shape
    T = B * S
    h = rms_norm(x, ffn_norm).reshape(T, D)
    g_logits = (h @ w_group + b_group).astype(jnp.float32)
    g_prob = jax.nn.softmax(g_logits, axis=-1)
    g_sel = jnp.argmax(g_logits, axis=-1)
    g_w = jnp.take_along_axis(g_prob, g_sel[:, None], axis=-1)
    e_logits = (h @ w_router + b_router).astype(jnp.float32).reshape(T, N_GROUPS, EXPERTS_PER_GROUP)
    e_in_group = jnp.take_along_axis(e_logits, g_sel[:, None, None], axis=1)[:, 0]
    e_top, e_idx = lax.top_k(e_in_group, TOP_K)
    e_w = jax.nn.softmax(e_top, axis=-1) * g_w
    flat_idx = g_sel[:, None] * EXPERTS_PER_GROUP + e_idx
    combine = jnp.sum(jax.nn.one_hot(flat_idx, N_EXPERTS, dtype=jnp.float32) * e_w[..., None], axis=1)
    combine = combine.astype(h.dtype)
    out = jnp.zeros_like(h)
    for gi in range(N_GROUPS):
        sl = slice(gi * EXPERTS_PER_GROUP, (gi + 1) * EXPERTS_PER_GROUP)
        a = jnp.einsum('td,edf->tef', h, w1[sl])
        b = jnp.einsum('td,edf->tef', h, w3[sl])
        hid = jax.nn.silu(a) * b * combine[:, sl, None]
        out = out + jnp.einsum('tef,efd->td', hid, w2[sl])
    return x + out.reshape(B, S, D)


def setup_inputs(seed: int = 0) -> dict:
    key = jax.random.key(seed)
    ks = jax.random.split(key, 32)
    f32 = jnp.float32

    def nrm(k, shape, scale):
        return jax.random.normal(k, shape, f32) * scale

    def gain(k, shape):
        return 1.0 + 0.05 * jax.random.normal(k, shape, f32)

    L = DEPTH
    x = jax.random.normal(ks[0], (BATCH, SEQ, D_MODEL), f32)
    positions = (jnp.arange(SEQ, dtype=jnp.int32)[None, :]
                 + jax.random.randint(ks[1], (BATCH, 1), 0, 1024, dtype=jnp.int32)).astype(jnp.int32)
    return {
        "x": x,
        "positions": positions,
        "attn_norm": gain(ks[2], (L, D_MODEL)),
        "w_in": nrm(ks[3], (L, D_MODEL, N_IN), D_MODEL ** -0.5),
        "q_a_norm": gain(ks[4], (L, Q_LORA)),
        "w_uq": nrm(ks[5], (L, Q_LORA, N_HEADS * QK_HEAD), Q_LORA ** -0.5),
        "kv_a_norm": gain(ks[6], (L, KV_LORA)),
        "w_ukv": nrm(ks[7], (L, KV_LORA, N_HEADS * (QK_NOPE + V_HEAD)), KV_LORA ** -0.5),
        "q_norm": gain(ks[8], (L, QK_HEAD)),
        "k_norm": gain(ks[9], (L, QK_HEAD)),
        "sgu_ln_g": gain(ks[10], (L, SGU_WIDTH)),
        "sgu_ln_b": nrm(ks[11], (L, SGU_WIDTH), 0.02),
        "sgu_w": nrm(ks[12], (L, SGU_GROUPS, CHUNK, CHUNK), CHUNK ** -0.5),
        "sgu_b": 1.0 + nrm(ks[13], (L, SGU_GROUPS, CHUNK), 0.1),
        "w_proj_attn": nrm(ks[14], (L, N_HEADS * V_HEAD, D_MODEL), (N_HEADS * V_HEAD) ** -0.5),
        "w_proj_sgu": nrm(ks[15], (L, SGU_WIDTH, D_MODEL), SGU_WIDTH ** -0.5),
        "w_proj_fnet": nrm(ks[16], (L, FNET_WIDTH, D_MODEL), FNET_WIDTH ** -0.5),
        "w_out": nrm(ks[17], (L, D_MODEL, D_MODEL), D_MODEL ** -0.5),
        "ffn_norm": gain(ks[18], (L, D_MODEL)),
        "w_group": nrm(ks[19], (L, D_MODEL, N_GROUPS), D_MODEL ** -0.5),
        "b_group": nrm(ks[20], (L, N_GROUPS), 0.01),
        "w_router": nrm(ks[21], (L, D_MODEL, N_EXPERTS), D_MODEL ** -0.5),
        "b_router": nrm(ks[22], (L, N_EXPERTS), 0.01),
        "w1": nrm(ks[23], (L, N_EXPERTS, D_MODEL, D_EXPERT), D_MODEL ** -0.5),
        "w3": nrm(ks[24], (L, N_EXPERTS, D_MODEL, D_EXPERT), D_MODEL ** -0.5),
        "w2": nrm(ks[25], (L, N_EXPERTS, D_EXPERT, D_MODEL), D_EXPERT ** -0.5),
    }


def reference(x, positions, attn_norm, w_in, q_a_norm, w_uq, kv_a_norm, w_ukv, q_norm, k_norm,
              sgu_ln_g, sgu_ln_b, sgu_w, sgu_b, w_proj_attn, w_proj_sgu, w_proj_fnet, w_out,
              ffn_norm, w_group, b_group, w_router, b_router, w1, w3, w2):
    cos, sin = rope_tables(positions)
    for l in range(DEPTH):
        x = mixer_sublayer(x, cos, sin, attn_norm[l], w_in[l], q_a_norm[l], w_uq[l], kv_a_norm[l],
                           w_ukv[l], q_norm[l], k_norm[l], sgu_ln_g[l], sgu_ln_b[l], sgu_w[l], sgu_b[l],
                           w_proj_attn[l], w_proj_sgu[l], w_proj_fnet[l], w_out[l])
        x = moe_sublayer(x, ffn_norm[l], w_group[l], b_group[l], w_router[l], b_router[l],
                         w1[l], w3[l], w2[l])
    return x
```

```python
import functools
import math

import numpy as np
import jax
import jax.numpy as jnp
from jax import lax
from jax.experimental import pallas as pl
from jax.experimental.pallas import tpu as pltpu

F32 = jnp.float32
BF16 = jnp.bfloat16

D_MODEL = 1024
N_HEADS = 8
QK_NOPE = 64
QK_ROPE = 32
QK_HEAD = QK_NOPE + QK_ROPE
V_HEAD = 64
Q_LORA = 384
KV_LORA = 256
ROPE_THETA = 10000.0
CHUNK = 128
SGU_GROUPS = 8
SGU_GROUP_DIM = 64
SGU_WIDTH = SGU_GROUPS * SGU_GROUP_DIM
FNET_GROUPS = 4
FNET_GROUP_DIM = 128
FNET_WIDTH = FNET_GROUPS * FNET_GROUP_DIM
N_BRANCH = 3
N_GROUPS = 4
EXPERTS_PER_GROUP = 8
N_EXPERTS = N_GROUPS * EXPERTS_PER_GROUP
D_EXPERT = 256
EPS = 1e-6

LANES = 128
HEAD_SLAB = LANES

O_CQ = 0
O_CKV = O_CQ + Q_LORA
O_KR = O_CKV + KV_LORA
O_SGU = O_KR + HEAD_SLAB
O_FN = O_SGU + 2 * SGU_WIDTH
O_GATE = O_FN + FNET_WIDTH
N_IN_EXT = O_GATE + N_BRANCH * D_MODEL

V7X_VMEM_BYTES = 64 * 1024 * 1024


def _cparams(dims, vmem_mb):
    return pltpu.CompilerParams(dimension_semantics=dims,
                                vmem_limit_bytes=min(vmem_mb * 1024 * 1024, V7X_VMEM_BYTES - (6 << 20)))


def _const_spec(shape):
    nd = len(shape)
    return pl.BlockSpec(shape, lambda *_: (0,) * nd)


def _inproj_kernel(x_ref, g_ref, w_ref, qan_ref, kvan_ref, lng_ref, lnb_ref, cs_ref,
                   cq_ref, ckv_ref, kr_ref, u_ref, v_ref, fa_ref, fb_ref, gate_ref):
    x = x_ref[...]
    h = (x * lax.rsqrt(jnp.mean(x * x, axis=-1, keepdims=True) + EPS) * g_ref[...]).astype(BF16)

    def seg(a, n):
        return jnp.dot(h, w_ref[:, a:a + n], preferred_element_type=F32)

    cq = seg(O_CQ, Q_LORA)
    cq_ref[...] = (cq * lax.rsqrt(jnp.mean(cq * cq, axis=-1, keepdims=True) + EPS) * qan_ref[...]).astype(BF16)
    ckv = seg(O_CKV, KV_LORA)
    ckv_ref[...] = (ckv * lax.rsqrt(jnp.mean(ckv * ckv, axis=-1, keepdims=True) + EPS) * kvan_ref[...]).astype(BF16)
    kr_ref[...] = seg(O_KR, HEAD_SLAB).astype(BF16)

    u = jax.nn.gelu(seg(O_SGU, SGU_WIDTH))
    u_ref[...] = u.astype(BF16)
    v = jax.nn.gelu(seg(O_SGU + SGU_WIDTH, SGU_WIDTH))
    mu = jnp.mean(v, axis=-1, keepdims=True)
    vc = v - mu
    var = jnp.mean(vc * vc, axis=-1, keepdims=True)
    v_ref[...] = (vc * lax.rsqrt(var + EPS) * lng_ref[...] + lnb_ref[...]).astype(BF16)

    zf = seg(O_FN, FNET_WIDTH).astype(BF16)
    for g in range(FNET_GROUPS):
        sl = slice(g * FNET_GROUP_DIM, (g + 1) * FNET_GROUP_DIM)
        ab = jnp.dot(zf[:, sl], cs_ref[...], preferred_element_type=F32)
        fa_ref[:, sl] = ab[:, :FNET_GROUP_DIM].astype(BF16)
        fb_ref[:, sl] = ab[:, FNET_GROUP_DIM:].astype(BF16)

    for j in range(N_BRANCH):
        gl = seg(O_GATE + j * D_MODEL, D_MODEL)
        gate_ref[:, j * D_MODEL:(j + 1) * D_MODEL] = (1.0 / (1.0 + jnp.exp(-gl))).astype(BF16)


def _inproj(x2, g, w_ext, qan, kvan, lng, lnb, cs, batch, seq, tm):
    T = x2.shape[0]
    n_s = seq // tm
    row = lambda n: pl.BlockSpec((tm, n), lambda i: (i, 0))
    sb = pl.BlockSpec((tm, FNET_WIDTH), lambda i: (i % n_s, i // n_s))
    outs = [
        jax.ShapeDtypeStruct((T, Q_LORA), BF16),
        jax.ShapeDtypeStruct((T, KV_LORA), BF16),
        jax.ShapeDtypeStruct((T, HEAD_SLAB), BF16),
        jax.ShapeDtypeStruct((T, SGU_WIDTH), BF16),
        jax.ShapeDtypeStruct((T, SGU_WIDTH), BF16),
        jax.ShapeDtypeStruct((seq, batch * FNET_WIDTH), BF16),
        jax.ShapeDtypeStruct((seq, batch * FNET_WIDTH), BF16),
        jax.ShapeDtypeStruct((T, N_BRANCH * D_MODEL), BF16),
    ]
    return pl.pallas_call(
        _inproj_kernel,
        grid=(T // tm,),
        in_specs=[row(D_MODEL), _const_spec((1, D_MODEL)), _const_spec((D_MODEL, N_IN_EXT)),
                  _const_spec((1, Q_LORA)), _const_spec((1, KV_LORA)),
                  _const_spec((1, SGU_WIDTH)), _const_spec((1, SGU_WIDTH)),
                  _const_spec((FNET_GROUP_DIM, 2 * FNET_GROUP_DIM))],
        out_specs=[row(Q_LORA), row(KV_LORA), row(HEAD_SLAB), row(SGU_WIDTH), row(SGU_WIDTH),
                   sb, sb, row(N_BRANCH * D_MODEL)],
        out_shape=outs,
        compiler_params=_cparams(("arbitrary",), 52),
        name="inproj",
    )(x2, g, w_ext, qan, kvan, lng, lnb, cs)


def _head_norm_rope(y, gain, ct, st, keep):
    ss = jnp.sum(jnp.where(keep, y * y, 0.0), axis=-1, keepdims=True)
    y = y * lax.rsqrt(ss * (1.0 / QK_HEAD) + EPS) * gain
    return y * ct + pltpu.roll(y, LANES - QK_ROPE, 1) * st


def _mla_prep_kernel(cq_ref, ckv_ref, kr_ref, ct_ref, st_ref, wq_ref, wk_ref, wv_ref, gq_ref, gk_ref,
                     q_ref, k_ref, v_ref):
    cq = cq_ref[...]
    ckv = ckv_ref[...]
    kr = kr_ref[...].astype(F32)
    ct = ct_ref[...]
    st = st_ref[...]
    lane = lax.broadcasted_iota(jnp.int32, ct.shape, 1)
    keep = lane < QK_HEAD
    is_v = lane < V_HEAD
    scale = QK_HEAD ** -0.5
    for h in range(N_HEADS):
        sl = slice(h * HEAD_SLAB, (h + 1) * HEAD_SLAB)
        qh = jnp.dot(cq, wq_ref[:, sl], preferred_element_type=F32)
        q_ref[0, h] = (_head_norm_rope(qh, gq_ref[...], ct, st, keep) * scale).astype(BF16)
        kh = jnp.dot(ckv, wk_ref[:, sl], preferred_element_type=F32) + kr
        k_ref[0, h] = _head_norm_rope(kh, gk_ref[...], ct, st, keep).astype(BF16)
        vh = jnp.dot(ckv, wv_ref[:, sl], preferred_element_type=F32)
        v_ref[0, h] = jnp.where(is_v, vh, 1.0).astype(BF16)


def _mla_prep(cqn, ckvn, kr, ctab, stab, wq, wk, wv, gq, gk, batch, seq, tm):
    T = cqn.shape[0]
    n_s = seq // tm
    row = lambda n: pl.BlockSpec((tm, n), lambda i: (i, 0))
    hb = pl.BlockSpec((1, N_HEADS, tm, HEAD_SLAB), lambda i: (i // n_s, 0, i % n_s, 0))
    shp = jax.ShapeDtypeStruct((batch, N_HEADS, seq, HEAD_SLAB), BF16)
    return pl.pallas_call(
        _mla_prep_kernel,
        grid=(T // tm,),
        in_specs=[row(Q_LORA), row(KV_LORA), row(HEAD_SLAB), row(HEAD_SLAB), row(HEAD_SLAB),
                  _const_spec((Q_LORA, N_HEADS * HEAD_SLAB)), _const_spec((KV_LORA, N_HEADS * HEAD_SLAB)),
                  _const_spec((KV_LORA, N_HEADS * HEAD_SLAB)),
                  _const_spec((1, HEAD_SLAB)), _const_spec((1, HEAD_SLAB))],
        out_specs=[hb, hb, hb],
        out_shape=[shp, shp, shp],
        compiler_params=_cparams(("arbitrary",), 32),
        name="mla_prep",
    )(cqn, ckvn, kr, ctab, stab, wq, wk, wv, gq, gk)


HEADS_PER_STEP = LANES // V_HEAD


def _attn_kernel(q_ref, k_ref, v_ref, o_ref):
    tq = q_ref.shape[2]
    lane = lax.broadcasted_iota(jnp.int32, (tq, LANES), 1)
    low = lane < V_HEAD
    outs = []
    for j in range(HEADS_PER_STEP):
        s = lax.dot_general(q_ref[0, j], k_ref[0, j], (((1,), (1,)), ((), ())),
                            preferred_element_type=F32)
        m = jnp.max(s, axis=-1, keepdims=True)
        p = jnp.exp(s - m).astype(BF16)
        oe = jnp.dot(p, v_ref[0, j], preferred_element_type=F32)
        den = jnp.where(low, pltpu.roll(oe, V_HEAD, 1), 1.0)
        outs.append(oe / den)
    o_ref[0] = jnp.where(low, outs[0], pltpu.roll(outs[1], V_HEAD, 1)).astype(BF16)


def _attention(q, k, v, tq):
    batch, _, seq, _ = q.shape
    n_hp = N_HEADS // HEADS_PER_STEP
    return pl.pallas_call(
        _attn_kernel,
        grid=(batch, n_hp, seq // tq),
        in_specs=[pl.BlockSpec((1, HEADS_PER_STEP, tq, HEAD_SLAB), lambda b, h, i: (b, h, i, 0)),
                  pl.BlockSpec((1, HEADS_PER_STEP, seq, HEAD_SLAB), lambda b, h, i: (b, h, 0, 0)),
                  pl.BlockSpec((1, HEADS_PER_STEP, seq, HEAD_SLAB), lambda b, h, i: (b, h, 0, 0))],
        out_specs=pl.BlockSpec((1, tq, LANES), lambda b, h, i: (b, i, h)),
        out_shape=jax.ShapeDtypeStruct((batch, seq, N_HEADS * V_HEAD), BF16),
        compiler_params=_cparams(("arbitrary", "arbitrary", "arbitrary"), 48),
        name="attention",
    )(q, k, v)


def _dft_kernel(c_ref, s_ref, a_ref, b_ref, o_ref, acc_ref, *, scale):
    kk = pl.program_id(2)

    @pl.when(kk == 0)
    def _():
        acc_ref[...] = jnp.zeros_like(acc_ref)

    acc_ref[...] += (jnp.dot(c_ref[...], a_ref[...], preferred_element_type=F32)
                     + jnp.dot(s_ref[...], b_ref[...], preferred_element_type=F32))

    @pl.when(kk == pl.num_programs(2) - 1)
    def _():
        o_ref[...] = (acc_ref[...] * scale).astype(BF16)


def _seq_dft(ctab, sneg, fa, fb, tm, tn, tk):
    seq, width = fa.shape
    scale = 1.0 / math.sqrt(seq * FNET_GROUP_DIM)
    return pl.pallas_call(
        functools.partial(_dft_kernel, scale=scale),
        grid=(seq // tm, width // tn, seq // tk),
        in_specs=[pl.BlockSpec((tm, tk), lambda i, j, k: (i, k)),
                  pl.BlockSpec((tm, tk), lambda i, j, k: (i, k)),
                  pl.BlockSpec((tk, tn), lambda i, j, k: (k, j)),
                  pl.BlockSpec((tk, tn), lambda i, j, k: (k, j))],
        out_specs=pl.BlockSpec((tm, tn), lambda i, j, k: (i, j)),
        out_shape=jax.ShapeDtypeStruct((seq, width), BF16),
        scratch_shapes=[pltpu.VMEM((tm, tn), F32)],
        compiler_params=_cparams(("arbitrary", "arbitrary", "arbitrary"), 40),
        name="seq_dft",
    )(ctab, sneg, fa, fb)


def _merge_kernel(x_ref, ya_ref, u_ref, v_ref, yf_ref, gate_ref, ws_ref, bias_ref,
                  wpa_ref, wps_ref, wpf_ref, wo_ref, o_ref, ysgu_ref):
    tm = x_ref.shape[0]
    lane = lax.broadcasted_iota(jnp.int32, (CHUNK, LANES), 1)
    low = lane < SGU_GROUP_DIM
    groups_per_slab = LANES // SGU_GROUP_DIM
    for c in range(tm // CHUNK):
        rows = slice(c * CHUNK, (c + 1) * CHUNK)
        for j in range(SGU_WIDTH // LANES):
            cols = slice(j * LANES, (j + 1) * LANES)
            vb = v_ref[rows, cols]
            ma = jnp.dot(ws_ref[groups_per_slab * j], vb, preferred_element_type=F32)
            mb = jnp.dot(ws_ref[groups_per_slab * j + 1], vb, preferred_element_type=F32)
            mixed = jnp.where(low, ma, mb) + bias_ref[:, cols]
            ysgu_ref[rows, cols] = (u_ref[rows, cols].astype(F32) * mixed).astype(BF16)
    pa = jnp.dot(ya_ref[...], wpa_ref[...], preferred_element_type=F32)
    merged = gate_ref[:, 0:D_MODEL].astype(F32) * pa
    ps = jnp.dot(ysgu_ref[...], wps_ref[...], preferred_element_type=F32)
    merged += gate_ref[:, D_MODEL:2 * D_MODEL].astype(F32) * ps
    pf = jnp.dot(yf_ref[...], wpf_ref[...], preferred_element_type=F32)
    merged += gate_ref[:, 2 * D_MODEL:3 * D_MODEL].astype(F32) * pf
    o_ref[...] = x_ref[...] + jnp.dot(merged.astype(BF16), wo_ref[...], preferred_element_type=F32)


def _merge(x2, ya, u, v, yf, gates, ws, bias, wpa, wps, wpf, wo, seq, tm):
    T = x2.shape[0]
    n_s = seq // tm
    row = lambda n: pl.BlockSpec((tm, n), lambda i: (i, 0))
    return pl.pallas_call(
        _merge_kernel,
        grid=(T // tm,),
        in_specs=[row(D_MODEL), row(N_HEADS * V_HEAD), row(SGU_WIDTH), row(SGU_WIDTH),
                  pl.BlockSpec((tm, FNET_WIDTH), lambda i: (i % n_s, i // n_s)),
                  row(N_BRANCH * D_MODEL),
                  _const_spec((SGU_GROUPS, CHUNK, CHUNK)), _const_spec((CHUNK, SGU_WIDTH)),
                  _const_spec((N_HEADS * V_HEAD, D_MODEL)), _const_spec((SGU_WIDTH, D_MODEL)),
                  _const_spec((FNET_WIDTH, D_MODEL)), _const_spec((D_MODEL, D_MODEL))],
        out_specs=row(D_MODEL),
        out_shape=jax.ShapeDtypeStruct((T, D_MODEL), F32),
        scratch_shapes=[pltpu.VMEM((tm, SGU_WIDTH), BF16)],
        compiler_params=_cparams(("arbitrary",), 48),
        name="merge",
    )(x2, ya, u, v, yf, gates, ws, bias, wpa, wps, wpf, wo)


O_GLOGIT = N_EXPERTS


def _route_kernel(x_ref, g_ref, whi_ref, wlo_ref, b_ref, h_ref, cw_ref):
    x = x_ref[...]
    h = x * lax.rsqrt(jnp.mean(x * x, axis=-1, keepdims=True) + EPS) * g_ref[...]
    hhi = h.astype(BF16)
    hlo = (h - hhi.astype(F32)).astype(BF16)
    h_ref[...] = hhi
    logits = (jnp.dot(hhi, whi_ref[...], preferred_element_type=F32)
              + jnp.dot(hlo, whi_ref[...], preferred_element_type=F32)
              + jnp.dot(hhi, wlo_ref[...], preferred_element_type=F32)) + b_ref[...]
    lane = lax.broadcasted_iota(jnp.int32, logits.shape, 1)
    lane_f = lane.astype(F32)
    neg = -jnp.inf
    far = float(2 * LANES)

    is_g = (lane >= O_GLOGIT) & (lane < O_GLOGIT + N_GROUPS)
    gl = jnp.where(is_g, logits, neg)
    gmax = jnp.max(gl, axis=-1, keepdims=True)
    gsel = jnp.min(jnp.where(gl == gmax, lane_f, far), axis=-1, keepdims=True) - float(O_GLOGIT)
    gw = 1.0 / jnp.sum(jnp.where(is_g, jnp.exp(gl - gmax), 0.0), axis=-1, keepdims=True)

    lane_grp = lax.shift_right_logical(lane, 3).astype(F32)
    in_grp = (lane < N_EXPERTS) & (lane_grp == gsel)
    el = jnp.where(in_grp, logits, neg)
    e1 = jnp.max(el, axis=-1, keepdims=True)
    i1 = jnp.min(jnp.where(el == e1, lane_f, far), axis=-1, keepdims=True)
    el2 = jnp.where(lane_f == i1, neg, el)
    e2 = jnp.max(el2, axis=-1, keepdims=True)
    i2 = jnp.min(jnp.where(el2 == e2, lane_f, far), axis=-1, keepdims=True)
    d = jnp.exp(e2 - e1)
    w1 = gw / (1.0 + d)
    w2 = w1 * d
    cw_ref[...] = jnp.where(lane_f == i1, w1, jnp.where(lane_f == i2, w2, 0.0))


def _route(x2, g, whi, wlo, b, tm):
    T = x2.shape[0]
    row = lambda n: pl.BlockSpec((tm, n), lambda i: (i, 0))
    return pl.pallas_call(
        _route_kernel,
        grid=(T // tm,),
        in_specs=[row(D_MODEL), _const_spec((1, D_MODEL)), _const_spec((D_MODEL, LANES)),
                  _const_spec((D_MODEL, LANES)), _const_spec((1, LANES))],
        out_specs=[row(D_MODEL), row(LANES)],
        out_shape=[jax.ShapeDtypeStruct((T, D_MODEL), BF16), jax.ShapeDtypeStruct((T, LANES), F32)],
        compiler_params=_cparams(("arbitrary",), 32),
        name="route",
    )(x2, g, whi, wlo, b)


def _moe_kernel(x_ref, h_ref, cw_ref, w1_ref, w3_ref, w2_ref, o_ref, acc_ref):
    e = pl.program_id(1)

    @pl.when(e == 0)
    def _():
        acc_ref[...] = jnp.zeros_like(acc_ref)

    h = h_ref[...]
    a = jnp.dot(h, w1_ref[0], preferred_element_type=F32)
    b = jnp.dot(h, w3_ref[0], preferred_element_type=F32)
    cw = cw_ref[...]
    lane = lax.broadcasted_iota(jnp.int32, cw.shape, 1)
    col = jnp.sum(jnp.where(lane == e, cw, 0.0), axis=-1, keepdims=True)
    hid = (a / (1.0 + jnp.exp(-a))) * b * col
    acc_ref[...] += jnp.dot(hid.astype(BF16), w2_ref[0], preferred_element_type=F32)

    @pl.when(e == pl.num_programs(1) - 1)
    def _():
        o_ref[...] = x_ref[...] + acc_ref[...]


def _moe(x2, h, cw, w1, w3, w2, tm):
    T = x2.shape[0]
    row = lambda n: pl.BlockSpec((tm, n), lambda i, e: (i, 0))
    return pl.pallas_call(
        _moe_kernel,
        grid=(T // tm, N_EXPERTS),
        in_specs=[row(D_MODEL), row(D_MODEL), row(LANES),
                  pl.BlockSpec((1, D_MODEL, D_EXPERT), lambda i, e: (e, 0, 0)),
                  pl.BlockSpec((1, D_MODEL, D_EXPERT), lambda i, e: (e, 0, 0)),
                  pl.BlockSpec((1, D_EXPERT, D_MODEL), lambda i, e: (e, 0, 0))],
        out_specs=row(D_MODEL),
        out_shape=jax.ShapeDtypeStruct((T, D_MODEL), F32),
        scratch_shapes=[pltpu.VMEM((tm, D_MODEL), F32)],
        compiler_params=_cparams(("arbitrary", "arbitrary"), 48),
        name="moe",
    )(x2, h, cw, w1, w3, w2)


def _rope_swap(w):
    half = QK_ROPE // 2
    return jnp.concatenate([w, w[..., half:], w[..., :half]], axis=-1)


def _head_gain(g):
    return jnp.concatenate([g[:QK_NOPE], _rope_swap(g[QK_NOPE:])])[None, :].astype(F32)


def _layer_params(l, attn_norm, w_in, q_a_norm, w_uq, kv_a_norm, w_ukv, q_norm, k_norm, sgu_ln_g, sgu_ln_b,
                  sgu_w, sgu_b, w_proj_attn, w_proj_sgu, w_proj_fnet, w_out, ffn_norm, w_group, b_group,
                  w_router, b_router, w1, w3, w2):
    wi = w_in[l]
    c0 = Q_LORA + KV_LORA
    c1 = c0 + QK_ROPE
    w_ext = jnp.concatenate(
        [wi[:, :c0], jnp.zeros((D_MODEL, QK_NOPE), F32), _rope_swap(wi[:, c0:c1]), wi[:, c1:]], axis=1).astype(BF16)
    wq = w_uq[l].reshape(Q_LORA, N_HEADS, QK_HEAD)
    wq = jnp.concatenate([wq[..., :QK_NOPE], _rope_swap(wq[..., QK_NOPE:])], axis=-1)
    wq = wq.reshape(Q_LORA, N_HEADS * HEAD_SLAB).astype(BF16)
    wkv = w_ukv[l].reshape(KV_LORA, N_HEADS, QK_NOPE + V_HEAD)
    zpad = jnp.zeros((KV_LORA, N_HEADS, HEAD_SLAB - QK_NOPE), F32)
    wk = jnp.concatenate([wkv[..., :QK_NOPE], zpad], axis=-1).reshape(KV_LORA, N_HEADS * HEAD_SLAB).astype(BF16)
    wv = jnp.concatenate([wkv[..., QK_NOPE:], zpad], axis=-1).reshape(KV_LORA, N_HEADS * HEAD_SLAB).astype(BF16)
    w_rg = jnp.concatenate([w_router[l], w_group[l], jnp.zeros((D_MODEL, LANES - N_EXPERTS - N_GROUPS), F32)], axis=1)
    w_rg_hi = w_rg.astype(BF16)
    w_rg_lo = (w_rg - w_rg_hi.astype(F32)).astype(BF16)
    b_rg = jnp.concatenate([b_router[l], b_group[l], jnp.zeros((LANES - N_EXPERTS - N_GROUPS,), F32)])[None, :]
    return dict(
        attn_norm=attn_norm[l][None, :], w_ext=w_ext,
        q_a_norm=q_a_norm[l][None, :], kv_a_norm=kv_a_norm[l][None, :],
        ln_g=sgu_ln_g[l][None, :], ln_b=sgu_ln_b[l][None, :],
        wq=wq, wk=wk, wv=wv, gq=_head_gain(q_norm[l]), gk=_head_gain(k_norm[l]),
        sgu_w=sgu_w[l].astype(BF16), sgu_bias=jnp.repeat(sgu_b[l].T, SGU_GROUP_DIM, axis=1),
        wpa=w_proj_attn[l].astype(BF16), wps=w_proj_sgu[l].astype(BF16), wpf=w_proj_fnet[l].astype(BF16),
        wo=w_out[l].astype(BF16),
        ffn_norm=ffn_norm[l][None, :], w_rg_hi=w_rg_hi, w_rg_lo=w_rg_lo, b_rg=b_rg,
        w1=w1[l].astype(BF16), w3=w3[l].astype(BF16), w2=w2[l].astype(BF16),
    )


def _rope_slabs(positions):
    inv = 1.0 / (ROPE_THETA ** (jnp.arange(0, QK_ROPE, 2, dtype=F32) / QK_ROPE))
    ang = positions.astype(F32).reshape(-1)[:, None] * inv
    cos, sin = jnp.cos(ang), jnp.sin(ang)
    T = ang.shape[0]
    ctab = jnp.concatenate([jnp.ones((T, QK_NOPE), F32), cos, cos, jnp.zeros((T, QK_ROPE), F32)], axis=1)
    stab = jnp.concatenate([jnp.zeros((T, QK_NOPE), F32), -sin, sin, jnp.zeros((T, QK_ROPE), F32)], axis=1)
    return ctab, stab


def _dft_tables(n):
    idx = jnp.arange(n, dtype=jnp.int32)
    kn = (idx[:, None] * idx[None, :]) & (n - 1)
    ang = kn.astype(F32) * (2.0 * math.pi / n)
    c = jnp.cos(ang).astype(BF16)
    sneg = (-jnp.sin(ang)).astype(BF16)
    m = FNET_GROUP_DIM
    jk = np.outer(np.arange(m), np.arange(m)) % m
    angc = jk.astype(np.float64) * (2.0 * math.pi / m)
    cs = jnp.asarray(np.concatenate([np.cos(angc), np.sin(angc)], axis=1), F32).astype(BF16)
    return c, sneg, cs


def _tile(n, want):
    t = min(n, want)
    assert n % t == 0
    return t


def kernel(x, positions, attn_norm, w_in, q_a_norm, w_uq, kv_a_norm, w_ukv, q_norm, k_norm, sgu_ln_g, sgu_ln_b,
           sgu_w, sgu_b, w_proj_attn, w_proj_sgu, w_proj_fnet, w_out, ffn_norm, w_group, b_group, w_router,
           b_router, w1, w3, w2):
    batch, seq, _ = x.shape
    depth = w_in.shape[0]
    assert seq & (seq - 1) == 0 and seq % CHUNK == 0
    T = batch * seq
    tm = _tile(seq, 512)
    ctab, stab = _rope_slabs(positions)
    dft_c, dft_sneg, cs = _dft_tables(seq)
    x2 = x.reshape(T, D_MODEL)
    for l in range(depth):
        p = _layer_params(l, attn_norm, w_in, q_a_norm, w_uq, kv_a_norm, w_ukv, q_norm, k_norm, sgu_ln_g,
                          sgu_ln_b, sgu_w, sgu_b, w_proj_attn, w_proj_sgu, w_proj_fnet, w_out, ffn_norm,
                          w_group, b_group, w_router, b_router, w1, w3, w2)
        cqn, ckvn, kr, u, v, fa, fb, gates = _inproj(
            x2, p["attn_norm"], p["w_ext"], p["q_a_norm"], p["kv_a_norm"], p["ln_g"], p["ln_b"], cs,
            batch, seq, _tile(seq, 256))
        q, k, vv = _mla_prep(cqn, ckvn, kr, ctab, stab, p["wq"], p["wk"], p["wv"], p["gq"], p["gk"],
                             batch, seq, tm)
        ya = _attention(q, k, vv, _tile(seq, 512)).reshape(T, N_HEADS * V_HEAD)
        yf = _seq_dft(dft_c, dft_sneg, fa, fb, _tile(seq, 1024), _tile(batch * FNET_WIDTH, 1024), _tile(seq, 1024))
        x2 = _merge(x2, ya, u, v, yf, gates, p["sgu_w"], p["sgu_bias"], p["wpa"], p["wps"], p["wpf"], p["wo"],
                    seq, tm)
        h2, cw = _route(x2, p["ffn_norm"], p["w_rg_hi"], p["w_rg_lo"], p["b_rg"], tm)
        x2 = _moe(x2, h2, cw, p["w1"], p["w3"], p["w2"], _tile(T, 1024))
    return x2.reshape(batch, seq, D_MODEL)
```

```python
import functools
import math

import numpy as np
import jax
import jax.numpy as jnp
from jax import lax
from jax.experimental import pallas as pl
from jax.experimental.pallas import tpu as pltpu

F32 = jnp.float32
BF16 = jnp.bfloat16

D_MODEL = 1024
N_HEADS = 8
QK_NOPE = 64
QK_ROPE = 32
QK_HEAD = QK_NOPE + QK_ROPE
V_HEAD = 64
Q_LORA = 384
KV_LORA = 256
ROPE_THETA = 10000.0
CHUNK = 128
SGU_GROUPS = 8
SGU_GROUP_DIM = 64
SGU_WIDTH = SGU_GROUPS * SGU_GROUP_DIM
FNET_GROUPS = 4
FNET_GROUP_DIM = 128
FNET_WIDTH = FNET_GROUPS * FNET_GROUP_DIM
N_BRANCH = 3
N_GROUPS = 4
EXPERTS_PER_GROUP = 8
N_EXPERTS = N_GROUPS * EXPERTS_PER_GROUP
D_EXPERT = 256
EPS = 1e-6

LANES = 128
HEAD_SLAB = LANES
O_CQ = 0
O_CKV = O_CQ + Q_LORA
O_KR = O_CKV + KV_LORA
O_SGU = O_KR + HEAD_SLAB
O_FN = O_SGU + 2 * SGU_WIDTH
O_GATE = O_FN + FNET_WIDTH
N_IN_EXT = O_GATE + N_BRANCH * D_MODEL

V7X_VMEM_BYTES = 64 * 1024 * 1024


def _cparams(dims, vmem_mb):
    return pltpu.CompilerParams(dimension_semantics=dims,
                                vmem_limit_bytes=min(vmem_mb * 1024 * 1024, V7X_VMEM_BYTES - (6 << 20)))


def _const_spec(shape):
    nd = len(shape)
    return pl.BlockSpec(shape, lambda *_: (0,) * nd)


def _inproj_kernel(x_ref, g_ref, w_ref, qan_ref, kvan_ref, lng_ref, lnb_ref, cs_ref,
                   cq_ref, ckv_ref, kr_ref, u_ref, v_ref, fa_ref, fb_ref, gate_ref):
    x = x_ref[...]
    h = (x * lax.rsqrt(jnp.mean(x * x, axis=-1, keepdims=True) + EPS) * g_ref[...]).astype(BF16)

    def seg(a, n):
        return jnp.dot(h, w_ref[:, a:a + n], preferred_element_type=F32)

    cq = seg(O_CQ, Q_LORA)
    cq_ref[...] = (cq * lax.rsqrt(jnp.mean(cq * cq, axis=-1, keepdims=True) + EPS) * qan_ref[...]).astype(BF16)
    ckv = seg(O_CKV, KV_LORA)
    ckv_ref[...] = (ckv * lax.rsqrt(jnp.mean(ckv * ckv, axis=-1, keepdims=True) + EPS) * kvan_ref[...]).astype(BF16)
    kr_ref[...] = seg(O_KR, HEAD_SLAB).astype(BF16)

    u = jax.nn.gelu(seg(O_SGU, SGU_WIDTH))
    u_ref[...] = u.astype(BF16)
    v = jax.nn.gelu(seg(O_SGU + SGU_WIDTH, SGU_WIDTH))
    mu = jnp.mean(v, axis=-1, keepdims=True)
    vc = v - mu
    var = jnp.mean(vc * vc, axis=-1, keepdims=True)
    v_ref[...] = (vc * lax.rsqrt(var + EPS) * lng_ref[...] + lnb_ref[...]).astype(BF16)

    zf = seg(O_FN, FNET_WIDTH).astype(BF16)
    for g in range(FNET_GROUPS):
        sl = slice(g * FNET_GROUP_DIM, (g + 1) * FNET_GROUP_DIM)
        ab = jnp.dot(zf[:, sl], cs_ref[...], preferred_element_type=F32)
        fa_ref[:, sl] = ab[:, :FNET_GROUP_DIM].astype(BF16)
        fb_ref[:, sl] = ab[:, FNET_GROUP_DIM:].astype(BF16)

    for j in range(N_BRANCH):
        gl = seg(O_GATE + j * D_MODEL, D_MODEL)
        gate_ref[:, j * D_MODEL:(j + 1) * D_MODEL] = (1.0 / (1.0 + jnp.exp(-gl))).astype(BF16)


def _inproj(x2, g, w_ext, qan, kvan, lng, lnb, cs, batch, seq, tm):
    T = x2.shape[0]
    n_s = seq // tm
    row = lambda n: pl.BlockSpec((tm, n), lambda i: (i, 0))
    sb = pl.BlockSpec((tm, FNET_WIDTH), lambda i: (i % n_s, i // n_s))
    outs = [
        jax.ShapeDtypeStruct((T, Q_LORA), BF16),
        jax.ShapeDtypeStruct((T, KV_LORA), BF16),
        jax.ShapeDtypeStruct((T, HEAD_SLAB), BF16),
        jax.ShapeDtypeStruct((T, SGU_WIDTH), BF16),
        jax.ShapeDtypeStruct((T, SGU_WIDTH), BF16),
        jax.ShapeDtypeStruct((seq, batch * FNET_WIDTH), BF16),
        jax.ShapeDtypeStruct((seq, batch * FNET_WIDTH), BF16),
        jax.ShapeDtypeStruct((T, N_BRANCH * D_MODEL), BF16),
    ]
    return pl.pallas_call(
        _inproj_kernel,
        grid=(T // tm,),
        in_specs=[row(D_MODEL), _const_spec((1, D_MODEL)), _const_spec((D_MODEL, N_IN_EXT)),
                  _const_spec((1, Q_LORA)), _const_spec((1, KV_LORA)),
                  _const_spec((1, SGU_WIDTH)), _const_spec((1, SGU_WIDTH)),
                  _const_spec((FNET_GROUP_DIM, 2 * FNET_GROUP_DIM))],
        out_specs=[row(Q_LORA), row(KV_LORA), row(HEAD_SLAB), row(SGU_WIDTH), row(SGU_WIDTH),
                   sb, sb, row(N_BRANCH * D_MODEL)],
        out_shape=outs,
        compiler_params=_cparams(("arbitrary",), 52),
        name="inproj",
    )(x2, g, w_ext, qan, kvan, lng, lnb, cs)


def _head_norm_rope(y, gain, ct, st, keep):
    ss = jnp.sum(jnp.where(keep, y * y, 0.0), axis=-1, keepdims=True)
    y = y * lax.rsqrt(ss * (1.0 / QK_HEAD) + EPS) * gain
    return y * ct + pltpu.roll(y, LANES - QK_ROPE, 1) * st


def _mla_prep_kernel(cq_ref, ckv_ref, kr_ref, ct_ref, st_ref, wq_ref, wk_ref, wv_ref, gq_ref, gk_ref,
                     q_ref, k_ref, v_ref):
    cq = cq_ref[...]
    ckv = ckv_ref[...]
    kr = kr_ref[...].astype(F32)
    ct = ct_ref[...]
    st = st_ref[...]
    lane = lax.broadcasted_iota(jnp.int32, ct.shape, 1)
    keep = lane < QK_HEAD
    is_v = lane < V_HEAD
    scale = QK_HEAD ** -0.5 * math.log2(math.e)
    for h in range(N_HEADS):
        sl = slice(h * HEAD_SLAB, (h + 1) * HEAD_SLAB)
        qh = jnp.dot(cq, wq_ref[:, sl], preferred_element_type=F32)
        q_ref[0, h] = (_head_norm_rope(qh, gq_ref[...], ct, st, keep) * scale).astype(BF16)
        kh = jnp.dot(ckv, wk_ref[:, sl], preferred_element_type=F32) + kr
        k_ref[0, h] = _head_norm_rope(kh, gk_ref[...], ct, st, keep).T.astype(BF16)
        vh = jnp.dot(ckv, wv_ref[:, sl], preferred_element_type=F32)
        v_ref[0, h] = jnp.where(is_v, vh, 1.0).astype(BF16)


def _mla_prep(cqn, ckvn, kr, ctab, stab, wq, wk, wv, gq, gk, batch, seq, tm):
    T = cqn.shape[0]
    n_s = seq // tm
    row = lambda n: pl.BlockSpec((tm, n), lambda i: (i, 0))
    hb = pl.BlockSpec((1, N_HEADS, tm, HEAD_SLAB), lambda i: (i // n_s, 0, i % n_s, 0))
    shp = jax.ShapeDtypeStruct((batch, N_HEADS, seq, HEAD_SLAB), BF16)
    return pl.pallas_call(
        _mla_prep_kernel,
        grid=(T // tm,),
        in_specs=[row(Q_LORA), row(KV_LORA), row(HEAD_SLAB), row(HEAD_SLAB), row(HEAD_SLAB),
                  _const_spec((Q_LORA, N_HEADS * HEAD_SLAB)), _const_spec((KV_LORA, N_HEADS * HEAD_SLAB)),
                  _const_spec((KV_LORA, N_HEADS * HEAD_SLAB)),
                  _const_spec((1, HEAD_SLAB)), _const_spec((1, HEAD_SLAB))],
        out_specs=[hb, pl.BlockSpec((1, N_HEADS, HEAD_SLAB, tm), lambda i: (i // n_s, 0, 0, i % n_s)), hb],
        out_shape=[shp, jax.ShapeDtypeStruct((batch, N_HEADS, HEAD_SLAB, seq), BF16), shp],
        compiler_params=_cparams(("arbitrary",), 32),
        name="mla_prep",
    )(cqn, ckvn, kr, ctab, stab, wq, wk, wv, gq, gk)


HEADS_PER_STEP = LANES // V_HEAD


ROW_CHAINS = 4


def _attn_kernel(q_ref, kt_ref, v_ref, o_ref):
    tq = q_ref.shape[2]
    rows_per_chain = tq // ROW_CHAINS
    lane = lax.broadcasted_iota(jnp.int32, (rows_per_chain, LANES), 1)
    low = lane < V_HEAD
    for r in range(ROW_CHAINS):
        rows = slice(r * rows_per_chain, (r + 1) * rows_per_chain)
        outs = []
        for j in range(HEADS_PER_STEP):
            s = jnp.dot(q_ref[0, j, rows, :], kt_ref[0, j], preferred_element_type=F32)
            m = jnp.max(s, axis=-1, keepdims=True)
            p = jnp.exp2((s - m).astype(BF16))
            oe = jnp.dot(p, v_ref[0, j], preferred_element_type=F32)
            den = jnp.where(low, pltpu.roll(oe, V_HEAD, 1), 1.0)
            outs.append(oe / den)
        o_ref[0, rows, :] = jnp.where(low, outs[0], pltpu.roll(outs[1], V_HEAD, 1)).astype(BF16)


def _attention(q, k, v, tq):
    batch, _, seq, _ = q.shape
    n_hp = N_HEADS // HEADS_PER_STEP
    return pl.pallas_call(
        _attn_kernel,
        grid=(batch, n_hp, seq // tq),
        in_specs=[pl.BlockSpec((1, HEADS_PER_STEP, tq, HEAD_SLAB), lambda b, h, i: (b, h, i, 0)),
                  pl.BlockSpec((1, HEADS_PER_STEP, HEAD_SLAB, seq), lambda b, h, i: (b, h, 0, 0)),
                  pl.BlockSpec((1, HEADS_PER_STEP, seq, HEAD_SLAB), lambda b, h, i: (b, h, 0, 0))],
        out_specs=pl.BlockSpec((1, tq, LANES), lambda b, h, i: (b, i, h)),
        out_shape=jax.ShapeDtypeStruct((batch, seq, N_HEADS * V_HEAD), BF16),
        compiler_params=_cparams(("arbitrary", "arbitrary", "arbitrary"), 48),
        name="attention",
    )(q, k, v)


def _dft_kernel(c_ref, s_ref, a_ref, b_ref, o_ref, acc_ref, *, scale):
    kk = pl.program_id(2)

    @pl.when(kk == 0)
    def _():
        acc_ref[...] = jnp.zeros_like(acc_ref)

    acc_ref[...] += (jnp.dot(c_ref[...], a_ref[...], preferred_element_type=F32)
                     + jnp.dot(s_ref[...], b_ref[...], preferred_element_type=F32))

    @pl.when(kk == pl.num_programs(2) - 1)
    def _():
        o_ref[...] = (acc_ref[...] * scale).astype(BF16)


def _seq_dft(ctab, sneg, fa, fb, tm, tn, tk):
    seq, width = fa.shape
    scale = 1.0 / math.sqrt(seq * FNET_GROUP_DIM)
    return pl.pallas_call(
        functools.partial(_dft_kernel, scale=scale),
        grid=(seq // tm, width // tn, seq // tk),
        in_specs=[pl.BlockSpec((tm, tk), lambda i, j, k: (i, k)),
                  pl.BlockSpec((tm, tk), lambda i, j, k: (i, k)),
                  pl.BlockSpec((tk, tn), lambda i, j, k: (k, j)),
                  pl.BlockSpec((tk, tn), lambda i, j, k: (k, j))],
        out_specs=pl.BlockSpec((tm, tn), lambda i, j, k: (i, j)),
        out_shape=jax.ShapeDtypeStruct((seq, width), BF16),
        scratch_shapes=[pltpu.VMEM((tm, tn), F32)],
        compiler_params=_cparams(("arbitrary", "arbitrary", "arbitrary"), 40),
        name="seq_dft",
    )(ctab, sneg, fa, fb)


def _merge_kernel(x_ref, ya_ref, u_ref, v_ref, yf_ref, gate_ref, ws_ref, bias_ref,
                  wpa_ref, wps_ref, wpf_ref, wo_ref, o_ref, ysgu_ref):
    tm = x_ref.shape[0]
    lane = lax.broadcasted_iota(jnp.int32, (CHUNK, LANES), 1)
    low = lane < SGU_GROUP_DIM
    groups_per_slab = LANES // SGU_GROUP_DIM
    for c in range(tm // CHUNK):
        rows = slice(c * CHUNK, (c + 1) * CHUNK)
        for j in range(SGU_WIDTH // LANES):
            cols = slice(j * LANES, (j + 1) * LANES)
            vb = v_ref[rows, cols]
            ma = jnp.dot(ws_ref[groups_per_slab * j], vb, preferred_element_type=F32)
            mb = jnp.dot(ws_ref[groups_per_slab * j + 1], vb, preferred_element_type=F32)
            mixed = jnp.where(low, ma, mb) + bias_ref[:, cols]
            ysgu_ref[rows, cols] = (u_ref[rows, cols].astype(F32) * mixed).astype(BF16)
    pa = jnp.dot(ya_ref[...], wpa_ref[...], preferred_element_type=F32)
    merged = gate_ref[:, 0:D_MODEL].astype(F32) * pa
    ps = jnp.dot(ysgu_ref[...], wps_ref[...], preferred_element_type=F32)
    merged += gate_ref[:, D_MODEL:2 * D_MODEL].astype(F32) * ps
    pf = jnp.dot(yf_ref[...], wpf_ref[...], preferred_element_type=F32)
    merged += gate_ref[:, 2 * D_MODEL:3 * D_MODEL].astype(F32) * pf
    o_ref[...] = x_ref[...] + jnp.dot(merged.astype(BF16), wo_ref[...], preferred_element_type=F32)


def _merge(x2, ya, u, v, yf, gates, ws, bias, wpa, wps, wpf, wo, seq, tm):
    T = x2.shape[0]
    n_s = seq // tm
    row = lambda n: pl.BlockSpec((tm, n), lambda i: (i, 0))
    return pl.pallas_call(
        _merge_kernel,
        grid=(T // tm,),
        in_specs=[row(D_MODEL), row(N_HEADS * V_HEAD), row(SGU_WIDTH), row(SGU_WIDTH),
                  pl.BlockSpec((tm, FNET_WIDTH), lambda i: (i % n_s, i // n_s)),
                  row(N_BRANCH * D_MODEL),
                  _const_spec((SGU_GROUPS, CHUNK, CHUNK)), _const_spec((CHUNK, SGU_WIDTH)),
                  _const_spec((N_HEADS * V_HEAD, D_MODEL)), _const_spec((SGU_WIDTH, D_MODEL)),
                  _const_spec((FNET_WIDTH, D_MODEL)), _const_spec((D_MODEL, D_MODEL))],
        out_specs=row(D_MODEL),
        out_shape=jax.ShapeDtypeStruct((T, D_MODEL), F32),
        scratch_shapes=[pltpu.VMEM((tm, SGU_WIDTH), BF16)],
        compiler_params=_cparams(("arbitrary",), 48),
        name="merge",
    )(x2, ya, u, v, yf, gates, ws, bias, wpa, wps, wpf, wo)


O_GLOGIT = N_EXPERTS


META_ROWS = 8
ROW_EXT = D_MODEL + LANES


def _route_kernel(x_ref, g_ref, whi_ref, wlo_ref, b_ref, utri_ref, xr_ref, meta_ref, cnt_ref):
    x = x_ref[...]
    tm = x.shape[0]
    h = x * lax.rsqrt(jnp.mean(x * x, axis=-1, keepdims=True) + EPS) * g_ref[...]
    hhi = h.astype(BF16)
    hlo = (h - hhi.astype(F32)).astype(BF16)
    logits = (jnp.dot(hhi, whi_ref[...], preferred_element_type=F32)
              + jnp.dot(hlo, whi_ref[...], preferred_element_type=F32)
              + jnp.dot(hhi, wlo_ref[...], preferred_element_type=F32)) + b_ref[...]
    lane = lax.broadcasted_iota(jnp.int32, logits.shape, 1)
    lane_f = lane.astype(F32)
    neg = -jnp.inf
    far = float(2 * LANES)

    is_g = (lane >= O_GLOGIT) & (lane < O_GLOGIT + N_GROUPS)
    gl = jnp.where(is_g, logits, neg)
    gmax = jnp.max(gl, axis=-1, keepdims=True)
    gsel = jnp.min(jnp.where(gl == gmax, lane_f, far), axis=-1, keepdims=True) - float(O_GLOGIT)
    gw = 1.0 / jnp.sum(jnp.where(is_g, jnp.exp(gl - gmax), 0.0), axis=-1, keepdims=True)

    lane_grp = lax.shift_right_logical(lane, 3).astype(F32)
    in_grp = (lane < N_EXPERTS) & (lane_grp == gsel)
    el = jnp.where(in_grp, logits, neg)
    e1 = jnp.max(el, axis=-1, keepdims=True)
    i1 = jnp.min(jnp.where(el == e1, lane_f, far), axis=-1, keepdims=True)
    el2 = jnp.where(lane_f == i1, neg, el)
    e2 = jnp.max(el2, axis=-1, keepdims=True)
    i2 = jnp.min(jnp.where(el2 == e2, lane_f, far), axis=-1, keepdims=True)
    d = jnp.exp(e2 - e1)
    w1 = gw / (1.0 + d)
    w2 = w1 * d
    xr_ref[:, :D_MODEL] = x
    xr_ref[:, D_MODEL:] = jnp.where(lane_f == i1, w1, jnp.where(lane_f == i2, w2, 0.0))

    onehot_t = jnp.where((lane_f == gsel) & (lane < N_GROUPS), 1.0, 0.0).T[0:META_ROWS, :]
    before = jnp.dot(onehot_t, utri_ref[...], preferred_element_type=F32)
    gid = lax.broadcasted_iota(jnp.int32, (META_ROWS, tm), 0).astype(F32)
    gsel_t = jnp.sum(onehot_t * gid, axis=0, keepdims=True)
    rank_t = jnp.sum(onehot_t * before, axis=0, keepdims=True)
    row_id = lax.broadcasted_iota(jnp.int32, (META_ROWS, tm), 0)
    meta_ref[0] = jnp.where(row_id == 0, gsel_t, jnp.where(row_id == 1, rank_t, 0.0)).astype(jnp.int32)
    cnt = jnp.sum(onehot_t, axis=1, keepdims=True)
    cnt_ref[0] = jnp.broadcast_to(cnt, (META_ROWS, LANES)).astype(jnp.int32)


def _route(x2, g, whi, wlo, b, utri, tm):
    T = x2.shape[0]
    n_tiles = T // tm
    row = lambda n: pl.BlockSpec((tm, n), lambda i: (i, 0))
    return pl.pallas_call(
        _route_kernel,
        grid=(n_tiles,),
        in_specs=[row(D_MODEL), _const_spec((1, D_MODEL)), _const_spec((D_MODEL, LANES)),
                  _const_spec((D_MODEL, LANES)), _const_spec((1, LANES)), _const_spec((tm, tm))],
        out_specs=[row(ROW_EXT), pl.BlockSpec((1, META_ROWS, tm), lambda i: (i, 0, 0)),
                   pl.BlockSpec((1, META_ROWS, LANES), lambda i: (i, 0, 0))],
        out_shape=[jax.ShapeDtypeStruct((T, ROW_EXT), F32),
                   jax.ShapeDtypeStruct((n_tiles, META_ROWS, tm), jnp.int32),
                   jax.ShapeDtypeStruct((n_tiles, META_ROWS, LANES), jnp.int32)],
        compiler_params=_cparams(("arbitrary",), 40),
        name="route",
    )(x2, g, whi, wlo, b, utri)


TT_GROUP, TT_USED, TT_PAD_LO, TT_PAD_HI = 0, 1, 2, 3


def _assign_kernel(cnt_ref, meta_ref, dest_ref, tt_ref, *, tmx):
    n_tiles, _, tm = meta_ref.shape
    shift = tmx.bit_length() - 1
    cnt = cnt_ref[...]
    tot = jnp.sum(cnt, axis=0)
    padded = lax.shift_left(lax.shift_right_logical(tot + (tmx - 1), shift), shift)
    starts, ends = [], []
    run = jnp.zeros((1, LANES), jnp.int32)
    for g in range(N_GROUPS):
        starts.append(run)
        run = run + padded[g:g + 1, :]
        ends.append(run)

    def widen(v):
        return jnp.concatenate([v] * (tm // LANES), axis=1)

    def body(i, seen):
        gs = meta_ref[i, 0:1, :]
        dest = meta_ref[i, 1:2, :]
        for g in range(N_GROUPS):
            dest = dest + jnp.where(gs == g, widen(starts[g] + seen[g]), 0)
        dest_ref[i] = dest
        c = cnt_ref[i]
        return tuple(seen[g] + c[g:g + 1, :] for g in range(N_GROUPS))

    lax.fori_loop(0, n_tiles, body, tuple(jnp.zeros((1, LANES), jnp.int32) for _ in range(N_GROUPS)))

    lane = lax.broadcasted_iota(jnp.int32, (1, LANES), 1)
    first_row = lane * tmx
    grp = jnp.zeros((1, LANES), jnp.int32)
    for g in range(N_GROUPS - 1):
        grp = grp + jnp.where(first_row >= ends[g], 1, 0)
    used = lax.shift_right_logical(ends[-1], shift)
    pad_lo = jnp.zeros((1, LANES), jnp.int32)
    pad_hi = jnp.zeros((1, LANES), jnp.int32)
    for g in range(N_GROUPS):
        pad_lo = pad_lo + jnp.where(lane == g, starts[g] + tot[g:g + 1, :], 0)
        pad_hi = pad_hi + jnp.where(lane == g, ends[g], 0)
    n_rows = (n_tiles * tm // tmx + N_GROUPS) * tmx
    pad_lo = pad_lo + jnp.where(lane == N_GROUPS, ends[-1], 0)
    pad_hi = pad_hi + jnp.where(lane == N_GROUPS, n_rows, 0)
    row_id = lax.broadcasted_iota(jnp.int32, (META_ROWS, LANES), 0)
    tt_ref[...] = jnp.where(row_id == TT_GROUP, grp,
                            jnp.where(row_id == TT_USED, used,
                                      jnp.where(row_id == TT_PAD_LO, pad_lo,
                                                jnp.where(row_id == TT_PAD_HI, pad_hi, 0))))


def _assign(cnt, meta, tmx):
    n_tiles, _, tm = meta.shape
    return pl.pallas_call(
        functools.partial(_assign_kernel, tmx=tmx),
        grid=(1,),
        in_specs=[_const_spec(cnt.shape), _const_spec(meta.shape)],
        out_specs=[_const_spec((n_tiles, 1, tm)), _const_spec((META_ROWS, LANES))],
        out_shape=[jax.ShapeDtypeStruct((n_tiles, 1, tm), jnp.int32),
                   jax.ShapeDtypeStruct((META_ROWS, LANES), jnp.int32)],
        compiler_params=_cparams(("arbitrary",), 32),
        name="assign",
    )(cnt, meta)


def _dispatch_kernel(dest_ref, tt_ref, xr_hbm, zero_hbm, xs_hbm, sem, *, tm):
    base = pl.program_id(0) * tm

    def issue(r, c):
        pltpu.make_async_copy(xr_hbm.at[pl.ds(base + r, 1)], xs_hbm.at[pl.ds(dest_ref[base + r], 1)], sem).start()
        return c

    lax.fori_loop(0, tm, issue, 0, unroll=8)
    pltpu.make_async_copy(xr_hbm.at[pl.ds(0, tm)], xs_hbm.at[pl.ds(0, tm)], sem).wait()

    @pl.when(pl.program_id(0) == 0)
    def _():
        for g in range(N_GROUPS + 1):
            lo = tt_ref[TT_PAD_LO * LANES + g]
            hi = tt_ref[TT_PAD_HI * LANES + g]

            def fill(r, c):
                pltpu.make_async_copy(zero_hbm, xs_hbm.at[pl.ds(r, 1)], sem).start()
                return c

            lax.fori_loop(lo, hi, fill, 0)

            def drain(r, c):
                pltpu.make_async_copy(zero_hbm, xs_hbm.at[pl.ds(0, 1)], sem).wait()
                return c

            lax.fori_loop(lo, hi, drain, 0)


def _dispatch(dest, tt, xr, n_rows, tm):
    T, width = xr.shape
    zero = jnp.zeros((1, width), xr.dtype)
    any_spec = pl.BlockSpec(memory_space=pl.ANY)
    return pl.pallas_call(
        functools.partial(_dispatch_kernel, tm=tm),
        grid_spec=pltpu.PrefetchScalarGridSpec(
            num_scalar_prefetch=2, grid=(T // tm,), in_specs=[any_spec, any_spec], out_specs=any_spec,
            scratch_shapes=[pltpu.SemaphoreType.DMA(())]),
        out_shape=jax.ShapeDtypeStruct((n_rows, width), xr.dtype),
        compiler_params=_cparams(("arbitrary",), 16),
        name="dispatch",
    )(dest, tt, xr, zero)


def _unsort_kernel(dest_ref, ys_hbm, y_hbm, sem, *, tm):
    base = pl.program_id(0) * tm

    def issue(r, c):
        pltpu.make_async_copy(ys_hbm.at[pl.ds(dest_ref[base + r], 1)], y_hbm.at[pl.ds(base + r, 1)], sem).start()
        return c

    lax.fori_loop(0, tm, issue, 0, unroll=8)
    pltpu.make_async_copy(ys_hbm.at[pl.ds(0, tm)], y_hbm.at[pl.ds(0, tm)], sem).wait()


def _unsort(dest, ys, T, tm):
    any_spec = pl.BlockSpec(memory_space=pl.ANY)
    return pl.pallas_call(
        functools.partial(_unsort_kernel, tm=tm),
        grid_spec=pltpu.PrefetchScalarGridSpec(
            num_scalar_prefetch=1, grid=(T // tm,), in_specs=[any_spec], out_specs=any_spec,
            scratch_shapes=[pltpu.SemaphoreType.DMA(())]),
        out_shape=jax.ShapeDtypeStruct((T, ys.shape[1]), ys.dtype),
        compiler_params=_cparams(("arbitrary",), 16),
        name="unsort",
    )(dest, ys)


def _experts_kernel(tt_ref, xs_ref, g_ref, w1_ref, w3_ref, w2_ref, o_ref):
    j = pl.program_id(0)

    @pl.when(j < tt_ref[TT_USED * LANES])
    def _():
        grp = tt_ref[TT_GROUP * LANES + j]
        x = xs_ref[:, :D_MODEL]
        cw = xs_ref[:, D_MODEL:]
        h = (x * lax.rsqrt(jnp.mean(x * x, axis=-1, keepdims=True) + EPS) * g_ref[...]).astype(BF16)
        lane = lax.broadcasted_iota(jnp.int32, cw.shape, 1)
        acc = x
        for e in range(EXPERTS_PER_GROUP):
            a = jnp.dot(h, w1_ref[e], preferred_element_type=F32)
            b = jnp.dot(h, w3_ref[e], preferred_element_type=F32)
            col = jnp.sum(jnp.where(lane == grp * EXPERTS_PER_GROUP + e, cw, 0.0), axis=-1, keepdims=True)
            hid = (a / (1.0 + jnp.exp(-a))) * b * col
            acc = acc + jnp.dot(hid.astype(BF16), w2_ref[e], preferred_element_type=F32)
        o_ref[...] = acc

    @pl.when(j >= tt_ref[TT_USED * LANES])
    def _():
        o_ref[...] = jnp.zeros_like(o_ref)


def _experts(tt, xs, g, w1, w3, w2, tmx):
    n_rows = xs.shape[0]
    wspec = lambda a, b: pl.BlockSpec((EXPERTS_PER_GROUP, a, b), lambda j, tt: (tt[TT_GROUP * LANES + j], 0, 0))
    return pl.pallas_call(
        _experts_kernel,
        grid_spec=pltpu.PrefetchScalarGridSpec(
            num_scalar_prefetch=1, grid=(n_rows // tmx,),
            in_specs=[pl.BlockSpec((tmx, ROW_EXT), lambda j, tt: (j, 0)),
                      pl.BlockSpec((1, D_MODEL), lambda j, tt: (0, 0)),
                      wspec(D_MODEL, D_EXPERT), wspec(D_MODEL, D_EXPERT), wspec(D_EXPERT, D_MODEL)],
            out_specs=pl.BlockSpec((tmx, D_MODEL), lambda j, tt: (j, 0))),
        out_shape=jax.ShapeDtypeStruct((n_rows, D_MODEL), F32),
        compiler_params=_cparams(("arbitrary",), 52),
        name="experts",
    )(tt, xs, g, w1, w3, w2)


def _rope_swap(w):
    half = QK_ROPE // 2
    return jnp.concatenate([w, w[..., half:], w[..., :half]], axis=-1)


def _head_gain(g):
    return jnp.concatenate([g[:QK_NOPE], _rope_swap(g[QK_NOPE:])])[None, :].astype(F32)


def _layer_params(l, attn_norm, w_in, q_a_norm, w_uq, kv_a_norm, w_ukv, q_norm, k_norm, sgu_ln_g, sgu_ln_b,
                  sgu_w, sgu_b, w_proj_attn, w_proj_sgu, w_proj_fnet, w_out, ffn_norm, w_group, b_group,
                  w_router, b_router, w1, w3, w2):
    wi = w_in[l]
    c0 = Q_LORA + KV_LORA
    c1 = c0 + QK_ROPE
    w_ext = jnp.concatenate(
        [wi[:, :c0], jnp.zeros((D_MODEL, QK_NOPE), F32), _rope_swap(wi[:, c0:c1]), wi[:, c1:]], axis=1).astype(BF16)
    wq = w_uq[l].reshape(Q_LORA, N_HEADS, QK_HEAD)
    wq = jnp.concatenate([wq[..., :QK_NOPE], _rope_swap(wq[..., QK_NOPE:])], axis=-1)
    wq = wq.reshape(Q_LORA, N_HEADS * HEAD_SLAB).astype(BF16)
    wkv = w_ukv[l].reshape(KV_LORA, N_HEADS, QK_NOPE + V_HEAD)
    zpad = jnp.zeros((KV_LORA, N_HEADS, HEAD_SLAB - QK_NOPE), F32)
    wk = jnp.concatenate([wkv[..., :QK_NOPE], zpad], axis=-1).reshape(KV_LORA, N_HEADS * HEAD_SLAB).astype(BF16)
    wv = jnp.concatenate([wkv[..., QK_NOPE:], zpad], axis=-1).reshape(KV_LORA, N_HEADS * HEAD_SLAB).astype(BF16)
    w_rg = jnp.concatenate([w_router[l], w_group[l], jnp.zeros((D_MODEL, LANES - N_EXPERTS - N_GROUPS), F32)], axis=1)
    w_rg_hi = w_rg.astype(BF16)
    w_rg_lo = (w_rg - w_rg_hi.astype(F32)).astype(BF16)
    b_rg = jnp.concatenate([b_router[l], b_group[l], jnp.zeros((LANES - N_EXPERTS - N_GROUPS,), F32)])[None, :]
    return dict(
        attn_norm=attn_norm[l][None, :], w_ext=w_ext,
        q_a_norm=q_a_norm[l][None, :], kv_a_norm=kv_a_norm[l][None, :],
        ln_g=sgu_ln_g[l][None, :], ln_b=sgu_ln_b[l][None, :],
        wq=wq, wk=wk, wv=wv, gq=_head_gain(q_norm[l]), gk=_head_gain(k_norm[l]),
        sgu_w=sgu_w[l].astype(BF16), sgu_bias=jnp.repeat(sgu_b[l].T, SGU_GROUP_DIM, axis=1),
        wpa=w_proj_attn[l].astype(BF16), wps=w_proj_sgu[l].astype(BF16), wpf=w_proj_fnet[l].astype(BF16),
        wo=w_out[l].astype(BF16),
        ffn_norm=ffn_norm[l][None, :], w_rg_hi=w_rg_hi, w_rg_lo=w_rg_lo, b_rg=b_rg,
        w1=w1[l].astype(BF16), w3=w3[l].astype(BF16), w2=w2[l].astype(BF16),
    )


def _rope_slabs(positions):
    inv = 1.0 / (ROPE_THETA ** (jnp.arange(0, QK_ROPE, 2, dtype=F32) / QK_ROPE))
    ang = positions.astype(F32).reshape(-1)[:, None] * inv
    cos, sin = jnp.cos(ang), jnp.sin(ang)
    T = ang.shape[0]
    ctab = jnp.concatenate([jnp.ones((T, QK_NOPE), F32), cos, cos, jnp.zeros((T, QK_ROPE), F32)], axis=1)
    stab = jnp.concatenate([jnp.zeros((T, QK_NOPE), F32), -sin, sin, jnp.zeros((T, QK_ROPE), F32)], axis=1)
    return ctab, stab


def _dft_tables(n):
    idx = jnp.arange(n, dtype=jnp.int32)
    kn = (idx[:, None] * idx[None, :]) & (n - 1)
    ang = kn.astype(F32) * (2.0 * math.pi / n)
    c = jnp.cos(ang).astype(BF16)
    sneg = (-jnp.sin(ang)).astype(BF16)
    m = FNET_GROUP_DIM
    jk = np.outer(np.arange(m), np.arange(m)) % m
    angc = jk.astype(np.float64) * (2.0 * math.pi / m)
    cs = jnp.asarray(np.concatenate([np.cos(angc), np.sin(angc)], axis=1), F32).astype(BF16)
    return c, sneg, cs


def _tile(n, want):
    t = min(n, want)
    assert n % t == 0
    return t


def kernel(x, positions, attn_norm, w_in, q_a_norm, w_uq, kv_a_norm, w_ukv, q_norm, k_norm, sgu_ln_g, sgu_ln_b,
           sgu_w, sgu_b, w_proj_attn, w_proj_sgu, w_proj_fnet, w_out, ffn_norm, w_group, b_group, w_router,
           b_router, w1, w3, w2):
    batch, seq, _ = x.shape
    depth = w_in.shape[0]
    assert seq & (seq - 1) == 0 and seq % CHUNK == 0
    T = batch * seq
    tm = _tile(seq, 512)
    tmx = tm
    ridx = jnp.arange(tm, dtype=jnp.int32)
    utri = (ridx[:, None] < ridx[None, :]).astype(F32)
    ctab, stab = _rope_slabs(positions)
    dft_c, dft_sneg, cs = _dft_tables(seq)
    x2 = x.reshape(T, D_MODEL)
    for l in range(depth):
        p = _layer_params(l, attn_norm, w_in, q_a_norm, w_uq, kv_a_norm, w_ukv, q_norm, k_norm, sgu_ln_g,
                          sgu_ln_b, sgu_w, sgu_b, w_proj_attn, w_proj_sgu, w_proj_fnet, w_out, ffn_norm,
                          w_group, b_group, w_router, b_router, w1, w3, w2)
        cqn, ckvn, kr, u, v, fa, fb, gates = _inproj(
            x2, p["attn_norm"], p["w_ext"], p["q_a_norm"], p["kv_a_norm"], p["ln_g"], p["ln_b"], cs,
            batch, seq, _tile(seq, 256))
        q, k, vv = _mla_prep(cqn, ckvn, kr, ctab, stab, p["wq"], p["wk"], p["wv"], p["gq"], p["gk"],
                             batch, seq, tm)
        ya = _attention(q, k, vv, _tile(seq, 1024)).reshape(T, N_HEADS * V_HEAD)
        yf = _seq_dft(dft_c, dft_sneg, fa, fb, _tile(seq, 1024), _tile(batch * FNET_WIDTH, 1024), _tile(seq, 1024))
        x2 = _merge(x2, ya, u, v, yf, gates, p["sgu_w"], p["sgu_bias"], p["wpa"], p["wps"], p["wpf"], p["wo"],
                    seq, tm)
        xr, meta, cnt = _route(x2, p["ffn_norm"], p["w_rg_hi"], p["w_rg_lo"], p["b_rg"], utri, tm)
        dest, tt = _assign(cnt, meta, tmx)
        dest = dest.reshape(T)
        tt = tt.reshape(META_ROWS * LANES)
        xs = _dispatch(dest, tt, xr, T + N_GROUPS * tmx, tm)
        ys = _experts(tt, xs, p["ffn_norm"], p["w1"], p["w3"], p["w2"], tmx)
        x2 = _unsort(dest, ys, T, tm)
    return x2.reshape(batch, seq, D_MODEL)
```

```python
import functools
import math

import numpy as np
import jax
import jax.numpy as jnp
from jax import lax
from jax.experimental import pallas as pl
from jax.experimental.pallas import tpu as pltpu

F32 = jnp.float32
BF16 = jnp.bfloat16

D_MODEL = 1024
N_HEADS = 8
QK_NOPE = 64
QK_ROPE = 32
QK_HEAD = QK_NOPE + QK_ROPE
V_HEAD = 64
Q_LORA = 384
KV_LORA = 256
ROPE_THETA = 10000.0
CHUNK = 128
SGU_GROUPS = 8
SGU_GROUP_DIM = 64
SGU_WIDTH = SGU_GROUPS * SGU_GROUP_DIM
FNET_GROUPS = 4
FNET_GROUP_DIM = 128
FNET_WIDTH = FNET_GROUPS * FNET_GROUP_DIM
N_BRANCH = 3
N_GROUPS = 4
EXPERTS_PER_GROUP = 8
N_EXPERTS = N_GROUPS * EXPERTS_PER_GROUP
D_EXPERT = 256
EPS = 1e-6

LANES = 128
HEAD_SLAB = LANES
O_CQ = 0
O_CKV = O_CQ + Q_LORA
O_KR = O_CKV + KV_LORA
O_SGU = O_KR + HEAD_SLAB
O_FN = O_SGU + 2 * SGU_WIDTH
O_GATE = O_FN + FNET_WIDTH
N_IN_EXT = O_GATE + N_BRANCH * D_MODEL

V7X_VMEM_BYTES = 64 * 1024 * 1024


def _cparams(dims, vmem_mb):
    return pltpu.CompilerParams(dimension_semantics=dims,
                                vmem_limit_bytes=min(vmem_mb * 1024 * 1024, V7X_VMEM_BYTES - (6 << 20)))


def _const_spec(shape):
    nd = len(shape)
    return pl.BlockSpec(shape, lambda *_: (0,) * nd)


def _inproj_kernel(x_ref, g_ref, w_ref, qan_ref, kvan_ref, lng_ref, lnb_ref, cs_ref,
                   cq_ref, ckv_ref, kr_ref, u_ref, v_ref, fa_ref, fb_ref, gate_ref):
    x = x_ref[...]
    h = (x * lax.rsqrt(jnp.mean(x * x, axis=-1, keepdims=True) + EPS) * g_ref[...]).astype(BF16)

    def seg(a, n):
        return jnp.dot(h, w_ref[:, a:a + n], preferred_element_type=F32)

    cq = seg(O_CQ, Q_LORA)
    cq_ref[...] = (cq * lax.rsqrt(jnp.mean(cq * cq, axis=-1, keepdims=True) + EPS) * qan_ref[...]).astype(BF16)
    ckv = seg(O_CKV, KV_LORA)
    ckv_ref[...] = (ckv * lax.rsqrt(jnp.mean(ckv * ckv, axis=-1, keepdims=True) + EPS) * kvan_ref[...]).astype(BF16)
    kr_ref[...] = seg(O_KR, HEAD_SLAB).astype(BF16)

    u = jax.nn.gelu(seg(O_SGU, SGU_WIDTH))
    u_ref[...] = u.astype(BF16)
    v = jax.nn.gelu(seg(O_SGU + SGU_WIDTH, SGU_WIDTH))
    mu = jnp.mean(v, axis=-1, keepdims=True)
    vc = v - mu
    var = jnp.mean(vc * vc, axis=-1, keepdims=True)
    v_ref[...] = (vc * lax.rsqrt(var + EPS) * lng_ref[...] + lnb_ref[...]).astype(BF16)

    zf = seg(O_FN, FNET_WIDTH).astype(BF16)
    for g in range(FNET_GROUPS):
        sl = slice(g * FNET_GROUP_DIM, (g + 1) * FNET_GROUP_DIM)
        ab = jnp.dot(zf[:, sl], cs_ref[...], preferred_element_type=F32)
        fa_ref[:, sl] = ab[:, :FNET_GROUP_DIM].astype(BF16)
        fb_ref[:, sl] = ab[:, FNET_GROUP_DIM:].astype(BF16)

    for j in range(N_BRANCH):
        gl = seg(O_GATE + j * D_MODEL, D_MODEL)
        gate_ref[:, j * D_MODEL:(j + 1) * D_MODEL] = (1.0 / (1.0 + jnp.exp(-gl))).astype(BF16)


def _inproj(x2, g, w_ext, qan, kvan, lng, lnb, cs, batch, seq, tm):
    T = x2.shape[0]
    n_s = seq // tm
    row = lambda n: pl.BlockSpec((tm, n), lambda i: (i, 0))
    sb = pl.BlockSpec((tm, FNET_WIDTH), lambda i: (i % n_s, i // n_s))
    outs = [
        jax.ShapeDtypeStruct((T, Q_LORA), BF16),
        jax.ShapeDtypeStruct((T, KV_LORA), BF16),
        jax.ShapeDtypeStruct((T, HEAD_SLAB), BF16),
        jax.ShapeDtypeStruct((T, SGU_WIDTH), BF16),
        jax.ShapeDtypeStruct((T, SGU_WIDTH), BF16),
        jax.ShapeDtypeStruct((seq, batch * FNET_WIDTH), BF16),
        jax.ShapeDtypeStruct((seq, batch * FNET_WIDTH), BF16),
        jax.ShapeDtypeStruct((T, N_BRANCH * D_MODEL), BF16),
    ]
    return pl.pallas_call(
        _inproj_kernel,
        grid=(T // tm,),
        in_specs=[row(D_MODEL), _const_spec((1, D_MODEL)), _const_spec((D_MODEL, N_IN_EXT)),
                  _const_spec((1, Q_LORA)), _const_spec((1, KV_LORA)),
                  _const_spec((1, SGU_WIDTH)), _const_spec((1, SGU_WIDTH)),
                  _const_spec((FNET_GROUP_DIM, 2 * FNET_GROUP_DIM))],
        out_specs=[row(Q_LORA), row(KV_LORA), row(HEAD_SLAB), row(SGU_WIDTH), row(SGU_WIDTH),
                   sb, sb, row(N_BRANCH * D_MODEL)],
        out_shape=outs,
        compiler_params=_cparams(("arbitrary",), 52),
        name="inproj",
    )(x2, g, w_ext, qan, kvan, lng, lnb, cs)


def _head_norm_rope(y, gain, ct, st, keep):
    ss = jnp.sum(jnp.where(keep, y * y, 0.0), axis=-1, keepdims=True)
    y = y * lax.rsqrt(ss * (1.0 / QK_HEAD) + EPS) * gain
    return y * ct + pltpu.roll(y, LANES - QK_ROPE, 1) * st


def _mla_prep_kernel(cq_ref, ckv_ref, kr_ref, ct_ref, st_ref, wq_ref, wk_ref, wv_ref, gq_ref, gk_ref,
                     q_ref, k_ref, v_ref):
    cq = cq_ref[...]
    ckv = ckv_ref[...]
    kr = kr_ref[...].astype(F32)
    ct = ct_ref[...]
    st = st_ref[...]
    lane = lax.broadcasted_iota(jnp.int32, ct.shape, 1)
    keep = lane < QK_HEAD
    is_v = lane < V_HEAD
    q_all = jnp.dot(cq, wq_ref[...], preferred_element_type=F32)
    k_all = jnp.dot(ckv, wk_ref[...], preferred_element_type=F32)
    v_all = jnp.dot(ckv, wv_ref[...], preferred_element_type=F32)
    for h in range(N_HEADS):
        sl = slice(h * HEAD_SLAB, (h + 1) * HEAD_SLAB)
        q_ref[0, h] = _head_norm_rope(q_all[:, sl], gq_ref[...], ct, st, keep).astype(BF16)
        k_ref[0, h] = _head_norm_rope(k_all[:, sl] + kr, gk_ref[...], ct, st, keep).T.astype(BF16)
        v_ref[0, h] = jnp.where(is_v, v_all[:, sl], 1.0).astype(BF16)


def _mla_prep(cqn, ckvn, kr, ctab, stab, wq, wk, wv, gq, gk, batch, seq, tm):
    T = cqn.shape[0]
    n_s = seq // tm
    row = lambda n: pl.BlockSpec((tm, n), lambda i: (i, 0))
    hb = pl.BlockSpec((1, N_HEADS, tm, HEAD_SLAB), lambda i: (i // n_s, 0, i % n_s, 0))
    shp = jax.ShapeDtypeStruct((batch, N_HEADS, seq, HEAD_SLAB), BF16)
    return pl.pallas_call(
        _mla_prep_kernel,
        grid=(T // tm,),
        in_specs=[row(Q_LORA), row(KV_LORA), row(HEAD_SLAB), row(HEAD_SLAB), row(HEAD_SLAB),
                  _const_spec((Q_LORA, N_HEADS * HEAD_SLAB)), _const_spec((KV_LORA, N_HEADS * HEAD_SLAB)),
                  _const_spec((KV_LORA, N_HEADS * HEAD_SLAB)),
                  _const_spec((1, HEAD_SLAB)), _const_spec((1, HEAD_SLAB))],
        out_specs=[hb, pl.BlockSpec((1, N_HEADS, HEAD_SLAB, tm), lambda i: (i // n_s, 0, 0, i % n_s)), hb],
        out_shape=[shp, jax.ShapeDtypeStruct((batch, N_HEADS, HEAD_SLAB, seq), BF16), shp],
        compiler_params=_cparams(("arbitrary",), 32),
        name="mla_prep",
    )(cqn, ckvn, kr, ctab, stab, wq, wk, wv, gq, gk)


HEADS_PER_STEP = LANES // V_HEAD


ROW_CHAINS = 4


def _attn_kernel(q_ref, kt_ref, v_ref, o_ref):
    tq = q_ref.shape[2]
    rows_per_chain = tq // ROW_CHAINS
    lane = lax.broadcasted_iota(jnp.int32, (rows_per_chain, LANES), 1)
    low = lane < V_HEAD
    for r in range(ROW_CHAINS):
        rows = slice(r * rows_per_chain, (r + 1) * rows_per_chain)
        outs = []
        for j in range(HEADS_PER_STEP):
            s = jnp.dot(q_ref[0, j, rows, :], kt_ref[0, j], preferred_element_type=F32)
            m = jnp.max(s, axis=-1, keepdims=True)
            p = jnp.exp2((s - m).astype(BF16))
            oe = jnp.dot(p, v_ref[0, j], preferred_element_type=F32)
            den = jnp.where(low, pltpu.roll(oe, V_HEAD, 1), 1.0)
            outs.append(oe / den)
        o_ref[0, rows, :] = jnp.where(low, outs[0], pltpu.roll(outs[1], V_HEAD, 1)).astype(BF16)


def _attention(q, k, v, tq):
    batch, _, seq, _ = q.shape
    n_hp = N_HEADS // HEADS_PER_STEP
    return pl.pallas_call(
        _attn_kernel,
        grid=(batch, n_hp, seq // tq),
        in_specs=[pl.BlockSpec((1, HEADS_PER_STEP, tq, HEAD_SLAB), lambda b, h, i: (b, h, i, 0)),
                  pl.BlockSpec((1, HEADS_PER_STEP, HEAD_SLAB, seq), lambda b, h, i: (b, h, 0, 0)),
                  pl.BlockSpec((1, HEADS_PER_STEP, seq, HEAD_SLAB), lambda b, h, i: (b, h, 0, 0))],
        out_specs=pl.BlockSpec((1, tq, LANES), lambda b, h, i: (b, i, h)),
        out_shape=jax.ShapeDtypeStruct((batch, seq, N_HEADS * V_HEAD), BF16),
        compiler_params=_cparams(("arbitrary", "arbitrary", "arbitrary"), 48),
        name="attention",
    )(q, k, v)


def _dft_kernel(c_ref, s_ref, a_ref, b_ref, o_ref, acc_ref, *, scale):
    kk = pl.program_id(2)

    @pl.when(kk == 0)
    def _():
        acc_ref[...] = jnp.zeros_like(acc_ref)

    acc_ref[...] += (jnp.dot(c_ref[...], a_ref[...], preferred_element_type=F32)
                     + jnp.dot(s_ref[...], b_ref[...], preferred_element_type=F32))

    @pl.when(kk == pl.num_programs(2) - 1)
    def _():
        o_ref[...] = (acc_ref[...] * scale).astype(BF16)


def _seq_dft(ctab, sneg, fa, fb, tm, tn, tk):
    seq, width = fa.shape
    scale = 1.0 / math.sqrt(seq * FNET_GROUP_DIM)
    return pl.pallas_call(
        functools.partial(_dft_kernel, scale=scale),
        grid=(seq // tm, width // tn, seq // tk),
        in_specs=[pl.BlockSpec((tm, tk), lambda i, j, k: (i, k)),
                  pl.BlockSpec((tm, tk), lambda i, j, k: (i, k)),
                  pl.BlockSpec((tk, tn), lambda i, j, k: (k, j)),
                  pl.BlockSpec((tk, tn), lambda i, j, k: (k, j))],
        out_specs=pl.BlockSpec((tm, tn), lambda i, j, k: (i, j)),
        out_shape=jax.ShapeDtypeStruct((seq, width), BF16),
        scratch_shapes=[pltpu.VMEM((tm, tn), F32)],
        compiler_params=_cparams(("arbitrary", "arbitrary", "arbitrary"), 40),
        name="seq_dft",
    )(ctab, sneg, fa, fb)


def _merge_kernel(x_ref, ya_ref, u_ref, v_ref, yf_ref, gate_ref, ws_ref, bias_ref,
                  wpa_ref, wps_ref, wpf_ref, wo_ref, o_ref, ysgu_ref):
    tm = x_ref.shape[0]
    lane = lax.broadcasted_iota(jnp.int32, (CHUNK, LANES), 1)
    low = lane < SGU_GROUP_DIM
    groups_per_slab = LANES // SGU_GROUP_DIM
    for c in range(tm // CHUNK):
        rows = slice(c * CHUNK, (c + 1) * CHUNK)
        for j in range(SGU_WIDTH // LANES):
            cols = slice(j * LANES, (j + 1) * LANES)
            vb = v_ref[rows, cols]
            ma = jnp.dot(ws_ref[groups_per_slab * j], vb, preferred_element_type=F32)
            mb = jnp.dot(ws_ref[groups_per_slab * j + 1], vb, preferred_element_type=F32)
            mixed = jnp.where(low, ma, mb) + bias_ref[:, cols]
            ysgu_ref[rows, cols] = (u_ref[rows, cols].astype(F32) * mixed).astype(BF16)
    pa = jnp.dot(ya_ref[...], wpa_ref[...], preferred_element_type=F32)
    merged = gate_ref[:, 0:D_MODEL].astype(F32) * pa
    ps = jnp.dot(ysgu_ref[...], wps_ref[...], preferred_element_type=F32)
    merged += gate_ref[:, D_MODEL:2 * D_MODEL].astype(F32) * ps
    pf = jnp.dot(yf_ref[...], wpf_ref[...], preferred_element_type=F32)
    merged += gate_ref[:, 2 * D_MODEL:3 * D_MODEL].astype(F32) * pf
    o_ref[...] = x_ref[...] + jnp.dot(merged.astype(BF16), wo_ref[...], preferred_element_type=F32)


def _merge(x2, ya, u, v, yf, gates, ws, bias, wpa, wps, wpf, wo, seq, tm):
    T = x2.shape[0]
    n_s = seq // tm
    row = lambda n: pl.BlockSpec((tm, n), lambda i: (i, 0))
    return pl.pallas_call(
        _merge_kernel,
        grid=(T // tm,),
        in_specs=[row(D_MODEL), row(N_HEADS * V_HEAD), row(SGU_WIDTH), row(SGU_WIDTH),
                  pl.BlockSpec((tm, FNET_WIDTH), lambda i: (i % n_s, i // n_s)),
                  row(N_BRANCH * D_MODEL),
                  _const_spec((SGU_GROUPS, CHUNK, CHUNK)), _const_spec((CHUNK, SGU_WIDTH)),
                  _const_spec((N_HEADS * V_HEAD, D_MODEL)), _const_spec((SGU_WIDTH, D_MODEL)),
                  _const_spec((FNET_WIDTH, D_MODEL)), _const_spec((D_MODEL, D_MODEL))],
        out_specs=row(D_MODEL),
        out_shape=jax.ShapeDtypeStruct((T, D_MODEL), F32),
        scratch_shapes=[pltpu.VMEM((tm, SGU_WIDTH), BF16)],
        compiler_params=_cparams(("arbitrary",), 48),
        name="merge",
    )(x2, ya, u, v, yf, gates, ws, bias, wpa, wps, wpf, wo)


O_GLOGIT = N_EXPERTS


META_ROWS = 8
ROW_EXT = D_MODEL + LANES


def _route_kernel(x_ref, g_ref, whi_ref, wlo_ref, b_ref, utri_ref, xr_ref, meta_ref, cnt_ref):
    x = x_ref[...]
    tm = x.shape[0]
    h = x * lax.rsqrt(jnp.mean(x * x, axis=-1, keepdims=True) + EPS) * g_ref[...]
    hhi = h.astype(BF16)
    hlo = (h - hhi.astype(F32)).astype(BF16)
    logits = (jnp.dot(hhi, whi_ref[...], preferred_element_type=F32)
              + jnp.dot(hlo, whi_ref[...], preferred_element_type=F32)
              + jnp.dot(hhi, wlo_ref[...], preferred_element_type=F32)) + b_ref[...]
    lane = lax.broadcasted_iota(jnp.int32, logits.shape, 1)
    lane_f = lane.astype(F32)
    neg = -jnp.inf
    far = float(2 * LANES)

    is_g = (lane >= O_GLOGIT) & (lane < O_GLOGIT + N_GROUPS)
    gl = jnp.where(is_g, logits, neg)
    gmax = jnp.max(gl, axis=-1, keepdims=True)
    gsel = jnp.min(jnp.where(gl == gmax, lane_f, far), axis=-1, keepdims=True) - float(O_GLOGIT)
    gw = 1.0 / jnp.sum(jnp.where(is_g, jnp.exp(gl - gmax), 0.0), axis=-1, keepdims=True)

    lane_grp = lax.shift_right_logical(lane, 3).astype(F32)
    in_grp = (lane < N_EXPERTS) & (lane_grp == gsel)
    el = jnp.where(in_grp, logits, neg)
    e1 = jnp.max(el, axis=-1, keepdims=True)
    i1 = jnp.min(jnp.where(el == e1, lane_f, far), axis=-1, keepdims=True)
    el2 = jnp.where(lane_f == i1, neg, el)
    e2 = jnp.max(el2, axis=-1, keepdims=True)
    i2 = jnp.min(jnp.where(el2 == e2, lane_f, far), axis=-1, keepdims=True)
    d = jnp.exp(e2 - e1)
    w1 = gw / (1.0 + d)
    w2 = w1 * d
    xr_ref[:, :D_MODEL] = x
    xr_ref[:, D_MODEL:] = jnp.where(lane_f == i1, w1, jnp.where(lane_f == i2, w2, 0.0))

    onehot_t = jnp.where((lane_f == gsel) & (lane < N_GROUPS), 1.0, 0.0).T[0:META_ROWS, :]
    before = jnp.dot(onehot_t, utri_ref[...], preferred_element_type=F32)
    gid = lax.broadcasted_iota(jnp.int32, (META_ROWS, tm), 0).astype(F32)
    gsel_t = jnp.sum(onehot_t * gid, axis=0, keepdims=True)
    rank_t = jnp.sum(onehot_t * before, axis=0, keepdims=True)
    row_id = lax.broadcasted_iota(jnp.int32, (META_ROWS, tm), 0)
    meta_ref[0] = jnp.where(row_id == 0, gsel_t, jnp.where(row_id == 1, rank_t, 0.0)).astype(jnp.int32)
    cnt = jnp.sum(onehot_t, axis=1, keepdims=True)
    cnt_ref[0] = jnp.broadcast_to(cnt, (META_ROWS, LANES)).astype(jnp.int32)


def _route(x2, g, whi, wlo, b, utri, tm):
    T = x2.shape[0]
    n_tiles = T // tm
    row = lambda n: pl.BlockSpec((tm, n), lambda i: (i, 0))
    return pl.pallas_call(
        _route_kernel,
        grid=(n_tiles,),
        in_specs=[row(D_MODEL), _const_spec((1, D_MODEL)), _const_spec((D_MODEL, LANES)),
                  _const_spec((D_MODEL, LANES)), _const_spec((1, LANES)), _const_spec((tm, tm))],
        out_specs=[row(ROW_EXT), pl.BlockSpec((1, META_ROWS, tm), lambda i: (i, 0, 0)),
                   pl.BlockSpec((1, META_ROWS, LANES), lambda i: (i, 0, 0))],
        out_shape=[jax.ShapeDtypeStruct((T, ROW_EXT), F32),
                   jax.ShapeDtypeStruct((n_tiles, META_ROWS, tm), jnp.int32),
                   jax.ShapeDtypeStruct((n_tiles, META_ROWS, LANES), jnp.int32)],
        compiler_params=_cparams(("arbitrary",), 40),
        name="route",
    )(x2, g, whi, wlo, b, utri)


TT_GROUP, TT_USED, TT_PAD_LO, TT_PAD_HI = 0, 1, 2, 3


def _assign_kernel(cnt_ref, meta_ref, dest_ref, tt_ref, *, tmx):
    n_tiles, _, tm = meta_ref.shape
    shift = tmx.bit_length() - 1
    cnt = cnt_ref[...]
    tot = jnp.sum(cnt, axis=0)
    padded = lax.shift_left(lax.shift_right_logical(tot + (tmx - 1), shift), shift)
    starts, ends = [], []
    run = jnp.zeros((1, LANES), jnp.int32)
    for g in range(N_GROUPS):
        starts.append(run)
        run = run + padded[g:g + 1, :]
        ends.append(run)

    def widen(v):
        return jnp.concatenate([v] * (tm // LANES), axis=1)

    def body(i, seen):
        gs = meta_ref[i, 0:1, :]
        dest = meta_ref[i, 1:2, :]
        for g in range(N_GROUPS):
            dest = dest + jnp.where(gs == g, widen(starts[g] + seen[g]), 0)
        dest_ref[i] = dest
        c = cnt_ref[i]
        return tuple(seen[g] + c[g:g + 1, :] for g in range(N_GROUPS))

    lax.fori_loop(0, n_tiles, body, tuple(jnp.zeros((1, LANES), jnp.int32) for _ in range(N_GROUPS)))

    lane = lax.broadcasted_iota(jnp.int32, (1, LANES), 1)
    first_row = lane * tmx
    grp = jnp.zeros((1, LANES), jnp.int32)
    for g in range(N_GROUPS - 1):
        grp = grp + jnp.where(first_row >= ends[g], 1, 0)
    used = lax.shift_right_logical(ends[-1], shift)
    pad_lo = jnp.zeros((1, LANES), jnp.int32)
    pad_hi = jnp.zeros((1, LANES), jnp.int32)
    for g in range(N_GROUPS):
        pad_lo = pad_lo + jnp.where(lane == g, starts[g] + tot[g:g + 1, :], 0)
        pad_hi = pad_hi + jnp.where(lane == g, ends[g], 0)
    n_rows = (n_tiles * tm // tmx + N_GROUPS) * tmx
    pad_lo = pad_lo + jnp.where(lane == N_GROUPS, ends[-1], 0)
    pad_hi = pad_hi + jnp.where(lane == N_GROUPS, n_rows, 0)
    row_id = lax.broadcasted_iota(jnp.int32, (META_ROWS, LANES), 0)
    tt_ref[...] = jnp.where(row_id == TT_GROUP, grp,
                            jnp.where(row_id == TT_USED, used,
                                      jnp.where(row_id == TT_PAD_LO, pad_lo,
                                                jnp.where(row_id == TT_PAD_HI, pad_hi, 0))))


def _assign(cnt, meta, tmx):
    n_tiles, _, tm = meta.shape
    return pl.pallas_call(
        functools.partial(_assign_kernel, tmx=tmx),
        grid=(1,),
        in_specs=[_const_spec(cnt.shape), _const_spec(meta.shape)],
        out_specs=[_const_spec((n_tiles, 1, tm)), _const_spec((META_ROWS, LANES))],
        out_shape=[jax.ShapeDtypeStruct((n_tiles, 1, tm), jnp.int32),
                   jax.ShapeDtypeStruct((META_ROWS, LANES), jnp.int32)],
        compiler_params=_cparams(("arbitrary",), 32),
        name="assign",
    )(cnt, meta)


def _dispatch_kernel(dest_ref, tt_ref, xr_ref, zero_ref, xs_hbm, sem, *, tm):
    base = pl.program_id(0) * tm

    def issue(r, c):
        pltpu.make_async_copy(xr_ref.at[pl.ds(r, 1)], xs_hbm.at[pl.ds(dest_ref[base + r], 1)], sem).start()
        return c

    lax.fori_loop(0, tm, issue, 0, unroll=8)
    pltpu.make_async_copy(xr_ref, xs_hbm.at[pl.ds(0, tm)], sem).wait()

    @pl.when(pl.program_id(0) == 0)
    def _():
        for g in range(N_GROUPS + 1):
            lo = tt_ref[TT_PAD_LO * LANES + g]
            hi = tt_ref[TT_PAD_HI * LANES + g]

            def fill(r, c):
                pltpu.make_async_copy(zero_ref.at[pl.ds(0, 1)], xs_hbm.at[pl.ds(r, 1)], sem).start()
                return c

            lax.fori_loop(lo, hi, fill, 0)

            def drain(r, c):
                pltpu.make_async_copy(zero_ref.at[pl.ds(0, 1)], xs_hbm.at[pl.ds(0, 1)], sem).wait()
                return c

            lax.fori_loop(lo, hi, drain, 0)


def _dispatch(dest, tt, xr, n_rows, tm):
    T, width = xr.shape
    zero = jnp.zeros((META_ROWS, width), xr.dtype)
    return pl.pallas_call(
        functools.partial(_dispatch_kernel, tm=tm),
        grid_spec=pltpu.PrefetchScalarGridSpec(
            num_scalar_prefetch=2, grid=(T // tm,),
            in_specs=[pl.BlockSpec((tm, width), lambda i, d, t: (i, 0)),
                      pl.BlockSpec((META_ROWS, width), lambda i, d, t: (0, 0))],
            out_specs=pl.BlockSpec(memory_space=pl.ANY),
            scratch_shapes=[pltpu.SemaphoreType.DMA(())]),
        out_shape=jax.ShapeDtypeStruct((n_rows, width), xr.dtype),
        compiler_params=_cparams(("arbitrary",), 16),
        name="dispatch",
    )(dest, tt, xr, zero)


def _unsort_kernel(dest_ref, ys_hbm, y_ref, sem, *, tm):
    base = pl.program_id(0) * tm

    def issue(r, c):
        pltpu.make_async_copy(ys_hbm.at[pl.ds(dest_ref[base + r], 1)], y_ref.at[pl.ds(r, 1)], sem).start()
        return c

    lax.fori_loop(0, tm, issue, 0, unroll=8)
    pltpu.make_async_copy(ys_hbm.at[pl.ds(0, tm)], y_ref, sem).wait()


def _unsort(dest, ys, T, tm):
    width = ys.shape[1]
    return pl.pallas_call(
        functools.partial(_unsort_kernel, tm=tm),
        grid_spec=pltpu.PrefetchScalarGridSpec(
            num_scalar_prefetch=1, grid=(T // tm,), in_specs=[pl.BlockSpec(memory_space=pl.ANY)],
            out_specs=pl.BlockSpec((tm, width), lambda i, d: (i, 0)),
            scratch_shapes=[pltpu.SemaphoreType.DMA(())]),
        out_shape=jax.ShapeDtypeStruct((T, width), ys.dtype),
        compiler_params=_cparams(("arbitrary",), 16),
        name="unsort",
    )(dest, ys)


def _experts_kernel(tt_ref, xs_ref, g_ref, w1_ref, w3_ref, w2_ref, o_ref):
    j = pl.program_id(0)

    @pl.when(j < tt_ref[TT_USED * LANES])
    def _():
        grp = tt_ref[TT_GROUP * LANES + j]
        x = xs_ref[:, :D_MODEL]
        cw = xs_ref[:, D_MODEL:]
        h = (x * lax.rsqrt(jnp.mean(x * x, axis=-1, keepdims=True) + EPS) * g_ref[...]).astype(BF16)
        lane = lax.broadcasted_iota(jnp.int32, cw.shape, 1)
        acc = x
        for e in range(EXPERTS_PER_GROUP):
            a = jnp.dot(h, w1_ref[e], preferred_element_type=F32)
            b = jnp.dot(h, w3_ref[e], preferred_element_type=F32)
            col = jnp.sum(jnp.where(lane == grp * EXPERTS_PER_GROUP + e, cw, 0.0), axis=-1, keepdims=True)
            hid = (a / (1.0 + jnp.exp(-a))) * b * col
            acc = acc + jnp.dot(hid.astype(BF16), w2_ref[e], preferred_element_type=F32)
        o_ref[...] = acc

    @pl.when(j >= tt_ref[TT_USED * LANES])
    def _():
        o_ref[...] = jnp.zeros_like(o_ref)


def _experts(tt, xs, g, w1, w3, w2, tmx):
    n_rows = xs.shape[0]
    wspec = lambda a, b: pl.BlockSpec((EXPERTS_PER_GROUP, a, b), lambda j, tt: (tt[TT_GROUP * LANES + j], 0, 0))
    return pl.pallas_call(
        _experts_kernel,
        grid_spec=pltpu.PrefetchScalarGridSpec(
            num_scalar_prefetch=1, grid=(n_rows // tmx,),
            in_specs=[pl.BlockSpec((tmx, ROW_EXT), lambda j, tt: (j, 0)),
                      pl.BlockSpec((1, D_MODEL), lambda j, tt: (0, 0)),
                      wspec(D_MODEL, D_EXPERT), wspec(D_MODEL, D_EXPERT), wspec(D_EXPERT, D_MODEL)],
            out_specs=pl.BlockSpec((tmx, D_MODEL), lambda j, tt: (j, 0))),
        out_shape=jax.ShapeDtypeStruct((n_rows, D_MODEL), F32),
        compiler_params=_cparams(("arbitrary",), 52),
        name="experts",
    )(tt, xs, g, w1, w3, w2)


def _rope_swap(w):
    half = QK_ROPE // 2
    return jnp.concatenate([w, w[..., half:], w[..., :half]], axis=-1)


def _head_gain(g):
    return jnp.concatenate([g[:QK_NOPE], _rope_swap(g[QK_NOPE:])])[None, :].astype(F32)


def _layer_params(l, attn_norm, w_in, q_a_norm, w_uq, kv_a_norm, w_ukv, q_norm, k_norm, sgu_ln_g, sgu_ln_b,
                  sgu_w, sgu_b, w_proj_attn, w_proj_sgu, w_proj_fnet, w_out, ffn_norm, w_group, b_group,
                  w_router, b_router, w1, w3, w2):
    wi = w_in[l]
    c0 = Q_LORA + KV_LORA
    c1 = c0 + QK_ROPE
    w_ext = jnp.concatenate(
        [wi[:, :c0], jnp.zeros((D_MODEL, QK_NOPE), F32), _rope_swap(wi[:, c0:c1]), wi[:, c1:]], axis=1).astype(BF16)
    wq = w_uq[l].reshape(Q_LORA, N_HEADS, QK_HEAD)
    wq = jnp.concatenate([wq[..., :QK_NOPE], _rope_swap(wq[..., QK_NOPE:])], axis=-1)
    wq = wq.reshape(Q_LORA, N_HEADS * HEAD_SLAB).astype(BF16)
    wkv = w_ukv[l].reshape(KV_LORA, N_HEADS, QK_NOPE + V_HEAD)
    zpad = jnp.zeros((KV_LORA, N_HEADS, HEAD_SLAB - QK_NOPE), F32)
    wk = jnp.concatenate([wkv[..., :QK_NOPE], zpad], axis=-1).reshape(KV_LORA, N_HEADS * HEAD_SLAB).astype(BF16)
    wv = jnp.concatenate([wkv[..., QK_NOPE:], zpad], axis=-1).reshape(KV_LORA, N_HEADS * HEAD_SLAB).astype(BF16)
    w_rg = jnp.concatenate([w_router[l], w_group[l], jnp.zeros((D_MODEL, LANES - N_EXPERTS - N_GROUPS), F32)], axis=1)
    w_rg_hi = w_rg.astype(BF16)
    w_rg_lo = (w_rg - w_rg_hi.astype(F32)).astype(BF16)
    b_rg = jnp.concatenate([b_router[l], b_group[l], jnp.zeros((LANES - N_EXPERTS - N_GROUPS,), F32)])[None, :]
    return dict(
        attn_norm=attn_norm[l][None, :], w_ext=w_ext,
        q_a_norm=q_a_norm[l][None, :], kv_a_norm=kv_a_norm[l][None, :],
        ln_g=sgu_ln_g[l][None, :], ln_b=sgu_ln_b[l][None, :],
        wq=wq, wk=wk, wv=wv, gq=_head_gain(q_norm[l]) * (QK_HEAD ** -0.5 * math.log2(math.e)),
        gk=_head_gain(k_norm[l]),
        sgu_w=sgu_w[l].astype(BF16), sgu_bias=jnp.repeat(sgu_b[l].T, SGU_GROUP_DIM, axis=1),
        wpa=w_proj_attn[l].astype(BF16), wps=w_proj_sgu[l].astype(BF16), wpf=w_proj_fnet[l].astype(BF16),
        wo=w_out[l].astype(BF16),
        ffn_norm=ffn_norm[l][None, :], w_rg_hi=w_rg_hi, w_rg_lo=w_rg_lo, b_rg=b_rg,
        w1=w1[l].astype(BF16), w3=w3[l].astype(BF16), w2=w2[l].astype(BF16),
    )


def _rope_slabs(positions):
    inv = 1.0 / (ROPE_THETA ** (jnp.arange(0, QK_ROPE, 2, dtype=F32) / QK_ROPE))
    ang = positions.astype(F32).reshape(-1)[:, None] * inv
    cos, sin = jnp.cos(ang), jnp.sin(ang)
    T = ang.shape[0]
    ctab = jnp.concatenate([jnp.ones((T, QK_NOPE), F32), cos, cos, jnp.zeros((T, QK_ROPE), F32)], axis=1)
    stab = jnp.concatenate([jnp.zeros((T, QK_NOPE), F32), -sin, sin, jnp.zeros((T, QK_ROPE), F32)], axis=1)
    return ctab, stab


def _dft_tables(n):
    r = 1 << ((n.bit_length() - 1) // 2)
    kk = jnp.arange(n, dtype=jnp.int32)[:, None]
    ang_hi = ((kk * (jnp.arange(n // r, dtype=jnp.int32) * r)[None, :]) & (n - 1)).astype(F32) * (2.0 * math.pi / n)
    ang_lo = ((kk * jnp.arange(r, dtype=jnp.int32)[None, :]) & (n - 1)).astype(F32) * (2.0 * math.pi / n)
    ch, sh = jnp.cos(ang_hi)[:, :, None], jnp.sin(ang_hi)[:, :, None]
    cl, sl = jnp.cos(ang_lo)[:, None, :], jnp.sin(ang_lo)[:, None, :]
    c = (ch * cl - sh * sl).reshape(n, n).astype(BF16)
    sneg = (-(sh * cl + ch * sl)).reshape(n, n).astype(BF16)
    m = FNET_GROUP_DIM
    jk = np.outer(np.arange(m), np.arange(m)) % m
    angc = jk.astype(np.float64) * (2.0 * math.pi / m)
    cs = jnp.asarray(np.concatenate([np.cos(angc), np.sin(angc)], axis=1), F32).astype(BF16)
    return c, sneg, cs


def _tile(n, want):
    t = min(n, want)
    assert n % t == 0
    return t


def kernel(x, positions, attn_norm, w_in, q_a_norm, w_uq, kv_a_norm, w_ukv, q_norm, k_norm, sgu_ln_g, sgu_ln_b,
           sgu_w, sgu_b, w_proj_attn, w_proj_sgu, w_proj_fnet, w_out, ffn_norm, w_group, b_group, w_router,
           b_router, w1, w3, w2):
    batch, seq, _ = x.shape
    depth = w_in.shape[0]
    assert seq & (seq - 1) == 0 and seq % CHUNK == 0
    T = batch * seq
    tm = _tile(seq, 512)
    tmx = tm
    ridx = jnp.arange(tm, dtype=jnp.int32)
    utri = (ridx[:, None] < ridx[None, :]).astype(F32)
    ctab, stab = _rope_slabs(positions)
    dft_c, dft_sneg, cs = _dft_tables(seq)
    x2 = x.reshape(T, D_MODEL)
    for l in range(depth):
        p = _layer_params(l, attn_norm, w_in, q_a_norm, w_uq, kv_a_norm, w_ukv, q_norm, k_norm, sgu_ln_g,
                          sgu_ln_b, sgu_w, sgu_b, w_proj_attn, w_proj_sgu, w_proj_fnet, w_out, ffn_norm,
                          w_group, b_group, w_router, b_router, w1, w3, w2)
        cqn, ckvn, kr, u, v, fa, fb, gates = _inproj(
            x2, p["attn_norm"], p["w_ext"], p["q_a_norm"], p["kv_a_norm"], p["ln_g"], p["ln_b"], cs,
            batch, seq, _tile(seq, 256))
        q, k, vv = _mla_prep(cqn, ckvn, kr, ctab, stab, p["wq"], p["wk"], p["wv"], p["gq"], p["gk"],
                             batch, seq, tm)
        ya = _attention(q, k, vv, _tile(seq, 1024)).reshape(T, N_HEADS * V_HEAD)
        yf = _seq_dft(dft_c, dft_sneg, fa, fb, _tile(seq, 1024), _tile(batch * FNET_WIDTH, 1024), _tile(seq, 1024))
        x2 = _merge(x2, ya, u, v, yf, gates, p["sgu_w"], p["sgu_bias"], p["wpa"], p["wps"], p["wpf"], p["wo"],
                    seq, tm)
        xr, meta, cnt = _route(x2, p["ffn_norm"], p["w_rg_hi"], p["w_rg_lo"], p["b_rg"], utri, tm)
        dest, tt = _assign(cnt, meta, tmx)
        dest = dest.reshape(T)
        tt = tt.reshape(META_ROWS * LANES)
        xs = _dispatch(dest, tt, xr, T + N_GROUPS * tmx, tm)
        ys = _experts(tt, xs, p["ffn_norm"], p["w1"], p["w3"], p["w2"], tmx)
        x2 = _unsort(dest, ys, T, tm)
    return x2.reshape(batch, seq, D_MODEL)
```

```python
import functools
import math

import numpy as np
import jax
import jax.numpy as jnp
from jax import lax
from jax.experimental import pallas as pl
from jax.experimental.pallas import tpu as pltpu

F32 = jnp.float32
BF16 = jnp.bfloat16

D_MODEL = 1024
N_HEADS = 8
QK_NOPE = 64
QK_ROPE = 32
QK_HEAD = QK_NOPE + QK_ROPE
V_HEAD = 64
Q_LORA = 384
KV_LORA = 256
ROPE_THETA = 10000.0
CHUNK = 128
SGU_GROUPS = 8
SGU_GROUP_DIM = 64
SGU_WIDTH = SGU_GROUPS * SGU_GROUP_DIM
FNET_GROUPS = 4
FNET_GROUP_DIM = 128
FNET_WIDTH = FNET_GROUPS * FNET_GROUP_DIM
N_BRANCH = 3
N_GROUPS = 4
EXPERTS_PER_GROUP = 8
N_EXPERTS = N_GROUPS * EXPERTS_PER_GROUP
D_EXPERT = 256
EPS = 1e-6

LANES = 128
HEAD_SLAB = LANES
SHIFT_LANE = QK_HEAD
O_CQ = 0
O_CKV = O_CQ + Q_LORA
O_KR = O_CKV + KV_LORA
O_SGU = O_KR + HEAD_SLAB
O_FN = O_SGU + 2 * SGU_WIDTH
O_GATE = O_FN + FNET_WIDTH
N_IN_EXT = O_GATE + N_BRANCH * D_MODEL

V7X_VMEM_BYTES = 64 * 1024 * 1024


def _cparams(dims, vmem_mb):
    return pltpu.CompilerParams(dimension_semantics=dims,
                                vmem_limit_bytes=min(vmem_mb * 1024 * 1024, V7X_VMEM_BYTES - (6 << 20)))


def _const_spec(shape):
    nd = len(shape)
    return pl.BlockSpec(shape, lambda *_: (0,) * nd)


def _inproj_kernel(x_ref, g_ref, w_ref, qan_ref, kvan_ref, lng_ref, lnb_ref, cs_ref,
                   cq_ref, ckv_ref, kr_ref, u_ref, v_ref, fa_ref, fb_ref, gate_ref):
    x = x_ref[...]
    h = (x * lax.rsqrt(jnp.mean(x * x, axis=-1, keepdims=True) + EPS) * g_ref[...]).astype(BF16)

    def seg(a, n):
        return jnp.dot(h, w_ref[:, a:a + n], preferred_element_type=F32)

    cq = seg(O_CQ, Q_LORA)
    cq_ref[...] = (cq * lax.rsqrt(jnp.mean(cq * cq, axis=-1, keepdims=True) + EPS) * qan_ref[...]).astype(BF16)
    ckv = seg(O_CKV, KV_LORA)
    ckv_ref[...] = (ckv * lax.rsqrt(jnp.mean(ckv * ckv, axis=-1, keepdims=True) + EPS) * kvan_ref[...]).astype(BF16)
    kr_ref[...] = seg(O_KR, HEAD_SLAB).astype(BF16)

    u = jax.nn.gelu(seg(O_SGU, SGU_WIDTH))
    u_ref[...] = u.astype(BF16)
    v = jax.nn.gelu(seg(O_SGU + SGU_WIDTH, SGU_WIDTH))
    mu = jnp.mean(v, axis=-1, keepdims=True)
    vc = v - mu
    var = jnp.mean(vc * vc, axis=-1, keepdims=True)
    v_ref[...] = (vc * lax.rsqrt(var + EPS) * lng_ref[...] + lnb_ref[...]).astype(BF16)

    zf = seg(O_FN, FNET_WIDTH).astype(BF16)
    for g in range(FNET_GROUPS):
        sl = slice(g * FNET_GROUP_DIM, (g + 1) * FNET_GROUP_DIM)
        ab = jnp.dot(zf[:, sl], cs_ref[...], preferred_element_type=F32)
        fa_ref[:, sl] = ab[:, :FNET_GROUP_DIM].astype(BF16)
        fb_ref[:, sl] = ab[:, FNET_GROUP_DIM:].astype(BF16)

    for j in range(N_BRANCH):
        gl = seg(O_GATE + j * D_MODEL, D_MODEL)
        gate_ref[:, j * D_MODEL:(j + 1) * D_MODEL] = (1.0 / (1.0 + jnp.exp(-gl))).astype(BF16)


def _inproj(x2, g, w_ext, qan, kvan, lng, lnb, cs, batch, seq, tm):
    T = x2.shape[0]
    n_s = seq // tm
    row = lambda n: pl.BlockSpec((tm, n), lambda i: (i, 0))
    sb = pl.BlockSpec((tm, FNET_WIDTH), lambda i: (i % n_s, i // n_s))
    outs = [
        jax.ShapeDtypeStruct((T, Q_LORA), BF16),
        jax.ShapeDtypeStruct((T, KV_LORA), BF16),
        jax.ShapeDtypeStruct((T, HEAD_SLAB), BF16),
        jax.ShapeDtypeStruct((T, SGU_WIDTH), BF16),
        jax.ShapeDtypeStruct((T, SGU_WIDTH), BF16),
        jax.ShapeDtypeStruct((seq, batch * FNET_WIDTH), BF16),
        jax.ShapeDtypeStruct((seq, batch * FNET_WIDTH), BF16),
        jax.ShapeDtypeStruct((T, N_BRANCH * D_MODEL), BF16),
    ]
    return pl.pallas_call(
        _inproj_kernel,
        grid=(T // tm,),
        in_specs=[row(D_MODEL), _const_spec((1, D_MODEL)), _const_spec((D_MODEL, N_IN_EXT)),
                  _const_spec((1, Q_LORA)), _const_spec((1, KV_LORA)),
                  _const_spec((1, SGU_WIDTH)), _const_spec((1, SGU_WIDTH)),
                  _const_spec((FNET_GROUP_DIM, 2 * FNET_GROUP_DIM))],
        out_specs=[row(Q_LORA), row(KV_LORA), row(HEAD_SLAB), row(SGU_WIDTH), row(SGU_WIDTH),
                   sb, sb, row(N_BRANCH * D_MODEL)],
        out_shape=outs,
        compiler_params=_cparams(("arbitrary",), 52),
        name="inproj",
    )(x2, g, w_ext, qan, kvan, lng, lnb, cs)


def _head_norm_rope(y, gain, ct, st, keep):
    ss = jnp.sum(jnp.where(keep, y * y, 0.0), axis=-1, keepdims=True)
    y = y * lax.rsqrt(ss * (1.0 / QK_HEAD) + EPS) * gain
    return y * ct + pltpu.roll(y, LANES - QK_ROPE, 1) * st


def _mla_prep_kernel(cq_ref, ckv_ref, kr_ref, ct_ref, st_ref, wq_ref, wk_ref, wv_ref, gq_ref, gk_ref,
                     q_ref, k_ref, v_ref):
    cq = cq_ref[...]
    ckv = ckv_ref[...]
    kr = kr_ref[...].astype(F32)
    ct = ct_ref[...]
    st = st_ref[...]
    lane = lax.broadcasted_iota(jnp.int32, ct.shape, 1)
    keep = lane < QK_HEAD
    is_v = lane < V_HEAD
    q_all = jnp.dot(cq, wq_ref[...], preferred_element_type=F32)
    k_all = jnp.dot(ckv, wk_ref[...], preferred_element_type=F32)
    v_all = jnp.dot(ckv, wv_ref[...], preferred_element_type=F32)
    for h in range(N_HEADS):
        sl = slice(h * HEAD_SLAB, (h + 1) * HEAD_SLAB)
        qh = _head_norm_rope(q_all[:, sl], gq_ref[0:1, :], ct, st, keep) + gq_ref[1:2, :]
        q_ref[0, h] = qh.astype(BF16)
        kh = _head_norm_rope(k_all[:, sl] + kr, gk_ref[0:1, :], ct, st, keep) + gk_ref[1:2, :]
        k_ref[0, h] = kh.T.astype(BF16)
        v_ref[0, h] = jnp.where(is_v, v_all[:, sl], 1.0).astype(BF16)


def _mla_prep(cqn, ckvn, kr, ctab, stab, wq, wk, wv, gq, gk, batch, seq, tm):
    T = cqn.shape[0]
    n_s = seq // tm
    row = lambda n: pl.BlockSpec((tm, n), lambda i: (i, 0))
    hb = pl.BlockSpec((1, N_HEADS, tm, HEAD_SLAB), lambda i: (i // n_s, 0, i % n_s, 0))
    shp = jax.ShapeDtypeStruct((batch, N_HEADS, seq, HEAD_SLAB), BF16)
    return pl.pallas_call(
        _mla_prep_kernel,
        grid=(T // tm,),
        in_specs=[row(Q_LORA), row(KV_LORA), row(HEAD_SLAB), row(HEAD_SLAB), row(HEAD_SLAB),
                  _const_spec((Q_LORA, N_HEADS * HEAD_SLAB)), _const_spec((KV_LORA, N_HEADS * HEAD_SLAB)),
                  _const_spec((KV_LORA, N_HEADS * HEAD_SLAB)),
                  _const_spec((2, HEAD_SLAB)), _const_spec((2, HEAD_SLAB))],
        out_specs=[hb, pl.BlockSpec((1, N_HEADS, HEAD_SLAB, tm), lambda i: (i // n_s, 0, 0, i % n_s)), hb],
        out_shape=[shp, jax.ShapeDtypeStruct((batch, N_HEADS, HEAD_SLAB, seq), BF16), shp],
        compiler_params=_cparams(("arbitrary",), 32),
        name="mla_prep",
    )(cqn, ckvn, kr, ctab, stab, wq, wk, wv, gq, gk)


HEADS_PER_STEP = LANES // V_HEAD


ROW_CHAINS = 2
SAFE_SHIFT_LIMIT = 60.0


def _attn_kernel(safe_ref, q_ref, kt_ref, v_ref, o_ref):
    tq = q_ref.shape[2]
    rows_per_chain = tq // ROW_CHAINS
    lane = lax.broadcasted_iota(jnp.int32, (rows_per_chain, LANES), 1)
    low = lane < V_HEAD

    def finish(rows, outs):
        outs = [o / jnp.where(low, pltpu.roll(o, V_HEAD, 1), 1.0) for o in outs]
        o_ref[0, rows, :] = jnp.where(low, outs[0], pltpu.roll(outs[1], V_HEAD, 1)).astype(BF16)

    @pl.when(safe_ref[0] != 0)
    def _():
        for r in range(ROW_CHAINS):
            rows = slice(r * rows_per_chain, (r + 1) * rows_per_chain)
            ps = [jnp.exp2(jnp.dot(q_ref[0, j, rows, :], kt_ref[0, j], preferred_element_type=F32).astype(BF16))
                  for j in range(HEADS_PER_STEP)]
            finish(rows, [jnp.dot(ps[j], v_ref[0, j], preferred_element_type=F32) for j in range(HEADS_PER_STEP)])

    @pl.when(safe_ref[0] == 0)
    def _():
        for r in range(ROW_CHAINS):
            rows = slice(r * rows_per_chain, (r + 1) * rows_per_chain)
            ss = [jnp.dot(q_ref[0, j, rows, :], kt_ref[0, j], preferred_element_type=F32)
                  for j in range(HEADS_PER_STEP)]
            ps = [jnp.exp2((s - jnp.max(s, axis=-1, keepdims=True)).astype(BF16)) for s in ss]
            finish(rows, [jnp.dot(ps[j], v_ref[0, j], preferred_element_type=F32) for j in range(HEADS_PER_STEP)])


def _attention(safe, q, k, v, tq):
    batch, _, seq, _ = q.shape
    n_hp = N_HEADS // HEADS_PER_STEP
    return pl.pallas_call(
        _attn_kernel,
        grid_spec=pltpu.PrefetchScalarGridSpec(
            num_scalar_prefetch=1, grid=(batch, n_hp, seq // tq),
            in_specs=[pl.BlockSpec((1, HEADS_PER_STEP, tq, HEAD_SLAB), lambda b, h, i, f: (b, h, i, 0)),
                      pl.BlockSpec((1, HEADS_PER_STEP, HEAD_SLAB, seq), lambda b, h, i, f: (b, h, 0, 0)),
                      pl.BlockSpec((1, HEADS_PER_STEP, seq, HEAD_SLAB), lambda b, h, i, f: (b, h, 0, 0))],
            out_specs=pl.BlockSpec((1, tq, LANES), lambda b, h, i, f: (b, i, h))),
        out_shape=jax.ShapeDtypeStruct((batch, seq, N_HEADS * V_HEAD), BF16),
        compiler_params=_cparams(("arbitrary", "arbitrary", "arbitrary"), 52),
        name="attention",
    )(safe, q, k, v)


def _dft_kernel(c_ref, s_ref, a_ref, b_ref, o_ref, acc_ref, *, scale):
    kk = pl.program_id(2)

    @pl.when(kk == 0)
    def _():
        acc_ref[...] = jnp.zeros_like(acc_ref)

    acc_ref[...] += (jnp.dot(c_ref[...], a_ref[...], preferred_element_type=F32)
                     + jnp.dot(s_ref[...], b_ref[...], preferred_element_type=F32))

    @pl.when(kk == pl.num_programs(2) - 1)
    def _():
        o_ref[...] = (acc_ref[...] * scale).astype(BF16)


def _seq_dft(ctab, sneg, fa, fb, tm, tn, tk):
    seq, width = fa.shape
    scale = 1.0 / math.sqrt(seq * FNET_GROUP_DIM)
    return pl.pallas_call(
        functools.partial(_dft_kernel, scale=scale),
        grid=(seq // tm, width // tn, seq // tk),
        in_specs=[pl.BlockSpec((tm, tk), lambda i, j, k: (i, k)),
                  pl.BlockSpec((tm, tk), lambda i, j, k: (i, k)),
                  pl.BlockSpec((tk, tn), lambda i, j, k: (k, j)),
                  pl.BlockSpec((tk, tn), lambda i, j, k: (k, j))],
        out_specs=pl.BlockSpec((tm, tn), lambda i, j, k: (i, j)),
        out_shape=jax.ShapeDtypeStruct((seq, width), BF16),
        scratch_shapes=[pltpu.VMEM((tm, tn), F32)],
        compiler_params=_cparams(("arbitrary", "arbitrary", "arbitrary"), 40),
        name="seq_dft",
    )(ctab, sneg, fa, fb)


def _merge_kernel(x_ref, ya_ref, u_ref, v_ref, yf_ref, gate_ref, ws_ref, bias_ref,
                  wpa_ref, wps_ref, wpf_ref, wo_ref, o_ref, ysgu_ref):
    tm = x_ref.shape[0]
    lane = lax.broadcasted_iota(jnp.int32, (CHUNK, LANES), 1)
    low = lane < SGU_GROUP_DIM
    groups_per_slab = LANES // SGU_GROUP_DIM
    for c in range(tm // CHUNK):
        rows = slice(c * CHUNK, (c + 1) * CHUNK)
        for j in range(SGU_WIDTH // LANES):
            cols = slice(j * LANES, (j + 1) * LANES)
            vb = v_ref[rows, cols]
            ma = jnp.dot(ws_ref[groups_per_slab * j], vb, preferred_element_type=F32)
            mb = jnp.dot(ws_ref[groups_per_slab * j + 1], vb, preferred_element_type=F32)
            mixed = jnp.where(low, ma, mb) + bias_ref[:, cols]
            ysgu_ref[rows, cols] = (u_ref[rows, cols].astype(F32) * mixed).astype(BF16)
    pa = jnp.dot(ya_ref[...], wpa_ref[...], preferred_element_type=F32)
    merged = gate_ref[:, 0:D_MODEL].astype(F32) * pa
    ps = jnp.dot(ysgu_ref[...], wps_ref[...], preferred_element_type=F32)
    merged += gate_ref[:, D_MODEL:2 * D_MODEL].astype(F32) * ps
    pf = jnp.dot(yf_ref[...], wpf_ref[...], preferred_element_type=F32)
    merged += gate_ref[:, 2 * D_MODEL:3 * D_MODEL].astype(F32) * pf
    o_ref[...] = x_ref[...] + jnp.dot(merged.astype(BF16), wo_ref[...], preferred_element_type=F32)


def _merge(x2, ya, u, v, yf, gates, ws, bias, wpa, wps, wpf, wo, seq, tm):
    T = x2.shape[0]
    n_s = seq // tm
    row = lambda n: pl.BlockSpec((tm, n), lambda i: (i, 0))
    return pl.pallas_call(
        _merge_kernel,
        grid=(T // tm,),
        in_specs=[row(D_MODEL), row(N_HEADS * V_HEAD), row(SGU_WIDTH), row(SGU_WIDTH),
                  pl.BlockSpec((tm, FNET_WIDTH), lambda i: (i % n_s, i // n_s)),
                  row(N_BRANCH * D_MODEL),
                  _const_spec((SGU_GROUPS, CHUNK, CHUNK)), _const_spec((CHUNK, SGU_WIDTH)),
                  _const_spec((N_HEADS * V_HEAD, D_MODEL)), _const_spec((SGU_WIDTH, D_MODEL)),
                  _const_spec((FNET_WIDTH, D_MODEL)), _const_spec((D_MODEL, D_MODEL))],
        out_specs=row(D_MODEL),
        out_shape=jax.ShapeDtypeStruct((T, D_MODEL), F32),
        scratch_shapes=[pltpu.VMEM((tm, SGU_WIDTH), BF16)],
        compiler_params=_cparams(("arbitrary",), 48),
        name="merge",
    )(x2, ya, u, v, yf, gates, ws, bias, wpa, wps, wpf, wo)


O_GLOGIT = N_EXPERTS


META_ROWS = 8
ROW_EXT = D_MODEL + LANES


def _route_kernel(x_ref, g_ref, whi_ref, wlo_ref, b_ref, utri_ref, xr_ref, meta_ref, cnt_ref):
    x = x_ref[...]
    tm = x.shape[0]
    h = x * lax.rsqrt(jnp.mean(x * x, axis=-1, keepdims=True) + EPS) * g_ref[...]
    hhi = h.astype(BF16)
    hlo = (h - hhi.astype(F32)).astype(BF16)
    logits = (jnp.dot(hhi, whi_ref[...], preferred_element_type=F32)
              + jnp.dot(hlo, whi_ref[...], preferred_element_type=F32)
              + jnp.dot(hhi, wlo_ref[...], preferred_element_type=F32)) + b_ref[...]
    lane = lax.broadcasted_iota(jnp.int32, logits.shape, 1)
    lane_f = lane.astype(F32)
    neg = -jnp.inf
    far = float(2 * LANES)

    is_g = (lane >= O_GLOGIT) & (lane < O_GLOGIT + N_GROUPS)
    gl = jnp.where(is_g, logits, neg)
    gmax = jnp.max(gl, axis=-1, keepdims=True)
    gsel = jnp.min(jnp.where(gl == gmax, lane_f, far), axis=-1, keepdims=True) - float(O_GLOGIT)
    gw = 1.0 / jnp.sum(jnp.where(is_g, jnp.exp(gl - gmax), 0.0), axis=-1, keepdims=True)

    lane_grp = lax.shift_right_logical(lane, 3).astype(F32)
    in_grp = (lane < N_EXPERTS) & (lane_grp == gsel)
    el = jnp.where(in_grp, logits, neg)
    e1 = jnp.max(el, axis=-1, keepdims=True)
    i1 = jnp.min(jnp.where(el == e1, lane_f, far), axis=-1, keepdims=True)
    el2 = jnp.where(lane_f == i1, neg, el)
    e2 = jnp.max(el2, axis=-1, keepdims=True)
    i2 = jnp.min(jnp.where(el2 == e2, lane_f, far), axis=-1, keepdims=True)
    d = jnp.exp(e2 - e1)
    w1 = gw / (1.0 + d)
    w2 = w1 * d
    xr_ref[:, :D_MODEL] = x
    xr_ref[:, D_MODEL:] = jnp.where(lane_f == i1, w1, jnp.where(lane_f == i2, w2, 0.0))

    onehot_t = jnp.where((lane_f == gsel) & (lane < N_GROUPS), 1.0, 0.0).T[0:META_ROWS, :]
    before = jnp.dot(onehot_t, utri_ref[...], preferred_element_type=F32)
    gid = lax.broadcasted_iota(jnp.int32, (META_ROWS, tm), 0).astype(F32)
    gsel_t = jnp.sum(onehot_t * gid, axis=0, keepdims=True)
    rank_t = jnp.sum(onehot_t * before, axis=0, keepdims=True)
    row_id = lax.broadcasted_iota(jnp.int32, (META_ROWS, tm), 0)
    meta_ref[0] = jnp.where(row_id == 0, gsel_t, jnp.where(row_id == 1, rank_t, 0.0)).astype(jnp.int32)
    cnt = jnp.sum(onehot_t, axis=1, keepdims=True)
    cnt_ref[0] = jnp.broadcast_to(cnt, (META_ROWS, LANES)).astype(jnp.int32)


def _route(x2, g, whi, wlo, b, utri, tm):
    T = x2.shape[0]
    n_tiles = T // tm
    row = lambda n: pl.BlockSpec((tm, n), lambda i: (i, 0))
    return pl.pallas_call(
        _route_kernel,
        grid=(n_tiles,),
        in_specs=[row(D_MODEL), _const_spec((1, D_MODEL)), _const_spec((D_MODEL, LANES)),
                  _const_spec((D_MODEL, LANES)), _const_spec((1, LANES)), _const_spec((tm, tm))],
        out_specs=[row(ROW_EXT), pl.BlockSpec((1, META_ROWS, tm), lambda i: (i, 0, 0)),
                   pl.BlockSpec((1, META_ROWS, LANES), lambda i: (i, 0, 0))],
        out_shape=[jax.ShapeDtypeStruct((T, ROW_EXT), F32),
                   jax.ShapeDtypeStruct((n_tiles, META_ROWS, tm), jnp.int32),
                   jax.ShapeDtypeStruct((n_tiles, META_ROWS, LANES), jnp.int32)],
        compiler_params=_cparams(("arbitrary",), 40),
        name="route",
    )(x2, g, whi, wlo, b, utri)


TT_GROUP, TT_USED, TT_PAD_LO, TT_PAD_HI = 0, 1, 2, 3


def _assign_kernel(cnt_ref, meta_ref, dest_ref, tt_ref, *, tmx):
    n_tiles, _, tm = meta_ref.shape
    shift = tmx.bit_length() - 1
    cnt = cnt_ref[...]
    tot = jnp.sum(cnt, axis=0)
    padded = lax.shift_left(lax.shift_right_logical(tot + (tmx - 1), shift), shift)
    starts, ends = [], []
    run = jnp.zeros((1, LANES), jnp.int32)
    for g in range(N_GROUPS):
        starts.append(run)
        run = run + padded[g:g + 1, :]
        ends.append(run)

    def widen(v):
        return jnp.concatenate([v] * (tm // LANES), axis=1)

    def body(i, seen):
        gs = meta_ref[i, 0:1, :]
        dest = meta_ref[i, 1:2, :]
        for g in range(N_GROUPS):
            dest = dest + jnp.where(gs == g, widen(starts[g] + seen[g]), 0)
        dest_ref[i] = dest
        c = cnt_ref[i]
        return tuple(seen[g] + c[g:g + 1, :] for g in range(N_GROUPS))

    lax.fori_loop(0, n_tiles, body, tuple(jnp.zeros((1, LANES), jnp.int32) for _ in range(N_GROUPS)))

    lane = lax.broadcasted_iota(jnp.int32, (1, LANES), 1)
    first_row = lane * tmx
    grp = jnp.zeros((1, LANES), jnp.int32)
    for g in range(N_GROUPS - 1):
        grp = grp + jnp.where(first_row >= ends[g], 1, 0)
    used = lax.shift_right_logical(ends[-1], shift)
    pad_lo = jnp.zeros((1, LANES), jnp.int32)
    pad_hi = jnp.zeros((1, LANES), jnp.int32)
    for g in range(N_GROUPS):
        pad_lo = pad_lo + jnp.where(lane == g, starts[g] + tot[g:g + 1, :], 0)
        pad_hi = pad_hi + jnp.where(lane == g, ends[g], 0)
    n_rows = (n_tiles * tm // tmx + N_GROUPS) * tmx
    pad_lo = pad_lo + jnp.where(lane == N_GROUPS, ends[-1], 0)
    pad_hi = pad_hi + jnp.where(lane == N_GROUPS, n_rows, 0)
    row_id = lax.broadcasted_iota(jnp.int32, (META_ROWS, LANES), 0)
    tt_ref[...] = jnp.where(row_id == TT_GROUP, grp,
                            jnp.where(row_id == TT_USED, used,
                                      jnp.where(row_id == TT_PAD_LO, pad_lo,
                                                jnp.where(row_id == TT_PAD_HI, pad_hi, 0))))


def _assign(cnt, meta, tmx):
    n_tiles, _, tm = meta.shape
    return pl.pallas_call(
        functools.partial(_assign_kernel, tmx=tmx),
        grid=(1,),
        in_specs=[_const_spec(cnt.shape), _const_spec(meta.shape)],
        out_specs=[_const_spec((n_tiles, 1, tm)), _const_spec((META_ROWS, LANES))],
        out_shape=[jax.ShapeDtypeStruct((n_tiles, 1, tm), jnp.int32),
                   jax.ShapeDtypeStruct((META_ROWS, LANES), jnp.int32)],
        compiler_params=_cparams(("arbitrary",), 32),
        name="assign",
    )(cnt, meta)


def _dispatch_kernel(dest_ref, tt_ref, xr_ref, zero_ref, xs_hbm, sem, *, tm):
    base = pl.program_id(0) * tm

    def issue(r, c):
        pltpu.make_async_copy(xr_ref.at[pl.ds(r, 1)], xs_hbm.at[pl.ds(dest_ref[base + r], 1)], sem).start()
        return c

    lax.fori_loop(0, tm, issue, 0, unroll=8)
    pltpu.make_async_copy(xr_ref, xs_hbm.at[pl.ds(0, tm)], sem).wait()

    @pl.when(pl.program_id(0) == 0)
    def _():
        for g in range(N_GROUPS + 1):
            lo = tt_ref[TT_PAD_LO * LANES + g]
            hi = tt_ref[TT_PAD_HI * LANES + g]

            def fill(r, c):
                pltpu.make_async_copy(zero_ref.at[pl.ds(0, 1)], xs_hbm.at[pl.ds(r, 1)], sem).start()
                return c

            lax.fori_loop(lo, hi, fill, 0)

            def drain(r, c):
                pltpu.make_async_copy(zero_ref.at[pl.ds(0, 1)], xs_hbm.at[pl.ds(0, 1)], sem).wait()
                return c

            lax.fori_loop(lo, hi, drain, 0)


def _dispatch(dest, tt, xr, n_rows, tm):
    T, width = xr.shape
    zero = jnp.zeros((META_ROWS, width), xr.dtype)
    return pl.pallas_call(
        functools.partial(_dispatch_kernel, tm=tm),
        grid_spec=pltpu.PrefetchScalarGridSpec(
            num_scalar_prefetch=2, grid=(T // tm,),
            in_specs=[pl.BlockSpec((tm, width), lambda i, d, t: (i, 0)),
                      pl.BlockSpec((META_ROWS, width), lambda i, d, t: (0, 0))],
            out_specs=pl.BlockSpec(memory_space=pl.ANY),
            scratch_shapes=[pltpu.SemaphoreType.DMA(())]),
        out_shape=jax.ShapeDtypeStruct((n_rows, width), xr.dtype),
        compiler_params=_cparams(("arbitrary",), 16),
        name="dispatch",
    )(dest, tt, xr, zero)


def _unsort_kernel(dest_ref, ys_hbm, y_ref, sem, *, tm):
    base = pl.program_id(0) * tm

    def issue(r, c):
        pltpu.make_async_copy(ys_hbm.at[pl.ds(dest_ref[base + r], 1)], y_ref.at[pl.ds(r, 1)], sem).start()
        return c

    lax.fori_loop(0, tm, issue, 0, unroll=8)
    pltpu.make_async_copy(ys_hbm.at[pl.ds(0, tm)], y_ref, sem).wait()


def _unsort(dest, ys, T, tm):
    width = ys.shape[1]
    return pl.pallas_call(
        functools.partial(_unsort_kernel, tm=tm),
        grid_spec=pltpu.PrefetchScalarGridSpec(
            num_scalar_prefetch=1, grid=(T // tm,), in_specs=[pl.BlockSpec(memory_space=pl.ANY)],
            out_specs=pl.BlockSpec((tm, width), lambda i, d: (i, 0)),
            scratch_shapes=[pltpu.SemaphoreType.DMA(())]),
        out_shape=jax.ShapeDtypeStruct((T, width), ys.dtype),
        compiler_params=_cparams(("arbitrary",), 16),
        name="unsort",
    )(dest, ys)


def _experts_kernel(tt_ref, xs_ref, g_ref, w1_ref, w3_ref, w2_ref, o_ref):
    j = pl.program_id(0)

    @pl.when(j < tt_ref[TT_USED * LANES])
    def _():
        grp = tt_ref[TT_GROUP * LANES + j]
        x = xs_ref[:, :D_MODEL]
        cw = xs_ref[:, D_MODEL:]
        h = (x * lax.rsqrt(jnp.mean(x * x, axis=-1, keepdims=True) + EPS) * g_ref[...]).astype(BF16)
        lane = lax.broadcasted_iota(jnp.int32, cw.shape, 1)
        acc = x
        for e in range(EXPERTS_PER_GROUP):
            a = jnp.dot(h, w1_ref[e], preferred_element_type=F32)
            b = jnp.dot(h, w3_ref[e], preferred_element_type=F32)
            col = jnp.sum(jnp.where(lane == grp * EXPERTS_PER_GROUP + e, cw, 0.0), axis=-1, keepdims=True)
            hid = (a / (1.0 + jnp.exp(-a))) * b * col
            acc = acc + jnp.dot(hid.astype(BF16), w2_ref[e], preferred_element_type=F32)
        o_ref[...] = acc

    @pl.when(j >= tt_ref[TT_USED * LANES])
    def _():
        o_ref[...] = jnp.zeros_like(o_ref)


def _experts(tt, xs, g, w1, w3, w2, tmx):
    n_rows = xs.shape[0]
    wspec = lambda a, b: pl.BlockSpec((EXPERTS_PER_GROUP, a, b), lambda j, tt: (tt[TT_GROUP * LANES + j], 0, 0))
    return pl.pallas_call(
        _experts_kernel,
        grid_spec=pltpu.PrefetchScalarGridSpec(
            num_scalar_prefetch=1, grid=(n_rows // tmx,),
            in_specs=[pl.BlockSpec((tmx, ROW_EXT), lambda j, tt: (j, 0)),
                      pl.BlockSpec((1, D_MODEL), lambda j, tt: (0, 0)),
                      wspec(D_MODEL, D_EXPERT), wspec(D_MODEL, D_EXPERT), wspec(D_EXPERT, D_MODEL)],
            out_specs=pl.BlockSpec((tmx, D_MODEL), lambda j, tt: (j, 0))),
        out_shape=jax.ShapeDtypeStruct((n_rows, D_MODEL), F32),
        compiler_params=_cparams(("arbitrary",), 52),
        name="experts",
    )(tt, xs, g, w1, w3, w2)


def _rope_swap(w):
    half = QK_ROPE // 2
    return jnp.concatenate([w, w[..., half:], w[..., :half]], axis=-1)


def _head_gain(g):
    return jnp.concatenate([g[:QK_NOPE], _rope_swap(g[QK_NOPE:])])[None, :].astype(F32)


def _score_shift(q_norm, k_norm):
    gq = _head_gain(q_norm) * (QK_HEAD ** -0.5 * math.log2(math.e))
    gk = _head_gain(k_norm)
    c = 1.02 * QK_HEAD * jnp.max(jnp.abs(gq)) * jnp.max(jnp.abs(gk))
    at_shift = (jnp.arange(LANES) == SHIFT_LANE)[None, :]
    return dict(gq=jnp.concatenate([gq, jnp.where(at_shift, -c, 0.0)], axis=0),
                gk=jnp.concatenate([gk, jnp.where(at_shift, 1.0, 0.0)], axis=0),
                safe=(c < SAFE_SHIFT_LIMIT).astype(jnp.int32).reshape(1))


def _layer_params(l, attn_norm, w_in, q_a_norm, w_uq, kv_a_norm, w_ukv, q_norm, k_norm, sgu_ln_g, sgu_ln_b,
                  sgu_w, sgu_b, w_proj_attn, w_proj_sgu, w_proj_fnet, w_out, ffn_norm, w_group, b_group,
                  w_router, b_router, w1, w3, w2):
    wi = w_in[l]
    c0 = Q_LORA + KV_LORA
    c1 = c0 + QK_ROPE
    w_ext = jnp.concatenate(
        [wi[:, :c0], jnp.zeros((D_MODEL, QK_NOPE), F32), _rope_swap(wi[:, c0:c1]), wi[:, c1:]], axis=1).astype(BF16)
    wq = w_uq[l].reshape(Q_LORA, N_HEADS, QK_HEAD)
    wq = jnp.concatenate([wq[..., :QK_NOPE], _rope_swap(wq[..., QK_NOPE:])], axis=-1)
    wq = wq.reshape(Q_LORA, N_HEADS * HEAD_SLAB).astype(BF16)
    wkv = w_ukv[l].reshape(KV_LORA, N_HEADS, QK_NOPE + V_HEAD)
    zpad = jnp.zeros((KV_LORA, N_HEADS, HEAD_SLAB - QK_NOPE), F32)
    wk = jnp.concatenate([wkv[..., :QK_NOPE], zpad], axis=-1).reshape(KV_LORA, N_HEADS * HEAD_SLAB).astype(BF16)
    wv = jnp.concatenate([wkv[..., QK_NOPE:], zpad], axis=-1).reshape(KV_LORA, N_HEADS * HEAD_SLAB).astype(BF16)
    w_rg = jnp.concatenate([w_router[l], w_group[l], jnp.zeros((D_MODEL, LANES - N_EXPERTS - N_GROUPS), F32)], axis=1)
    w_rg_hi = w_rg.astype(BF16)
    w_rg_lo = (w_rg - w_rg_hi.astype(F32)).astype(BF16)
    b_rg = jnp.concatenate([b_router[l], b_group[l], jnp.zeros((LANES - N_EXPERTS - N_GROUPS,), F32)])[None, :]
    return dict(
        attn_norm=attn_norm[l][None, :], w_ext=w_ext,
        q_a_norm=q_a_norm[l][None, :], kv_a_norm=kv_a_norm[l][None, :],
        ln_g=sgu_ln_g[l][None, :], ln_b=sgu_ln_b[l][None, :],
        wq=wq, wk=wk, wv=wv, **_score_shift(q_norm[l], k_norm[l]),
        sgu_w=sgu_w[l].astype(BF16), sgu_bias=jnp.repeat(sgu_b[l].T, SGU_GROUP_DIM, axis=1),
        wpa=w_proj_attn[l].astype(BF16), wps=w_proj_sgu[l].astype(BF16), wpf=w_proj_fnet[l].astype(BF16),
        wo=w_out[l].astype(BF16),
        ffn_norm=ffn_norm[l][None, :], w_rg_hi=w_rg_hi, w_rg_lo=w_rg_lo, b_rg=b_rg,
        w1=w1[l].astype(BF16), w3=w3[l].astype(BF16), w2=w2[l].astype(BF16),
    )


def _rope_slabs(positions):
    inv = 1.0 / (ROPE_THETA ** (jnp.arange(0, QK_ROPE, 2, dtype=F32) / QK_ROPE))
    ang = positions.astype(F32).reshape(-1)[:, None] * inv
    cos, sin = jnp.cos(ang), jnp.sin(ang)
    T = ang.shape[0]
    ctab = jnp.concatenate([jnp.ones((T, QK_NOPE), F32), cos, cos, jnp.zeros((T, QK_ROPE), F32)], axis=1)
    stab = jnp.concatenate([jnp.zeros((T, QK_NOPE), F32), -sin, sin, jnp.zeros((T, QK_ROPE), F32)], axis=1)
    return ctab, stab


def _dft_tables(n):
    r = 1 << ((n.bit_length() - 1) // 2)
    kk = jnp.arange(n, dtype=jnp.int32)[:, None]
    ang_hi = ((kk * (jnp.arange(n // r, dtype=jnp.int32) * r)[None, :]) & (n - 1)).astype(F32) * (2.0 * math.pi / n)
    ang_lo = ((kk * jnp.arange(r, dtype=jnp.int32)[None, :]) & (n - 1)).astype(F32) * (2.0 * math.pi / n)
    ch, sh = jnp.cos(ang_hi)[:, :, None], jnp.sin(ang_hi)[:, :, None]
    cl, sl = jnp.cos(ang_lo)[:, None, :], jnp.sin(ang_lo)[:, None, :]
    c = (ch * cl - sh * sl).reshape(n, n).astype(BF16)
    sneg = (-(sh * cl + ch * sl)).reshape(n, n).astype(BF16)
    m = FNET_GROUP_DIM
    jk = np.outer(np.arange(m), np.arange(m)) % m
    angc = jk.astype(np.float64) * (2.0 * math.pi / m)
    cs = jnp.asarray(np.concatenate([np.cos(angc), np.sin(angc)], axis=1), F32).astype(BF16)
    return c, sneg, cs


def _tile(n, want):
    t = min(n, want)
    assert n % t == 0
    return t


def kernel(x, positions, attn_norm, w_in, q_a_norm, w_uq, kv_a_norm, w_ukv, q_norm, k_norm, sgu_ln_g, sgu_ln_b,
           sgu_w, sgu_b, w_proj_attn, w_proj_sgu, w_proj_fnet, w_out, ffn_norm, w_group, b_group, w_router,
           b_router, w1, w3, w2):
    batch, seq, _ = x.shape
    depth = w_in.shape[0]
    assert seq & (seq - 1) == 0 and seq % CHUNK == 0
    T = batch * seq
    tm = _tile(seq, 512)
    tmx = tm
    ridx = jnp.arange(tm, dtype=jnp.int32)
    utri = (ridx[:, None] < ridx[None, :]).astype(F32)
    ctab, stab = _rope_slabs(positions)
    dft_c, dft_sneg, cs = _dft_tables(seq)
    x2 = x.reshape(T, D_MODEL)
    for l in range(depth):
        p = _layer_params(l, attn_norm, w_in, q_a_norm, w_uq, kv_a_norm, w_ukv, q_norm, k_norm, sgu_ln_g,
                          sgu_ln_b, sgu_w, sgu_b, w_proj_attn, w_proj_sgu, w_proj_fnet, w_out, ffn_norm,
                          w_group, b_group, w_router, b_router, w1, w3, w2)
        cqn, ckvn, kr, u, v, fa, fb, gates = _inproj(
            x2, p["attn_norm"], p["w_ext"], p["q_a_norm"], p["kv_a_norm"], p["ln_g"], p["ln_b"], cs,
            batch, seq, _tile(seq, 256))
        q, k, vv = _mla_prep(cqn, ckvn, kr, ctab, stab, p["wq"], p["wk"], p["wv"], p["gq"], p["gk"],
                             batch, seq, tm)
        ya = _attention(p["safe"], q, k, vv, _tile(seq, 1024)).reshape(T, N_HEADS * V_HEAD)
        yf = _seq_dft(dft_c, dft_sneg, fa, fb, _tile(seq, 1024), _tile(batch * FNET_WIDTH, 1024), _tile(seq, 1024))
        x2 = _merge(x2, ya, u, v, yf, gates, p["sgu_w"], p["sgu_bias"], p["wpa"], p["wps"], p["wpf"], p["wo"],
                    seq, tm)
        xr, meta, cnt = _route(x2, p["ffn_norm"], p["w_rg_hi"], p["w_rg_lo"], p["b_rg"], utri, tm)
        dest, tt = _assign(cnt, meta, tmx)
        dest = dest.reshape(T)
        tt = tt.reshape(META_ROWS * LANES)
        xs = _dispatch(dest, tt, xr, T + N_GROUPS * tmx, tm)
        ys = _experts(tt, xs, p["ffn_norm"], p["w1"], p["w3"], p["w2"], tmx)
        x2 = _unsort(dest, ys, T, tm)
    return x2.reshape(batch, seq, D_MODEL)
```

```python
import functools
import math

import numpy as np
import jax
import jax.numpy as jnp
from jax import lax
from jax.experimental import pallas as pl
from jax.experimental.pallas import tpu as pltpu

F32 = jnp.float32
BF16 = jnp.bfloat16
F8 = jnp.float8_e4m3fn

D_MODEL = 1024
N_HEADS = 8
QK_NOPE = 64
QK_ROPE = 32
QK_HEAD = QK_NOPE + QK_ROPE
V_HEAD = 64
Q_LORA = 384
KV_LORA = 256
ROPE_THETA = 10000.0
CHUNK = 128
SGU_GROUPS = 8
SGU_GROUP_DIM = 64
SGU_WIDTH = SGU_GROUPS * SGU_GROUP_DIM
FNET_GROUPS = 4
FNET_GROUP_DIM = 128
FNET_WIDTH = FNET_GROUPS * FNET_GROUP_DIM
N_BRANCH = 3
N_GROUPS = 4
EXPERTS_PER_GROUP = 8
N_EXPERTS = N_GROUPS * EXPERTS_PER_GROUP
D_EXPERT = 256
EPS = 1e-6

LANES = 128
HEAD_SLAB = LANES
SHIFT_LANE = QK_HEAD
O_CQ = 0
O_CKV = O_CQ + Q_LORA
O_KR = O_CKV + KV_LORA
O_SGU = O_KR + HEAD_SLAB
O_FN = O_SGU + 2 * SGU_WIDTH
O_GATE = O_FN + FNET_WIDTH
N_IN_EXT = O_GATE + N_BRANCH * D_MODEL

V7X_VMEM_BYTES = 64 * 1024 * 1024


def _cparams(dims, vmem_mb):
    return pltpu.CompilerParams(dimension_semantics=dims,
                                vmem_limit_bytes=min(vmem_mb * 1024 * 1024, V7X_VMEM_BYTES - (6 << 20)))


def _const_spec(shape):
    nd = len(shape)
    return pl.BlockSpec(shape, lambda *_: (0,) * nd)


def _inproj_kernel(x_ref, g_ref, w_ref, qan_ref, kvan_ref, lng_ref, lnb_ref, cs_ref,
                   cq_ref, ckv_ref, kr_ref, u_ref, v_ref, fa_ref, fb_ref, gate_ref):
    x = x_ref[...]
    h = (x * lax.rsqrt(jnp.mean(x * x, axis=-1, keepdims=True) + EPS) * g_ref[...]).astype(BF16)

    def seg(a, n):
        return jnp.dot(h, w_ref[:, a:a + n], preferred_element_type=F32)

    cq = seg(O_CQ, Q_LORA)
    cq_ref[...] = (cq * lax.rsqrt(jnp.mean(cq * cq, axis=-1, keepdims=True) + EPS) * qan_ref[...]).astype(BF16)
    ckv = seg(O_CKV, KV_LORA)
    ckv_ref[...] = (ckv * lax.rsqrt(jnp.mean(ckv * ckv, axis=-1, keepdims=True) + EPS) * kvan_ref[...]).astype(BF16)
    kr_ref[...] = seg(O_KR, HEAD_SLAB).astype(BF16)

    u = jax.nn.gelu(seg(O_SGU, SGU_WIDTH))
    u_ref[...] = u.astype(BF16)
    v = jax.nn.gelu(seg(O_SGU + SGU_WIDTH, SGU_WIDTH))
    mu = jnp.mean(v, axis=-1, keepdims=True)
    vc = v - mu
    var = jnp.mean(vc * vc, axis=-1, keepdims=True)
    v_ref[...] = (vc * lax.rsqrt(var + EPS) * lng_ref[...] + lnb_ref[...]).astype(BF16)

    zf = seg(O_FN, FNET_WIDTH).astype(BF16)
    for g in range(FNET_GROUPS):
        sl = slice(g * FNET_GROUP_DIM, (g + 1) * FNET_GROUP_DIM)
        ab = jnp.dot(zf[:, sl], cs_ref[...], preferred_element_type=F32)
        fa_ref[:, sl] = ab[:, :FNET_GROUP_DIM].astype(BF16)
        fb_ref[:, sl] = ab[:, FNET_GROUP_DIM:].astype(BF16)

    for j in range(N_BRANCH):
        gl = seg(O_GATE + j * D_MODEL, D_MODEL)
        gate_ref[:, j * D_MODEL:(j + 1) * D_MODEL] = (1.0 / (1.0 + jnp.exp(-gl))).astype(BF16)


def _inproj(x2, g, w_ext, qan, kvan, lng, lnb, cs, batch, seq, tm):
    T = x2.shape[0]
    n_s = seq // tm
    row = lambda n: pl.BlockSpec((tm, n), lambda i: (i, 0))
    sb = pl.BlockSpec((tm, FNET_WIDTH), lambda i: (i % n_s, i // n_s))
    outs = [
        jax.ShapeDtypeStruct((T, Q_LORA), BF16),
        jax.ShapeDtypeStruct((T, KV_LORA), BF16),
        jax.ShapeDtypeStruct((T, HEAD_SLAB), BF16),
        jax.ShapeDtypeStruct((T, SGU_WIDTH), BF16),
        jax.ShapeDtypeStruct((T, SGU_WIDTH), BF16),
        jax.ShapeDtypeStruct((seq, batch * FNET_WIDTH), BF16),
        jax.ShapeDtypeStruct((seq, batch * FNET_WIDTH), BF16),
        jax.ShapeDtypeStruct((T, N_BRANCH * D_MODEL), BF16),
    ]
    return pl.pallas_call(
        _inproj_kernel,
        grid=(T // tm,),
        in_specs=[row(D_MODEL), _const_spec((1, D_MODEL)), _const_spec((D_MODEL, N_IN_EXT)),
                  _const_spec((1, Q_LORA)), _const_spec((1, KV_LORA)),
                  _const_spec((1, SGU_WIDTH)), _const_spec((1, SGU_WIDTH)),
                  _const_spec((FNET_GROUP_DIM, 2 * FNET_GROUP_DIM))],
        out_specs=[row(Q_LORA), row(KV_LORA), row(HEAD_SLAB), row(SGU_WIDTH), row(SGU_WIDTH),
                   sb, sb, row(N_BRANCH * D_MODEL)],
        out_shape=outs,
        compiler_params=_cparams(("arbitrary",), 52),
        name="inproj",
    )(x2, g, w_ext, qan, kvan, lng, lnb, cs)


def _head_norm_rope(y, gain, ct, st, keep):
    ss = jnp.sum(jnp.where(keep, y * y, 0.0), axis=-1, keepdims=True)
    y = y * lax.rsqrt(ss * (1.0 / QK_HEAD) + EPS) * gain
    return y * ct + pltpu.roll(y, LANES - QK_ROPE, 1) * st


def _mla_prep_kernel(cq_ref, ckv_ref, kr_ref, ct_ref, st_ref, wq_ref, wk_ref, wv_ref, gq_ref, gk_ref,
                     q_ref, k_ref, v_ref):
    cq = cq_ref[...]
    ckv = ckv_ref[...]
    kr = kr_ref[...].astype(F32)
    ct = ct_ref[...]
    st = st_ref[...]
    lane = lax.broadcasted_iota(jnp.int32, ct.shape, 1)
    keep = lane < QK_HEAD
    is_v = lane < V_HEAD
    q_all = jnp.dot(cq, wq_ref[...], preferred_element_type=F32)
    k_all = jnp.dot(ckv, wk_ref[...], preferred_element_type=F32)
    v_all = jnp.dot(ckv, wv_ref[...], preferred_element_type=F32)
    for h in range(N_HEADS):
        sl = slice(h * HEAD_SLAB, (h + 1) * HEAD_SLAB)
        qh = _head_norm_rope(q_all[:, sl], gq_ref[0:1, :], ct, st, keep) + gq_ref[1:2, :]
        q_ref[0, h] = qh.astype(BF16)
        kh = _head_norm_rope(k_all[:, sl] + kr, gk_ref[0:1, :], ct, st, keep) + gk_ref[1:2, :]
        k_ref[0, h] = kh.T.astype(BF16)
        v_ref[0, h] = jnp.where(is_v, v_all[:, sl], 1.0).astype(BF16)


def _mla_prep(cqn, ckvn, kr, ctab, stab, wq, wk, wv, gq, gk, batch, seq, tm):
    T = cqn.shape[0]
    n_s = seq // tm
    row = lambda n: pl.BlockSpec((tm, n), lambda i: (i, 0))
    hb = pl.BlockSpec((1, N_HEADS, tm, HEAD_SLAB), lambda i: (i // n_s, 0, i % n_s, 0))
    shp = jax.ShapeDtypeStruct((batch, N_HEADS, seq, HEAD_SLAB), BF16)
    return pl.pallas_call(
        _mla_prep_kernel,
        grid=(T // tm,),
        in_specs=[row(Q_LORA), row(KV_LORA), row(HEAD_SLAB), row(HEAD_SLAB), row(HEAD_SLAB),
                  _const_spec((Q_LORA, N_HEADS * HEAD_SLAB)), _const_spec((KV_LORA, N_HEADS * HEAD_SLAB)),
                  _const_spec((KV_LORA, N_HEADS * HEAD_SLAB)),
                  _const_spec((2, HEAD_SLAB)), _const_spec((2, HEAD_SLAB))],
        out_specs=[hb, pl.BlockSpec((1, N_HEADS, HEAD_SLAB, tm), lambda i: (i // n_s, 0, 0, i % n_s)), hb],
        out_shape=[shp, jax.ShapeDtypeStruct((batch, N_HEADS, HEAD_SLAB, seq), BF16), shp],
        compiler_params=_cparams(("arbitrary",), 32),
        name="mla_prep",
    )(cqn, ckvn, kr, ctab, stab, wq, wk, wv, gq, gk)


HEADS_PER_STEP = LANES // V_HEAD


ROW_CHAINS = 2
SAFE_SHIFT_LIMIT = 60.0
FP8_SHIFT_LIMIT = 32.0
MODE_ROW_MAX, MODE_BOUND_SHIFT, MODE_FP8_SCORES = 0, 1, 2


def _attn_kernel(safe_ref, q_ref, kt_ref, v_ref, o_ref):
    tq = q_ref.shape[2]
    rows_per_chain = tq // ROW_CHAINS
    lane = lax.broadcasted_iota(jnp.int32, (rows_per_chain, LANES), 1)
    low = lane < V_HEAD

    def finish(rows, outs):
        outs = [o / jnp.where(low, pltpu.roll(o, V_HEAD, 1), 1.0) for o in outs]
        o_ref[0, rows, :] = jnp.where(low, outs[0], pltpu.roll(outs[1], V_HEAD, 1)).astype(BF16)

    def bound_shifted(dtype):
        kts = [kt_ref[0, j].astype(dtype) for j in range(HEADS_PER_STEP)]
        for r in range(ROW_CHAINS):
            rows = slice(r * rows_per_chain, (r + 1) * rows_per_chain)
            ps = [jnp.exp2(jnp.dot(q_ref[0, j, rows, :].astype(dtype), kts[j],
                                   preferred_element_type=F32).astype(BF16)) for j in range(HEADS_PER_STEP)]
            finish(rows, [jnp.dot(ps[j], v_ref[0, j], preferred_element_type=F32) for j in range(HEADS_PER_STEP)])

    @pl.when(safe_ref[0] == MODE_FP8_SCORES)
    def _():
        bound_shifted(F8)

    @pl.when(safe_ref[0] == MODE_BOUND_SHIFT)
    def _():
        bound_shifted(BF16)

    @pl.when(safe_ref[0] == MODE_ROW_MAX)
    def _():
        for r in range(ROW_CHAINS):
            rows = slice(r * rows_per_chain, (r + 1) * rows_per_chain)
            ss = [jnp.dot(q_ref[0, j, rows, :], kt_ref[0, j], preferred_element_type=F32)
                  for j in range(HEADS_PER_STEP)]
            ps = [jnp.exp2((s - jnp.max(s, axis=-1, keepdims=True)).astype(BF16)) for s in ss]
            finish(rows, [jnp.dot(ps[j], v_ref[0, j], preferred_element_type=F32) for j in range(HEADS_PER_STEP)])


def _attention(safe, q, k, v, tq):
    batch, _, seq, _ = q.shape
    n_hp = N_HEADS // HEADS_PER_STEP
    return pl.pallas_call(
        _attn_kernel,
        grid_spec=pltpu.PrefetchScalarGridSpec(
            num_scalar_prefetch=1, grid=(batch, n_hp, seq // tq),
            in_specs=[pl.BlockSpec((1, HEADS_PER_STEP, tq, HEAD_SLAB), lambda b, h, i, f: (b, h, i, 0)),
                      pl.BlockSpec((1, HEADS_PER_STEP, HEAD_SLAB, seq), lambda b, h, i, f: (b, h, 0, 0)),
                      pl.BlockSpec((1, HEADS_PER_STEP, seq, HEAD_SLAB), lambda b, h, i, f: (b, h, 0, 0))],
            out_specs=pl.BlockSpec((1, tq, LANES), lambda b, h, i, f: (b, i, h))),
        out_shape=jax.ShapeDtypeStruct((batch, seq, N_HEADS * V_HEAD), BF16),
        compiler_params=_cparams(("arbitrary", "arbitrary", "arbitrary"), 52),
        name="attention",
    )(safe, q, k, v)


def _dft_kernel(c_ref, s_ref, a_ref, b_ref, o_ref, acc_ref, *, scale):
    kk = pl.program_id(2)

    @pl.when(kk == 0)
    def _():
        acc_ref[...] = jnp.zeros_like(acc_ref)

    acc_ref[...] += (jnp.dot(c_ref[...], a_ref[...], preferred_element_type=F32)
                     + jnp.dot(s_ref[...], b_ref[...], preferred_element_type=F32))

    @pl.when(kk == pl.num_programs(2) - 1)
    def _():
        o_ref[...] = (acc_ref[...] * scale).astype(BF16)


def _seq_dft(ctab, sneg, fa, fb, tm, tn, tk):
    seq, width = fa.shape
    scale = 1.0 / math.sqrt(seq * FNET_GROUP_DIM)
    return pl.pallas_call(
        functools.partial(_dft_kernel, scale=scale),
        grid=(seq // tm, width // tn, seq // tk),
        in_specs=[pl.BlockSpec((tm, tk), lambda i, j, k: (i, k)),
                  pl.BlockSpec((tm, tk), lambda i, j, k: (i, k)),
                  pl.BlockSpec((tk, tn), lambda i, j, k: (k, j)),
                  pl.BlockSpec((tk, tn), lambda i, j, k: (k, j))],
        out_specs=pl.BlockSpec((tm, tn), lambda i, j, k: (i, j)),
        out_shape=jax.ShapeDtypeStruct((seq, width), BF16),
        scratch_shapes=[pltpu.VMEM((tm, tn), F32)],
        compiler_params=_cparams(("arbitrary", "arbitrary", "arbitrary"), 40),
        name="seq_dft",
    )(ctab, sneg, fa, fb)


def _merge_kernel(x_ref, ya_ref, u_ref, v_ref, yf_ref, gate_ref, ws_ref, bias_ref,
                  wpa_ref, wps_ref, wpf_ref, wo_ref, gn_ref, whi_ref, wlo_ref, brg_ref, utri_ref,
                  xr_ref, meta_ref, cnt_ref, ysgu_ref):
    tm = x_ref.shape[0]
    lane = lax.broadcasted_iota(jnp.int32, (CHUNK, LANES), 1)
    low = lane < SGU_GROUP_DIM
    groups_per_slab = LANES // SGU_GROUP_DIM
    for c in range(tm // CHUNK):
        rows = slice(c * CHUNK, (c + 1) * CHUNK)
        for j in range(SGU_WIDTH // LANES):
            cols = slice(j * LANES, (j + 1) * LANES)
            vb = v_ref[rows, cols]
            ma = jnp.dot(ws_ref[groups_per_slab * j], vb, preferred_element_type=F32)
            mb = jnp.dot(ws_ref[groups_per_slab * j + 1], vb, preferred_element_type=F32)
            mixed = jnp.where(low, ma, mb) + bias_ref[:, cols]
            ysgu_ref[rows, cols] = (u_ref[rows, cols].astype(F32) * mixed).astype(BF16)
    pa = jnp.dot(ya_ref[...], wpa_ref[...], preferred_element_type=F32)
    merged = gate_ref[:, 0:D_MODEL].astype(F32) * pa
    ps = jnp.dot(ysgu_ref[...], wps_ref[...], preferred_element_type=F32)
    merged += gate_ref[:, D_MODEL:2 * D_MODEL].astype(F32) * ps
    pf = jnp.dot(yf_ref[...], wpf_ref[...], preferred_element_type=F32)
    merged += gate_ref[:, 2 * D_MODEL:3 * D_MODEL].astype(F32) * pf
    x_mid = x_ref[...] + jnp.dot(merged.astype(BF16), wo_ref[...], preferred_element_type=F32)
    _route_tile(x_mid, gn_ref, whi_ref, wlo_ref, brg_ref, utri_ref, xr_ref, meta_ref, cnt_ref)


def _merge_route(x2, ya, u, v, yf, gates, ws, bias, wpa, wps, wpf, wo, gn, whi, wlo, brg, utri, seq, tm):
    T = x2.shape[0]
    n_s = seq // tm
    n_tiles = T // tm
    row = lambda n: pl.BlockSpec((tm, n), lambda i: (i, 0))
    return pl.pallas_call(
        _merge_kernel,
        grid=(n_tiles,),
        in_specs=[row(D_MODEL), row(N_HEADS * V_HEAD), row(SGU_WIDTH), row(SGU_WIDTH),
                  pl.BlockSpec((tm, FNET_WIDTH), lambda i: (i % n_s, i // n_s)),
                  row(N_BRANCH * D_MODEL),
                  _const_spec((SGU_GROUPS, CHUNK, CHUNK)), _const_spec((CHUNK, SGU_WIDTH)),
                  _const_spec((N_HEADS * V_HEAD, D_MODEL)), _const_spec((SGU_WIDTH, D_MODEL)),
                  _const_spec((FNET_WIDTH, D_MODEL)), _const_spec((D_MODEL, D_MODEL)),
                  _const_spec((1, D_MODEL)), _const_spec((D_MODEL, LANES)), _const_spec((D_MODEL, LANES)),
                  _const_spec((1, LANES)), _const_spec((tm, tm))],
        out_specs=[row(ROW_EXT), pl.BlockSpec((1, META_ROWS, tm), lambda i: (i, 0, 0)),
                   pl.BlockSpec((1, META_ROWS, LANES), lambda i: (i, 0, 0))],
        out_shape=[jax.ShapeDtypeStruct((T, ROW_EXT), F32),
                   jax.ShapeDtypeStruct((n_tiles, META_ROWS, tm), jnp.int32),
                   jax.ShapeDtypeStruct((n_tiles, META_ROWS, LANES), jnp.int32)],
        scratch_shapes=[pltpu.VMEM((tm, SGU_WIDTH), BF16)],
        compiler_params=_cparams(("arbitrary",), 52),
        name="merge_route",
    )(x2, ya, u, v, yf, gates, ws, bias, wpa, wps, wpf, wo, gn, whi, wlo, brg, utri)


O_GLOGIT = N_EXPERTS


META_ROWS = 8
ROW_EXT = D_MODEL + LANES


def _route_tile(x, g_ref, whi_ref, wlo_ref, b_ref, utri_ref, xr_ref, meta_ref, cnt_ref):
    tm = x.shape[0]
    h = x * lax.rsqrt(jnp.mean(x * x, axis=-1, keepdims=True) + EPS) * g_ref[...]
    hhi = h.astype(BF16)
    hlo = (h - hhi.astype(F32)).astype(BF16)
    logits = (jnp.dot(hhi, whi_ref[...], preferred_element_type=F32)
              + jnp.dot(hlo, whi_ref[...], preferred_element_type=F32)
              + jnp.dot(hhi, wlo_ref[...], preferred_element_type=F32)) + b_ref[...]
    lane = lax.broadcasted_iota(jnp.int32, logits.shape, 1)
    lane_f = lane.astype(F32)
    neg = -jnp.inf
    far = float(2 * LANES)

    is_g = (lane >= O_GLOGIT) & (lane < O_GLOGIT + N_GROUPS)
    gl = jnp.where(is_g, logits, neg)
    gmax = jnp.max(gl, axis=-1, keepdims=True)
    gsel = jnp.min(jnp.where(gl == gmax, lane_f, far), axis=-1, keepdims=True) - float(O_GLOGIT)
    gw = 1.0 / jnp.sum(jnp.where(is_g, jnp.exp(gl - gmax), 0.0), axis=-1, keepdims=True)

    lane_grp = lax.shift_right_logical(lane, 3).astype(F32)
    in_grp = (lane < N_EXPERTS) & (lane_grp == gsel)
    el = jnp.where(in_grp, logits, neg)
    e1 = jnp.max(el, axis=-1, keepdims=True)
    i1 = jnp.min(jnp.where(el == e1, lane_f, far), axis=-1, keepdims=True)
    el2 = jnp.where(lane_f == i1, neg, el)
    e2 = jnp.max(el2, axis=-1, keepdims=True)
    i2 = jnp.min(jnp.where(el2 == e2, lane_f, far), axis=-1, keepdims=True)
    d = jnp.exp(e2 - e1)
    w1 = gw / (1.0 + d)
    w2 = w1 * d
    xr_ref[:, :D_MODEL] = x
    xr_ref[:, D_MODEL:] = jnp.where(lane_f == i1, w1, jnp.where(lane_f == i2, w2, 0.0))

    onehot_t = jnp.where((lane_f == gsel) & (lane < N_GROUPS), 1.0, 0.0).T[0:META_ROWS, :]
    before = jnp.dot(onehot_t, utri_ref[...], preferred_element_type=F32)
    gid = lax.broadcasted_iota(jnp.int32, (META_ROWS, tm), 0).astype(F32)
    gsel_t = jnp.sum(onehot_t * gid, axis=0, keepdims=True)
    rank_t = jnp.sum(onehot_t * before, axis=0, keepdims=True)
    row_id = lax.broadcasted_iota(jnp.int32, (META_ROWS, tm), 0)
    meta_ref[0] = jnp.where(row_id == 0, gsel_t, jnp.where(row_id == 1, rank_t, 0.0)).astype(jnp.int32)
    cnt = jnp.sum(onehot_t, axis=1, keepdims=True)
    cnt_ref[0] = jnp.broadcast_to(cnt, (META_ROWS, LANES)).astype(jnp.int32)


TT_GROUP, TT_USED, TT_PAD_LO, TT_PAD_HI = 0, 1, 2, 3


def _assign_kernel(cnt_ref, meta_ref, dest_ref, tt_ref, *, tmx):
    n_tiles, _, tm = meta_ref.shape
    shift = tmx.bit_length() - 1
    cnt = cnt_ref[...]
    tot = jnp.sum(cnt, axis=0)
    padded = lax.shift_left(lax.shift_right_logical(tot + (tmx - 1), shift), shift)
    starts, ends = [], []
    run = jnp.zeros((1, LANES), jnp.int32)
    for g in range(N_GROUPS):
        starts.append(run)
        run = run + padded[g:g + 1, :]
        ends.append(run)

    def widen(v):
        return jnp.concatenate([v] * (tm // LANES), axis=1)

    def body(i, seen):
        gs = meta_ref[i, 0:1, :]
        dest = meta_ref[i, 1:2, :]
        for g in range(N_GROUPS):
            dest = dest + jnp.where(gs == g, widen(starts[g] + seen[g]), 0)
        dest_ref[i] = dest
        c = cnt_ref[i]
        return tuple(seen[g] + c[g:g + 1, :] for g in range(N_GROUPS))

    lax.fori_loop(0, n_tiles, body, tuple(jnp.zeros((1, LANES), jnp.int32) for _ in range(N_GROUPS)))

    lane = lax.broadcasted_iota(jnp.int32, (1, LANES), 1)
    first_row = lane * tmx
    grp = jnp.zeros((1, LANES), jnp.int32)
    for g in range(N_GROUPS - 1):
        grp = grp + jnp.where(first_row >= ends[g], 1, 0)
    used = lax.shift_right_logical(ends[-1], shift)
    pad_lo = jnp.zeros((1, LANES), jnp.int32)
    pad_hi = jnp.zeros((1, LANES), jnp.int32)
    for g in range(N_GROUPS):
        pad_lo = pad_lo + jnp.where(lane == g, starts[g] + tot[g:g + 1, :], 0)
        pad_hi = pad_hi + jnp.where(lane == g, ends[g], 0)
    n_rows = (n_tiles * tm // tmx + N_GROUPS) * tmx
    pad_lo = pad_lo + jnp.where(lane == N_GROUPS, ends[-1], 0)
    pad_hi = pad_hi + jnp.where(lane == N_GROUPS, n_rows, 0)
    row_id = lax.broadcasted_iota(jnp.int32, (META_ROWS, LANES), 0)
    tt_ref[...] = jnp.where(row_id == TT_GROUP, grp,
                            jnp.where(row_id == TT_USED, used,
                                      jnp.where(row_id == TT_PAD_LO, pad_lo,
                                                jnp.where(row_id == TT_PAD_HI, pad_hi, 0))))


def _assign(cnt, meta, tmx):
    n_tiles, _, tm = meta.shape
    return pl.pallas_call(
        functools.partial(_assign_kernel, tmx=tmx),
        grid=(1,),
        in_specs=[_const_spec(cnt.shape), _const_spec(meta.shape)],
        out_specs=[_const_spec((n_tiles, 1, tm)), _const_spec((META_ROWS, LANES))],
        out_shape=[jax.ShapeDtypeStruct((n_tiles, 1, tm), jnp.int32),
                   jax.ShapeDtypeStruct((META_ROWS, LANES), jnp.int32)],
        compiler_params=_cparams(("arbitrary",), 32),
        name="assign",
    )(cnt, meta)


def _dispatch_kernel(dest_ref, tt_ref, xr_ref, zero_ref, xs_hbm, sem, *, tm):
    base = pl.program_id(0) * tm

    def issue(r, c):
        pltpu.make_async_copy(xr_ref.at[pl.ds(r, 1)], xs_hbm.at[pl.ds(dest_ref[base + r], 1)], sem).start()
        return c

    lax.fori_loop(0, tm, issue, 0, unroll=8)
    pltpu.make_async_copy(xr_ref, xs_hbm.at[pl.ds(0, tm)], sem).wait()

    @pl.when(pl.program_id(0) == 0)
    def _():
        for g in range(N_GROUPS + 1):
            lo = tt_ref[TT_PAD_LO * LANES + g]
            hi = tt_ref[TT_PAD_HI * LANES + g]

            def fill(r, c):
                pltpu.make_async_copy(zero_ref.at[pl.ds(0, 1)], xs_hbm.at[pl.ds(r, 1)], sem).start()
                return c

            lax.fori_loop(lo, hi, fill, 0)

            def drain(r, c):
                pltpu.make_async_copy(zero_ref.at[pl.ds(0, 1)], xs_hbm.at[pl.ds(0, 1)], sem).wait()
                return c

            lax.fori_loop(lo, hi, drain, 0)


def _dispatch(dest, tt, xr, n_rows, tm):
    T, width = xr.shape
    zero = jnp.zeros((META_ROWS, width), xr.dtype)
    return pl.pallas_call(
        functools.partial(_dispatch_kernel, tm=tm),
        grid_spec=pltpu.PrefetchScalarGridSpec(
            num_scalar_prefetch=2, grid=(T // tm,),
            in_specs=[pl.BlockSpec((tm, width), lambda i, d, t: (i, 0)),
                      pl.BlockSpec((META_ROWS, width), lambda i, d, t: (0, 0))],
            out_specs=pl.BlockSpec(memory_space=pl.ANY),
            scratch_shapes=[pltpu.SemaphoreType.DMA(())]),
        out_shape=jax.ShapeDtypeStruct((n_rows, width), xr.dtype),
        compiler_params=_cparams(("arbitrary",), 16),
        name="dispatch",
    )(dest, tt, xr, zero)


def _unsort_kernel(dest_ref, ys_hbm, y_ref, sem, *, tm):
    base = pl.program_id(0) * tm

    def issue(r, c):
        pltpu.make_async_copy(ys_hbm.at[pl.ds(dest_ref[base + r], 1)], y_ref.at[pl.ds(r, 1)], sem).start()
        return c

    lax.fori_loop(0, tm, issue, 0, unroll=8)
    pltpu.make_async_copy(ys_hbm.at[pl.ds(0, tm)], y_ref, sem).wait()


def _unsort(dest, ys, T, tm):
    width = ys.shape[1]
    return pl.pallas_call(
        functools.partial(_unsort_kernel, tm=tm),
        grid_spec=pltpu.PrefetchScalarGridSpec(
            num_scalar_prefetch=1, grid=(T // tm,), in_specs=[pl.BlockSpec(memory_space=pl.ANY)],
            out_specs=pl.BlockSpec((tm, width), lambda i, d: (i, 0)),
            scratch_shapes=[pltpu.SemaphoreType.DMA(())]),
        out_shape=jax.ShapeDtypeStruct((T, width), ys.dtype),
        compiler_params=_cparams(("arbitrary",), 16),
        name="unsort",
    )(dest, ys)


def _experts_kernel(tt_ref, xs_ref, g_ref, w1_ref, w3_ref, w2_ref, o_ref):
    j = pl.program_id(0)

    @pl.when(j < tt_ref[TT_USED * LANES])
    def _():
        grp = tt_ref[TT_GROUP * LANES + j]
        x = xs_ref[:, :D_MODEL]
        cw = xs_ref[:, D_MODEL:]
        h = (x * lax.rsqrt(jnp.mean(x * x, axis=-1, keepdims=True) + EPS) * g_ref[...]).astype(BF16)
        lane = lax.broadcasted_iota(jnp.int32, cw.shape, 1)
        acc = x
        for e in range(EXPERTS_PER_GROUP):
            a = jnp.dot(h, w1_ref[e], preferred_element_type=F32)
            b = jnp.dot(h, w3_ref[e], preferred_element_type=F32)
            col = jnp.sum(jnp.where(lane == grp * EXPERTS_PER_GROUP + e, cw, 0.0), axis=-1, keepdims=True)
            hid = (a / (1.0 + jnp.exp(-a))) * b * col
            acc = acc + jnp.dot(hid.astype(BF16), w2_ref[e], preferred_element_type=F32)
        o_ref[...] = acc

    @pl.when(j >= tt_ref[TT_USED * LANES])
    def _():
        o_ref[...] = jnp.zeros_like(o_ref)


def _experts(tt, xs, g, w1, w3, w2, tmx):
    n_rows = xs.shape[0]
    wspec = lambda a, b: pl.BlockSpec((EXPERTS_PER_GROUP, a, b), lambda j, tt: (tt[TT_GROUP * LANES + j], 0, 0))
    return pl.pallas_call(
        _experts_kernel,
        grid_spec=pltpu.PrefetchScalarGridSpec(
            num_scalar_prefetch=1, grid=(n_rows // tmx,),
            in_specs=[pl.BlockSpec((tmx, ROW_EXT), lambda j, tt: (j, 0)),
                      pl.BlockSpec((1, D_MODEL), lambda j, tt: (0, 0)),
                      wspec(D_MODEL, D_EXPERT), wspec(D_MODEL, D_EXPERT), wspec(D_EXPERT, D_MODEL)],
            out_specs=pl.BlockSpec((tmx, D_MODEL), lambda j, tt: (j, 0))),
        out_shape=jax.ShapeDtypeStruct((n_rows, D_MODEL), F32),
        compiler_params=_cparams(("arbitrary",), 52),
        name="experts",
    )(tt, xs, g, w1, w3, w2)


def _rope_swap(w):
    half = QK_ROPE // 2
    return jnp.concatenate([w, w[..., half:], w[..., :half]], axis=-1)


def _head_gain(g):
    return jnp.concatenate([g[:QK_NOPE], _rope_swap(g[QK_NOPE:])])[None, :].astype(F32)


def _score_shift(q_norm, k_norm):
    gq = _head_gain(q_norm) * (QK_HEAD ** -0.5 * math.log2(math.e))
    gk = _head_gain(k_norm)
    c = (1.0 + 1.0 / 16.0) ** 2 * 1.01 * QK_HEAD * jnp.max(jnp.abs(gq)) * jnp.max(jnp.abs(gk))
    at_shift = (jnp.arange(LANES) == SHIFT_LANE)[None, :]
    return dict(gq=jnp.concatenate([gq, jnp.where(at_shift, -c, 0.0)], axis=0),
                gk=jnp.concatenate([gk, jnp.where(at_shift, 1.0, 0.0)], axis=0),
                safe=jnp.where(c < FP8_SHIFT_LIMIT, MODE_FP8_SCORES,
                               jnp.where(c < SAFE_SHIFT_LIMIT, MODE_BOUND_SHIFT, MODE_ROW_MAX)
                               ).astype(jnp.int32).reshape(1))


def _layer_params(l, attn_norm, w_in, q_a_norm, w_uq, kv_a_norm, w_ukv, q_norm, k_norm, sgu_ln_g, sgu_ln_b,
                  sgu_w, sgu_b, w_proj_attn, w_proj_sgu, w_proj_fnet, w_out, ffn_norm, w_group, b_group,
                  w_router, b_router, w1, w3, w2):
    wi = w_in[l]
    c0 = Q_LORA + KV_LORA
    c1 = c0 + QK_ROPE
    w_ext = jnp.concatenate(
        [wi[:, :c0], jnp.zeros((D_MODEL, QK_NOPE), F32), _rope_swap(wi[:, c0:c1]), wi[:, c1:]], axis=1).astype(BF16)
    wq = w_uq[l].reshape(Q_LORA, N_HEADS, QK_HEAD)
    wq = jnp.concatenate([wq[..., :QK_NOPE], _rope_swap(wq[..., QK_NOPE:])], axis=-1)
    wq = wq.reshape(Q_LORA, N_HEADS * HEAD_SLAB).astype(BF16)
    wkv = w_ukv[l].reshape(KV_LORA, N_HEADS, QK_NOPE + V_HEAD)
    zpad = jnp.zeros((KV_LORA, N_HEADS, HEAD_SLAB - QK_NOPE), F32)
    wk = jnp.concatenate([wkv[..., :QK_NOPE], zpad], axis=-1).reshape(KV_LORA, N_HEADS * HEAD_SLAB).astype(BF16)
    wv = jnp.concatenate([wkv[..., QK_NOPE:], zpad], axis=-1).reshape(KV_LORA, N_HEADS * HEAD_SLAB).astype(BF16)
    w_rg = jnp.concatenate([w_router[l], w_group[l], jnp.zeros((D_MODEL, LANES - N_EXPERTS - N_GROUPS), F32)], axis=1)
    w_rg_hi = w_rg.astype(BF16)
    w_rg_lo = (w_rg - w_rg_hi.astype(F32)).astype(BF16)
    b_rg = jnp.concatenate([b_router[l], b_group[l], jnp.zeros((LANES - N_EXPERTS - N_GROUPS,), F32)])[None, :]
    return dict(
        attn_norm=attn_norm[l][None, :], w_ext=w_ext,
        q_a_norm=q_a_norm[l][None, :], kv_a_norm=kv_a_norm[l][None, :],
        ln_g=sgu_ln_g[l][None, :], ln_b=sgu_ln_b[l][None, :],
        wq=wq, wk=wk, wv=wv, **_score_shift(q_norm[l], k_norm[l]),
        sgu_w=sgu_w[l].astype(BF16), sgu_bias=jnp.repeat(sgu_b[l].T, SGU_GROUP_DIM, axis=1),
        wpa=w_proj_attn[l].astype(BF16), wps=w_proj_sgu[l].astype(BF16), wpf=w_proj_fnet[l].astype(BF16),
        wo=w_out[l].astype(BF16),
        ffn_norm=ffn_norm[l][None, :], w_rg_hi=w_rg_hi, w_rg_lo=w_rg_lo, b_rg=b_rg,
        w1=w1[l].astype(BF16), w3=w3[l].astype(BF16), w2=w2[l].astype(BF16),
    )


def _rope_slabs(positions):
    inv = 1.0 / (ROPE_THETA ** (jnp.arange(0, QK_ROPE, 2, dtype=F32) / QK_ROPE))
    ang = positions.astype(F32).reshape(-1)[:, None] * inv
    cos, sin = jnp.cos(ang), jnp.sin(ang)
    T = ang.shape[0]
    ctab = jnp.concatenate([jnp.ones((T, QK_NOPE), F32), cos, cos, jnp.zeros((T, QK_ROPE), F32)], axis=1)
    stab = jnp.concatenate([jnp.zeros((T, QK_NOPE), F32), -sin, sin, jnp.zeros((T, QK_ROPE), F32)], axis=1)
    return ctab, stab


def _dft_tables(n):
    r = 1 << ((n.bit_length() - 1) // 2)
    kk = jnp.arange(n, dtype=jnp.int32)[:, None]
    ang_hi = ((kk * (jnp.arange(n // r, dtype=jnp.int32) * r)[None, :]) & (n - 1)).astype(F32) * (2.0 * math.pi / n)
    ang_lo = ((kk * jnp.arange(r, dtype=jnp.int32)[None, :]) & (n - 1)).astype(F32) * (2.0 * math.pi / n)
    ch, sh = jnp.cos(ang_hi)[:, :, None], jnp.sin(ang_hi)[:, :, None]
    cl, sl = jnp.cos(ang_lo)[:, None, :], jnp.sin(ang_lo)[:, None, :]
    c = (ch * cl - sh * sl).reshape(n, n).astype(BF16)
    sneg = (-(sh * cl + ch * sl)).reshape(n, n).astype(BF16)
    m = FNET_GROUP_DIM
    jk = np.outer(np.arange(m), np.arange(m)) % m
    angc = jk.astype(np.float64) * (2.0 * math.pi / m)
    cs = jnp.asarray(np.concatenate([np.cos(angc), np.sin(angc)], axis=1), F32).astype(BF16)
    return c, sneg, cs


def _tile(n, want):
    t = min(n, want)
    assert n % t == 0
    return t


def kernel(x, positions, attn_norm, w_in, q_a_norm, w_uq, kv_a_norm, w_ukv, q_norm, k_norm, sgu_ln_g, sgu_ln_b,
           sgu_w, sgu_b, w_proj_attn, w_proj_sgu, w_proj_fnet, w_out, ffn_norm, w_group, b_group, w_router,
           b_router, w1, w3, w2):
    batch, seq, _ = x.shape
    depth = w_in.shape[0]
    assert seq & (seq - 1) == 0 and seq % CHUNK == 0
    T = batch * seq
    tm = _tile(seq, 512)
    tmx = tm
    ridx = jnp.arange(tm, dtype=jnp.int32)
    utri = (ridx[:, None] < ridx[None, :]).astype(F32)
    ctab, stab = _rope_slabs(positions)
    dft_c, dft_sneg, cs = _dft_tables(seq)
    x2 = x.reshape(T, D_MODEL)
    for l in range(depth):
        p = _layer_params(l, attn_norm, w_in, q_a_norm, w_uq, kv_a_norm, w_ukv, q_norm, k_norm, sgu_ln_g,
                          sgu_ln_b, sgu_w, sgu_b, w_proj_attn, w_proj_sgu, w_proj_fnet, w_out, ffn_norm,
                          w_group, b_group, w_router, b_router, w1, w3, w2)
        cqn, ckvn, kr, u, v, fa, fb, gates = _inproj(
            x2, p["attn_norm"], p["w_ext"], p["q_a_norm"], p["kv_a_norm"], p["ln_g"], p["ln_b"], cs,
            batch, seq, _tile(seq, 256))
        q, k, vv = _mla_prep(cqn, ckvn, kr, ctab, stab, p["wq"], p["wk"], p["wv"], p["gq"], p["gk"],
                             batch, seq, tm)
        ya = _attention(p["safe"], q, k, vv, _tile(seq, 1024)).reshape(T, N_HEADS * V_HEAD)
        yf = _seq_dft(dft_c, dft_sneg, fa, fb, _tile(seq, 1024), _tile(batch * FNET_WIDTH, 1024), _tile(seq, 1024))
        xr, meta, cnt = _merge_route(
            x2, ya, u, v, yf, gates, p["sgu_w"], p["sgu_bias"], p["wpa"], p["wps"], p["wpf"], p["wo"],
            p["ffn_norm"], p["w_rg_hi"], p["w_rg_lo"], p["b_rg"], utri, seq, tm)
        dest, tt = _assign(cnt, meta, tmx)
        dest = dest.reshape(T)
        tt = tt.reshape(META_ROWS * LANES)
        xs = _dispatch(dest, tt, xr, T + N_GROUPS * tmx, tm)
        ys = _experts(tt, xs, p["ffn_norm"], p["w1"], p["w3"], p["w2"], tmx)
        x2 = _unsort(dest, ys, T, tm)
    return x2.reshape(batch, seq, D_MODEL)
```

```python
import functools
import math

import numpy as np
import jax
import jax.numpy as jnp
from jax import lax
from jax.experimental import pallas as pl
from jax.experimental.pallas import tpu as pltpu

F32 = jnp.float32
BF16 = jnp.bfloat16
F8 = jnp.float8_e4m3fn

D_MODEL = 1024
N_HEADS = 8
QK_NOPE = 64
QK_ROPE = 32
QK_HEAD = QK_NOPE + QK_ROPE
V_HEAD = 64
Q_LORA = 384
KV_LORA = 256
ROPE_THETA = 10000.0
CHUNK = 128
SGU_GROUPS = 8
SGU_GROUP_DIM = 64
SGU_WIDTH = SGU_GROUPS * SGU_GROUP_DIM
FNET_GROUPS = 4
FNET_GROUP_DIM = 128
FNET_WIDTH = FNET_GROUPS * FNET_GROUP_DIM
N_BRANCH = 3
N_GROUPS = 4
EXPERTS_PER_GROUP = 8
N_EXPERTS = N_GROUPS * EXPERTS_PER_GROUP
D_EXPERT = 256
EPS = 1e-6

LANES = 128
HEAD_SLAB = LANES
SHIFT_LANE = QK_HEAD
O_CQ = 0
O_CKV = O_CQ + Q_LORA
O_KR = O_CKV + KV_LORA
O_SGU = O_KR + HEAD_SLAB
O_FN = O_SGU + 2 * SGU_WIDTH
O_GATE = O_FN + FNET_WIDTH
N_IN_EXT = O_GATE + N_BRANCH * D_MODEL

V7X_VMEM_BYTES = 64 * 1024 * 1024


def _cparams(dims, vmem_mb):
    return pltpu.CompilerParams(dimension_semantics=dims,
                                vmem_limit_bytes=min(vmem_mb * 1024 * 1024, V7X_VMEM_BYTES - (6 << 20)))


def _const_spec(shape):
    nd = len(shape)
    return pl.BlockSpec(shape, lambda *_: (0,) * nd)


def _inproj_kernel(x_ref, g_ref, w_ref, qan_ref, kvan_ref, lng_ref, lnb_ref, cs_ref,
                   cq_ref, ckv_ref, ckvt_ref, krt_ref, u_ref, v_ref, fa_ref, fb_ref, gate_ref):
    x = x_ref[...]
    h = (x * lax.rsqrt(jnp.mean(x * x, axis=-1, keepdims=True) + EPS) * g_ref[...]).astype(BF16)

    def seg(a, n):
        return jnp.dot(h, w_ref[:, a:a + n], preferred_element_type=F32)

    cq = seg(O_CQ, Q_LORA)
    cq_ref[...] = (cq * lax.rsqrt(jnp.mean(cq * cq, axis=-1, keepdims=True) + EPS) * qan_ref[...]).astype(BF16)
    ckv = seg(O_CKV, KV_LORA)
    ckvn = ckv * lax.rsqrt(jnp.mean(ckv * ckv, axis=-1, keepdims=True) + EPS) * kvan_ref[...]
    ckv_ref[...] = ckvn.astype(BF16)
    ckvt_ref[...] = ckvn.T.astype(BF16)
    krt_ref[...] = seg(O_KR, HEAD_SLAB).T.astype(BF16)

    u = jax.nn.gelu(seg(O_SGU, SGU_WIDTH))
    u_ref[...] = u.astype(BF16)
    v = jax.nn.gelu(seg(O_SGU + SGU_WIDTH, SGU_WIDTH))
    mu = jnp.mean(v, axis=-1, keepdims=True)
    vc = v - mu
    var = jnp.mean(vc * vc, axis=-1, keepdims=True)
    v_ref[...] = (vc * lax.rsqrt(var + EPS) * lng_ref[...] + lnb_ref[...]).astype(BF16)

    zf = seg(O_FN, FNET_WIDTH).astype(BF16)
    for g in range(FNET_GROUPS):
        sl = slice(g * FNET_GROUP_DIM, (g + 1) * FNET_GROUP_DIM)
        ab = jnp.dot(zf[:, sl], cs_ref[...], preferred_element_type=F32)
        fa_ref[:, sl] = ab[:, :FNET_GROUP_DIM].astype(BF16)
        fb_ref[:, sl] = ab[:, FNET_GROUP_DIM:].astype(BF16)

    for j in range(N_BRANCH):
        gl = seg(O_GATE + j * D_MODEL, D_MODEL)
        gate_ref[:, j * D_MODEL:(j + 1) * D_MODEL] = (1.0 / (1.0 + jnp.exp(-gl))).astype(BF16)


def _inproj(x2, g, w_ext, qan, kvan, lng, lnb, cs, batch, seq, tm):
    T = x2.shape[0]
    n_s = seq // tm
    row = lambda n: pl.BlockSpec((tm, n), lambda i: (i, 0))
    sb = pl.BlockSpec((tm, FNET_WIDTH), lambda i: (i % n_s, i // n_s))
    col = lambda n: pl.BlockSpec((n, tm), lambda i: (0, i))
    outs = [
        jax.ShapeDtypeStruct((T, Q_LORA), BF16),
        jax.ShapeDtypeStruct((T, KV_LORA), BF16),
        jax.ShapeDtypeStruct((KV_LORA, T), BF16),
        jax.ShapeDtypeStruct((HEAD_SLAB, T), BF16),
        jax.ShapeDtypeStruct((T, SGU_WIDTH), BF16),
        jax.ShapeDtypeStruct((T, SGU_WIDTH), BF16),
        jax.ShapeDtypeStruct((seq, batch * FNET_WIDTH), BF16),
        jax.ShapeDtypeStruct((seq, batch * FNET_WIDTH), BF16),
        jax.ShapeDtypeStruct((T, N_BRANCH * D_MODEL), BF16),
    ]
    return pl.pallas_call(
        _inproj_kernel,
        grid=(T // tm,),
        in_specs=[row(D_MODEL), _const_spec((1, D_MODEL)), _const_spec((D_MODEL, N_IN_EXT)),
                  _const_spec((1, Q_LORA)), _const_spec((1, KV_LORA)),
                  _const_spec((1, SGU_WIDTH)), _const_spec((1, SGU_WIDTH)),
                  _const_spec((FNET_GROUP_DIM, 2 * FNET_GROUP_DIM))],
        out_specs=[row(Q_LORA), row(KV_LORA), col(KV_LORA), col(HEAD_SLAB), row(SGU_WIDTH), row(SGU_WIDTH),
                   sb, sb, row(N_BRANCH * D_MODEL)],
        out_shape=outs,
        compiler_params=_cparams(("arbitrary",), 52),
        name="inproj",
    )(x2, g, w_ext, qan, kvan, lng, lnb, cs)


ROPE_ROWS = slice(QK_NOPE, QK_HEAD)
SWAP_ROWS = slice(QK_HEAD, HEAD_SLAB)


def _mla_prep_kernel(cq_ref, ckv_ref, ckvt_ref, krt_ref, ct_ref, st_ref, ctt_ref, stt_ref,
                     wq_ref, wqs_ref, wkt_ref, wv_ref, gq_ref, gkt_ref, q_ref, k_ref, v_ref):
    cq = cq_ref[...]
    tm = cq.shape[0]
    reps = tm // LANES
    q_all = jnp.dot(cq, wq_ref[...], preferred_element_type=F32)
    q_swp = jnp.dot(cq, wqs_ref[...], preferred_element_type=F32)
    a_q = gq_ref[0:1, :] * ct_ref[...]
    b_q = gq_ref[2:3, :] * st_ref[...]
    for h in range(N_HEADS):
        sl = slice(h * HEAD_SLAB, (h + 1) * HEAD_SLAB)
        y = q_all[:, sl]
        r = lax.rsqrt(jnp.sum(y * y, axis=-1, keepdims=True) * (1.0 / QK_HEAD) + EPS)
        q_ref[0, h] = ((y * a_q + q_swp[:, sl] * b_q) * r + gq_ref[1:2, :]).astype(BF16)

    kt_all = jnp.dot(wkt_ref[...], ckvt_ref[...], preferred_element_type=F32)
    krt = krt_ref[...].astype(F32)
    gains = jnp.concatenate([gkt_ref[...]] * reps, axis=1)
    cos_t = ctt_ref[...]
    sin_t = stt_ref[...]
    row = lax.broadcasted_iota(jnp.int32, (HEAD_SLAB - QK_HEAD, tm), 0)
    tail = jnp.where(row == 0, 1.0, 0.0)
    for h in range(N_HEADS):
        y = kt_all[h * HEAD_SLAB:(h + 1) * HEAD_SLAB, :] + krt
        head = y[0:QK_HEAD, :]
        r = lax.rsqrt(jnp.sum(head * head, axis=0, keepdims=True) * (1.0 / QK_HEAD) + EPS)
        yg = y * gains * r
        rope = yg[ROPE_ROWS, :] * cos_t + yg[SWAP_ROWS, :] * sin_t
        k_ref[0, h] = jnp.concatenate([yg[0:QK_NOPE, :], rope, tail], axis=0).astype(BF16)

    v_all = jnp.dot(ckv_ref[...], wv_ref[...], preferred_element_type=F32)
    is_v = lax.broadcasted_iota(jnp.int32, (tm, HEAD_SLAB), 1) < V_HEAD
    for h in range(N_HEADS):
        v_ref[0, h] = jnp.where(is_v, v_all[:, h * HEAD_SLAB:(h + 1) * HEAD_SLAB], 1.0).astype(BF16)


def _mla_prep(cqn, ckvn, ckvt, krt, rope, p, batch, seq, tm):
    T = cqn.shape[0]
    n_s = seq // tm
    row = lambda n: pl.BlockSpec((tm, n), lambda i: (i, 0))
    col = lambda n: pl.BlockSpec((n, tm), lambda i: (0, i))
    hb = pl.BlockSpec((1, N_HEADS, tm, HEAD_SLAB), lambda i: (i // n_s, 0, i % n_s, 0))
    shp = jax.ShapeDtypeStruct((batch, N_HEADS, seq, HEAD_SLAB), BF16)
    wide = N_HEADS * HEAD_SLAB
    return pl.pallas_call(
        _mla_prep_kernel,
        grid=(T // tm,),
        in_specs=[row(Q_LORA), row(KV_LORA), col(KV_LORA), col(HEAD_SLAB),
                  row(HEAD_SLAB), row(HEAD_SLAB), col(QK_ROPE), col(QK_ROPE),
                  _const_spec((Q_LORA, wide)), _const_spec((Q_LORA, wide)), _const_spec((wide, KV_LORA)),
                  _const_spec((KV_LORA, wide)), _const_spec((3, HEAD_SLAB)), _const_spec((HEAD_SLAB, LANES))],
        out_specs=[hb, pl.BlockSpec((1, N_HEADS, HEAD_SLAB, tm), lambda i: (i // n_s, 0, 0, i % n_s)), hb],
        out_shape=[shp, jax.ShapeDtypeStruct((batch, N_HEADS, HEAD_SLAB, seq), BF16), shp],
        compiler_params=_cparams(("arbitrary",), 40),
        name="mla_prep",
    )(cqn, ckvn, ckvt, krt, rope["ct"], rope["st"], rope["ctt"], rope["stt"],
      p["wq"], p["wq_swap"], p["wkt"], p["wv"], p["gq"], p["gkt"])


HEADS_PER_STEP = LANES // V_HEAD


ROW_CHAINS = 2
SAFE_SHIFT_LIMIT = 60.0
FP8_SHIFT_LIMIT = 32.0
MODE_ROW_MAX, MODE_BOUND_SHIFT, MODE_FP8_SCORES = 0, 1, 2


def _attn_kernel(safe_ref, q_ref, kt_ref, v_ref, o_ref):
    tq = q_ref.shape[2]
    rows_per_chain = tq // ROW_CHAINS
    lane = lax.broadcasted_iota(jnp.int32, (rows_per_chain, LANES), 1)
    low = lane < V_HEAD

    def finish(rows, outs):
        outs = [o / jnp.where(low, pltpu.roll(o, V_HEAD, 1), 1.0) for o in outs]
        o_ref[0, rows, :] = jnp.where(low, outs[0], pltpu.roll(outs[1], V_HEAD, 1)).astype(BF16)

    def bound_shifted(dtype):
        kts = [kt_ref[0, j].astype(dtype) for j in range(HEADS_PER_STEP)]
        for r in range(ROW_CHAINS):
            rows = slice(r * rows_per_chain, (r + 1) * rows_per_chain)
            ps = [jnp.exp2(jnp.dot(q_ref[0, j, rows, :].astype(dtype), kts[j],
                                   preferred_element_type=F32).astype(BF16)) for j in range(HEADS_PER_STEP)]
            finish(rows, [jnp.dot(ps[j], v_ref[0, j], preferred_element_type=F32) for j in range(HEADS_PER_STEP)])

    @pl.when(safe_ref[0] == MODE_FP8_SCORES)
    def _():
        bound_shifted(F8)

    @pl.when(safe_ref[0] == MODE_BOUND_SHIFT)
    def _():
        bound_shifted(BF16)

    @pl.when(safe_ref[0] == MODE_ROW_MAX)
    def _():
        for r in range(ROW_CHAINS):
            rows = slice(r * rows_per_chain, (r + 1) * rows_per_chain)
            ss = [jnp.dot(q_ref[0, j, rows, :], kt_ref[0, j], preferred_element_type=F32)
                  for j in range(HEADS_PER_STEP)]
            ps = [jnp.exp2((s - jnp.max(s, axis=-1, keepdims=True)).astype(BF16)) for s in ss]
            finish(rows, [jnp.dot(ps[j], v_ref[0, j], preferred_element_type=F32) for j in range(HEADS_PER_STEP)])


def _attention(safe, q, k, v, tq):
    batch, _, seq, _ = q.shape
    n_hp = N_HEADS // HEADS_PER_STEP
    return pl.pallas_call(
        _attn_kernel,
        grid_spec=pltpu.PrefetchScalarGridSpec(
            num_scalar_prefetch=1, grid=(batch, n_hp, seq // tq),
            in_specs=[pl.BlockSpec((1, HEADS_PER_STEP, tq, HEAD_SLAB), lambda b, h, i, f: (b, h, i, 0)),
                      pl.BlockSpec((1, HEADS_PER_STEP, HEAD_SLAB, seq), lambda b, h, i, f: (b, h, 0, 0)),
                      pl.BlockSpec((1, HEADS_PER_STEP, seq, HEAD_SLAB), lambda b, h, i, f: (b, h, 0, 0))],
            out_specs=pl.BlockSpec((1, tq, LANES), lambda b, h, i, f: (b, i, h))),
        out_shape=jax.ShapeDtypeStruct((batch, seq, N_HEADS * V_HEAD), BF16),
        compiler_params=_cparams(("arbitrary", "arbitrary", "arbitrary"), 52),
        name="attention",
    )(safe, q, k, v)


def _dft_kernel(c_ref, s_ref, a_ref, b_ref, o_ref, acc_ref, *, scale):
    kk = pl.program_id(2)

    @pl.when(kk == 0)
    def _():
        acc_ref[...] = jnp.zeros_like(acc_ref)

    acc_ref[...] += (jnp.dot(c_ref[...], a_ref[...], preferred_element_type=F32)
                     + jnp.dot(s_ref[...], b_ref[...], preferred_element_type=F32))

    @pl.when(kk == pl.num_programs(2) - 1)
    def _():
        o_ref[...] = (acc_ref[...] * scale).astype(BF16)


def _seq_dft(ctab, sneg, fa, fb, tm, tn, tk):
    seq, width = fa.shape
    scale = 1.0 / math.sqrt(seq * FNET_GROUP_DIM)
    return pl.pallas_call(
        functools.partial(_dft_kernel, scale=scale),
        grid=(seq // tm, width // tn, seq // tk),
        in_specs=[pl.BlockSpec((tm, tk), lambda i, j, k: (i, k)),
                  pl.BlockSpec((tm, tk), lambda i, j, k: (i, k)),
                  pl.BlockSpec((tk, tn), lambda i, j, k: (k, j)),
                  pl.BlockSpec((tk, tn), lambda i, j, k: (k, j))],
        out_specs=pl.BlockSpec((tm, tn), lambda i, j, k: (i, j)),
        out_shape=jax.ShapeDtypeStruct((seq, width), BF16),
        scratch_shapes=[pltpu.VMEM((tm, tn), F32)],
        compiler_params=_cparams(("arbitrary", "arbitrary", "arbitrary"), 40),
        name="seq_dft",
    )(ctab, sneg, fa, fb)


def _merge_kernel(x_ref, ya_ref, u_ref, v_ref, yf_ref, gate_ref, ws_ref, bias_ref,
                  wpa_ref, wps_ref, wpf_ref, wo_ref, gn_ref, whi_ref, wlo_ref, brg_ref, utri_ref,
                  xr_ref, meta_ref, cnt_ref, ysgu_ref):
    tm = x_ref.shape[0]
    lane = lax.broadcasted_iota(jnp.int32, (CHUNK, LANES), 1)
    low = lane < SGU_GROUP_DIM
    groups_per_slab = LANES // SGU_GROUP_DIM
    for c in range(tm // CHUNK):
        rows = slice(c * CHUNK, (c + 1) * CHUNK)
        for j in range(SGU_WIDTH // LANES):
            cols = slice(j * LANES, (j + 1) * LANES)
            vb = v_ref[rows, cols]
            ma = jnp.dot(ws_ref[groups_per_slab * j], vb, preferred_element_type=F32)
            mb = jnp.dot(ws_ref[groups_per_slab * j + 1], vb, preferred_element_type=F32)
            mixed = jnp.where(low, ma, mb) + bias_ref[:, cols]
            ysgu_ref[rows, cols] = (u_ref[rows, cols].astype(F32) * mixed).astype(BF16)
    pa = jnp.dot(ya_ref[...], wpa_ref[...], preferred_element_type=F32)
    merged = gate_ref[:, 0:D_MODEL].astype(F32) * pa
    ps = jnp.dot(ysgu_ref[...], wps_ref[...], preferred_element_type=F32)
    merged += gate_ref[:, D_MODEL:2 * D_MODEL].astype(F32) * ps
    pf = jnp.dot(yf_ref[...], wpf_ref[...], preferred_element_type=F32)
    merged += gate_ref[:, 2 * D_MODEL:3 * D_MODEL].astype(F32) * pf
    x_mid = x_ref[...] + jnp.dot(merged.astype(BF16), wo_ref[...], preferred_element_type=F32)
    _route_tile(x_mid, gn_ref, whi_ref, wlo_ref, brg_ref, utri_ref, xr_ref, meta_ref, cnt_ref)


def _merge_route(x2, ya, u, v, yf, gates, ws, bias, wpa, wps, wpf, wo, gn, whi, wlo, brg, utri, seq, tm):
    T = x2.shape[0]
    n_s = seq // tm
    n_tiles = T // tm
    row = lambda n: pl.BlockSpec((tm, n), lambda i: (i, 0))
    return pl.pallas_call(
        _merge_kernel,
        grid=(n_tiles,),
        in_specs=[row(D_MODEL), row(N_HEADS * V_HEAD), row(SGU_WIDTH), row(SGU_WIDTH),
                  pl.BlockSpec((tm, FNET_WIDTH), lambda i: (i % n_s, i // n_s)),
                  row(N_BRANCH * D_MODEL),
                  _const_spec((SGU_GROUPS, CHUNK, CHUNK)), _const_spec((CHUNK, SGU_WIDTH)),
                  _const_spec((N_HEADS * V_HEAD, D_MODEL)), _const_spec((SGU_WIDTH, D_MODEL)),
                  _const_spec((FNET_WIDTH, D_MODEL)), _const_spec((D_MODEL, D_MODEL)),
                  _const_spec((1, D_MODEL)), _const_spec((D_MODEL, LANES)), _const_spec((D_MODEL, LANES)),
                  _const_spec((1, LANES)), _const_spec((tm, tm))],
        out_specs=[row(ROW_EXT), pl.BlockSpec((1, META_ROWS, tm), lambda i: (i, 0, 0)),
                   pl.BlockSpec((1, META_ROWS, LANES), lambda i: (i, 0, 0))],
        out_shape=[jax.ShapeDtypeStruct((T, ROW_EXT), F32),
                   jax.ShapeDtypeStruct((n_tiles, META_ROWS, tm), jnp.int32),
                   jax.ShapeDtypeStruct((n_tiles, META_ROWS, LANES), jnp.int32)],
        scratch_shapes=[pltpu.VMEM((tm, SGU_WIDTH), BF16)],
        compiler_params=_cparams(("arbitrary",), 52),
        name="merge_route",
    )(x2, ya, u, v, yf, gates, ws, bias, wpa, wps, wpf, wo, gn, whi, wlo, brg, utri)


O_GLOGIT = N_EXPERTS


META_ROWS = 8
ROW_EXT = D_MODEL + LANES


def _route_tile(x, g_ref, whi_ref, wlo_ref, b_ref, utri_ref, xr_ref, meta_ref, cnt_ref):
    tm = x.shape[0]
    h = x * lax.rsqrt(jnp.mean(x * x, axis=-1, keepdims=True) + EPS) * g_ref[...]
    hhi = h.astype(BF16)
    hlo = (h - hhi.astype(F32)).astype(BF16)
    logits = (jnp.dot(hhi, whi_ref[...], preferred_element_type=F32)
              + jnp.dot(hlo, whi_ref[...], preferred_element_type=F32)
              + jnp.dot(hhi, wlo_ref[...], preferred_element_type=F32)) + b_ref[...]
    lane = lax.broadcasted_iota(jnp.int32, logits.shape, 1)
    lane_f = lane.astype(F32)
    neg = -jnp.inf
    far = float(2 * LANES)

    is_g = (lane >= O_GLOGIT) & (lane < O_GLOGIT + N_GROUPS)
    gl = jnp.where(is_g, logits, neg)
    gmax = jnp.max(gl, axis=-1, keepdims=True)
    gsel = jnp.min(jnp.where(gl == gmax, lane_f, far), axis=-1, keepdims=True) - float(O_GLOGIT)
    gw = 1.0 / jnp.sum(jnp.where(is_g, jnp.exp(gl - gmax), 0.0), axis=-1, keepdims=True)

    lane_grp = lax.shift_right_logical(lane, 3).astype(F32)
    in_grp = (lane < N_EXPERTS) & (lane_grp == gsel)
    el = jnp.where(in_grp, logits, neg)
    e1 = jnp.max(el, axis=-1, keepdims=True)
    i1 = jnp.min(jnp.where(el == e1, lane_f, far), axis=-1, keepdims=True)
    el2 = jnp.where(lane_f == i1, neg, el)
    e2 = jnp.max(el2, axis=-1, keepdims=True)
    i2 = jnp.min(jnp.where(el2 == e2, lane_f, far), axis=-1, keepdims=True)
    d = jnp.exp(e2 - e1)
    w1 = gw / (1.0 + d)
    w2 = w1 * d
    xr_ref[:, :D_MODEL] = x
    xr_ref[:, D_MODEL:] = jnp.where(lane_f == i1, w1, jnp.where(lane_f == i2, w2, 0.0))

    onehot_t = jnp.where((lane_f == gsel) & (lane < N_GROUPS), 1.0, 0.0).T[0:META_ROWS, :]
    before = jnp.dot(onehot_t, utri_ref[...], preferred_element_type=F32)
    gid = lax.broadcasted_iota(jnp.int32, (META_ROWS, tm), 0).astype(F32)
    gsel_t = jnp.sum(onehot_t * gid, axis=0, keepdims=True)
    rank_t = jnp.sum(onehot_t * before, axis=0, keepdims=True)
    row_id = lax.broadcasted_iota(jnp.int32, (META_ROWS, tm), 0)
    meta_ref[0] = jnp.where(row_id == 0, gsel_t, jnp.where(row_id == 1, rank_t, 0.0)).astype(jnp.int32)
    cnt = jnp.sum(onehot_t, axis=1, keepdims=True)
    cnt_ref[0] = jnp.broadcast_to(cnt, (META_ROWS, LANES)).astype(jnp.int32)


TT_GROUP, TT_USED, TT_PAD_LO, TT_PAD_HI = 0, 1, 2, 3


def _assign_kernel(cnt_ref, meta_ref, dest_ref, tt_ref, *, tmx):
    n_tiles, _, tm = meta_ref.shape
    shift = tmx.bit_length() - 1
    cnt = cnt_ref[...]
    tot = jnp.sum(cnt, axis=0)
    padded = lax.shift_left(lax.shift_right_logical(tot + (tmx - 1), shift), shift)
    starts, ends = [], []
    run = jnp.zeros((1, LANES), jnp.int32)
    for g in range(N_GROUPS):
        starts.append(run)
        run = run + padded[g:g + 1, :]
        ends.append(run)

    def widen(v):
        return jnp.concatenate([v] * (tm // LANES), axis=1)

    def body(i, seen):
        gs = meta_ref[i, 0:1, :]
        dest = meta_ref[i, 1:2, :]
        for g in range(N_GROUPS):
            dest = dest + jnp.where(gs == g, widen(starts[g] + seen[g]), 0)
        dest_ref[i] = dest
        c = cnt_ref[i]
        return tuple(seen[g] + c[g:g + 1, :] for g in range(N_GROUPS))

    lax.fori_loop(0, n_tiles, body, tuple(jnp.zeros((1, LANES), jnp.int32) for _ in range(N_GROUPS)))

    lane = lax.broadcasted_iota(jnp.int32, (1, LANES), 1)
    first_row = lane * tmx
    grp = jnp.zeros((1, LANES), jnp.int32)
    for g in range(N_GROUPS - 1):
        grp = grp + jnp.where(first_row >= ends[g], 1, 0)
    used = lax.shift_right_logical(ends[-1], shift)
    pad_lo = jnp.zeros((1, LANES), jnp.int32)
    pad_hi = jnp.zeros((1, LANES), jnp.int32)
    for g in range(N_GROUPS):
        pad_lo = pad_lo + jnp.where(lane == g, starts[g] + tot[g:g + 1, :], 0)
        pad_hi = pad_hi + jnp.where(lane == g, ends[g], 0)
    n_rows = (n_tiles * tm // tmx + N_GROUPS) * tmx
    pad_lo = pad_lo + jnp.where(lane == N_GROUPS, ends[-1], 0)
    pad_hi = pad_hi + jnp.where(lane == N_GROUPS, n_rows, 0)
    row_id = lax.broadcasted_iota(jnp.int32, (META_ROWS, LANES), 0)
    tt_ref[...] = jnp.where(row_id == TT_GROUP, grp,
                            jnp.where(row_id == TT_USED, used,
                                      jnp.where(row_id == TT_PAD_LO, pad_lo,
                                                jnp.where(row_id == TT_PAD_HI, pad_hi, 0))))


def _assign(cnt, meta, tmx):
    n_tiles, _, tm = meta.shape
    return pl.pallas_call(
        functools.partial(_assign_kernel, tmx=tmx),
        grid=(1,),
        in_specs=[_const_spec(cnt.shape), _const_spec(meta.shape)],
        out_specs=[_const_spec((n_tiles, 1, tm)), _const_spec((META_ROWS, LANES))],
        out_shape=[jax.ShapeDtypeStruct((n_tiles, 1, tm), jnp.int32),
                   jax.ShapeDtypeStruct((META_ROWS, LANES), jnp.int32)],
        compiler_params=_cparams(("arbitrary",), 32),
        name="assign",
    )(cnt, meta)


def _dispatch_kernel(dest_ref, tt_ref, xr_ref, zero_ref, xs_hbm, sem, *, tm):
    base = pl.program_id(0) * tm

    def issue(r, c):
        pltpu.make_async_copy(xr_ref.at[pl.ds(r, 1)], xs_hbm.at[pl.ds(dest_ref[base + r], 1)], sem).start()
        return c

    lax.fori_loop(0, tm, issue, 0, unroll=8)
    pltpu.make_async_copy(xr_ref, xs_hbm.at[pl.ds(0, tm)], sem).wait()

    @pl.when(pl.program_id(0) == 0)
    def _():
        for g in range(N_GROUPS + 1):
            lo = tt_ref[TT_PAD_LO * LANES + g]
            hi = tt_ref[TT_PAD_HI * LANES + g]

            def fill(r, c):
                pltpu.make_async_copy(zero_ref.at[pl.ds(0, 1)], xs_hbm.at[pl.ds(r, 1)], sem).start()
                return c

            lax.fori_loop(lo, hi, fill, 0)

            def drain(r, c):
                pltpu.make_async_copy(zero_ref.at[pl.ds(0, 1)], xs_hbm.at[pl.ds(0, 1)], sem).wait()
                return c

            lax.fori_loop(lo, hi, drain, 0)


def _dispatch(dest, tt, xr, n_rows, tm):
    T, width = xr.shape
    zero = jnp.zeros((META_ROWS, width), xr.dtype)
    return pl.pallas_call(
        functools.partial(_dispatch_kernel, tm=tm),
        grid_spec=pltpu.PrefetchScalarGridSpec(
            num_scalar_prefetch=2, grid=(T // tm,),
            in_specs=[pl.BlockSpec((tm, width), lambda i, d, t: (i, 0)),
                      pl.BlockSpec((META_ROWS, width), lambda i, d, t: (0, 0))],
            out_specs=pl.BlockSpec(memory_space=pl.ANY),
            scratch_shapes=[pltpu.SemaphoreType.DMA(())]),
        out_shape=jax.ShapeDtypeStruct((n_rows, width), xr.dtype),
        compiler_params=_cparams(("arbitrary",), 16),
        name="dispatch",
    )(dest, tt, xr, zero)


def _unsort_kernel(dest_ref, ys_hbm, y_ref, sem, *, tm):
    base = pl.program_id(0) * tm

    def issue(r, c):
        pltpu.make_async_copy(ys_hbm.at[pl.ds(dest_ref[base + r], 1)], y_ref.at[pl.ds(r, 1)], sem).start()
        return c

    lax.fori_loop(0, tm, issue, 0, unroll=8)
    pltpu.make_async_copy(ys_hbm.at[pl.ds(0, tm)], y_ref, sem).wait()


def _unsort(dest, ys, T, tm):
    width = ys.shape[1]
    return pl.pallas_call(
        functools.partial(_unsort_kernel, tm=tm),
        grid_spec=pltpu.PrefetchScalarGridSpec(
            num_scalar_prefetch=1, grid=(T // tm,), in_specs=[pl.BlockSpec(memory_space=pl.ANY)],
            out_specs=pl.BlockSpec((tm, width), lambda i, d: (i, 0)),
            scratch_shapes=[pltpu.SemaphoreType.DMA(())]),
        out_shape=jax.ShapeDtypeStruct((T, width), ys.dtype),
        compiler_params=_cparams(("arbitrary",), 16),
        name="unsort",
    )(dest, ys)


def _experts_kernel(tt_ref, xs_ref, g_ref, w1_ref, w3_ref, w2_ref, o_ref):
    j = pl.program_id(0)

    @pl.when(j < tt_ref[TT_USED * LANES])
    def _():
        grp = tt_ref[TT_GROUP * LANES + j]
        x = xs_ref[:, :D_MODEL]
        cw = xs_ref[:, D_MODEL:]
        h = (x * lax.rsqrt(jnp.mean(x * x, axis=-1, keepdims=True) + EPS) * g_ref[...]).astype(BF16)
        lane = lax.broadcasted_iota(jnp.int32, cw.shape, 1)
        acc = x
        for e in range(EXPERTS_PER_GROUP):
            a = jnp.dot(h, w1_ref[e], preferred_element_type=F32)
            b = jnp.dot(h, w3_ref[e], preferred_element_type=F32)
            col = jnp.sum(jnp.where(lane == grp * EXPERTS_PER_GROUP + e, cw, 0.0), axis=-1, keepdims=True)
            hid = (a / (1.0 + jnp.exp(-a))) * b * col
            acc = acc + jnp.dot(hid.astype(BF16), w2_ref[e], preferred_element_type=F32)
        o_ref[...] = acc

    @pl.when(j >= tt_ref[TT_USED * LANES])
    def _():
        o_ref[...] = jnp.zeros_like(o_ref)


def _experts(tt, xs, g, w1, w3, w2, tmx):
    n_rows = xs.shape[0]
    wspec = lambda a, b: pl.BlockSpec((EXPERTS_PER_GROUP, a, b), lambda j, tt: (tt[TT_GROUP * LANES + j], 0, 0))
    return pl.pallas_call(
        _experts_kernel,
        grid_spec=pltpu.PrefetchScalarGridSpec(
            num_scalar_prefetch=1, grid=(n_rows // tmx,),
            in_specs=[pl.BlockSpec((tmx, ROW_EXT), lambda j, tt: (j, 0)),
                      pl.BlockSpec((1, D_MODEL), lambda j, tt: (0, 0)),
                      wspec(D_MODEL, D_EXPERT), wspec(D_MODEL, D_EXPERT), wspec(D_EXPERT, D_MODEL)],
            out_specs=pl.BlockSpec((tmx, D_MODEL), lambda j, tt: (j, 0))),
        out_shape=jax.ShapeDtypeStruct((n_rows, D_MODEL), F32),
        compiler_params=_cparams(("arbitrary",), 52),
        name="experts",
    )(tt, xs, g, w1, w3, w2)


def _half_swap(w):
    half = QK_ROPE // 2
    return jnp.concatenate([w[..., half:], w[..., :half]], axis=-1)


def _rope_swap(w):
    return jnp.concatenate([w, _half_swap(w)], axis=-1)


def _score_shift(q_norm, k_norm):
    gq = q_norm.astype(F32) * (QK_HEAD ** -0.5 * math.log2(math.e))
    gk = k_norm.astype(F32)
    c = (1.0 + 1.0 / 16.0) ** 2 * 1.01 * QK_HEAD * jnp.max(jnp.abs(gq)) * jnp.max(jnp.abs(gk))
    zeros = lambda n: jnp.zeros((n,), F32)
    tail = HEAD_SLAB - QK_HEAD
    gq_rows = jnp.stack([
        jnp.concatenate([gq, zeros(tail)]),
        jnp.where(jnp.arange(LANES) == SHIFT_LANE, -c, 0.0),
        jnp.concatenate([zeros(QK_NOPE), _half_swap(gq[QK_NOPE:]), zeros(tail)])])
    gkt = jnp.broadcast_to(jnp.concatenate([gk, _half_swap(gk[QK_NOPE:])])[:, None], (HEAD_SLAB, LANES))
    return dict(gq=gq_rows, gkt=gkt,
                safe=jnp.where(c < FP8_SHIFT_LIMIT, MODE_FP8_SCORES,
                               jnp.where(c < SAFE_SHIFT_LIMIT, MODE_BOUND_SHIFT, MODE_ROW_MAX)
                               ).astype(jnp.int32).reshape(1))


def _layer_params(l, attn_norm, w_in, q_a_norm, w_uq, kv_a_norm, w_ukv, q_norm, k_norm, sgu_ln_g, sgu_ln_b,
                  sgu_w, sgu_b, w_proj_attn, w_proj_sgu, w_proj_fnet, w_out, ffn_norm, w_group, b_group,
                  w_router, b_router, w1, w3, w2):
    wi = w_in[l]
    c0 = Q_LORA + KV_LORA
    c1 = c0 + QK_ROPE
    w_ext = jnp.concatenate(
        [wi[:, :c0], jnp.zeros((D_MODEL, QK_NOPE), F32), _rope_swap(wi[:, c0:c1]), wi[:, c1:]], axis=1).astype(BF16)
    wq3 = w_uq[l].reshape(Q_LORA, N_HEADS, QK_HEAD)
    qz = lambda n: jnp.zeros((Q_LORA, N_HEADS, n), F32)
    wq = jnp.concatenate([wq3, qz(HEAD_SLAB - QK_HEAD)], axis=-1)
    wq_swap = jnp.concatenate([qz(QK_NOPE), _half_swap(wq3[..., QK_NOPE:]), qz(HEAD_SLAB - QK_HEAD)], axis=-1)
    wq = wq.reshape(Q_LORA, N_HEADS * HEAD_SLAB).astype(BF16)
    wq_swap = wq_swap.reshape(Q_LORA, N_HEADS * HEAD_SLAB).astype(BF16)
    wkv = w_ukv[l].reshape(KV_LORA, N_HEADS, QK_NOPE + V_HEAD)
    zpad = jnp.zeros((KV_LORA, N_HEADS, HEAD_SLAB - QK_NOPE), F32)
    wkt = jnp.concatenate([wkv[..., :QK_NOPE], zpad], axis=-1).reshape(KV_LORA, N_HEADS * HEAD_SLAB).T.astype(BF16)
    wv = jnp.concatenate([wkv[..., QK_NOPE:], zpad], axis=-1).reshape(KV_LORA, N_HEADS * HEAD_SLAB).astype(BF16)
    w_rg = jnp.concatenate([w_router[l], w_group[l], jnp.zeros((D_MODEL, LANES - N_EXPERTS - N_GROUPS), F32)], axis=1)
    w_rg_hi = w_rg.astype(BF16)
    w_rg_lo = (w_rg - w_rg_hi.astype(F32)).astype(BF16)
    b_rg = jnp.concatenate([b_router[l], b_group[l], jnp.zeros((LANES - N_EXPERTS - N_GROUPS,), F32)])[None, :]
    return dict(
        attn_norm=attn_norm[l][None, :], w_ext=w_ext,
        q_a_norm=q_a_norm[l][None, :], kv_a_norm=kv_a_norm[l][None, :],
        ln_g=sgu_ln_g[l][None, :], ln_b=sgu_ln_b[l][None, :],
        wq=wq, wq_swap=wq_swap, wkt=wkt, wv=wv, **_score_shift(q_norm[l], k_norm[l]),
        sgu_w=sgu_w[l].astype(BF16), sgu_bias=jnp.repeat(sgu_b[l].T, SGU_GROUP_DIM, axis=1),
        wpa=w_proj_attn[l].astype(BF16), wps=w_proj_sgu[l].astype(BF16), wpf=w_proj_fnet[l].astype(BF16),
        wo=w_out[l].astype(BF16),
        ffn_norm=ffn_norm[l][None, :], w_rg_hi=w_rg_hi, w_rg_lo=w_rg_lo, b_rg=b_rg,
        w1=w1[l].astype(BF16), w3=w3[l].astype(BF16), w2=w2[l].astype(BF16),
    )


def _rope_slabs(positions):
    inv = 1.0 / (ROPE_THETA ** (jnp.arange(0, QK_ROPE, 2, dtype=F32) / QK_ROPE))
    ang = positions.astype(F32).reshape(-1)[:, None] * inv
    cos, sin = jnp.cos(ang), jnp.sin(ang)
    T = ang.shape[0]
    c2 = jnp.concatenate([cos, cos], axis=1)
    s2 = jnp.concatenate([-sin, sin], axis=1)
    tail = jnp.zeros((T, HEAD_SLAB - QK_HEAD), F32)
    return dict(ct=jnp.concatenate([jnp.ones((T, QK_NOPE), F32), c2, tail], axis=1),
                st=jnp.concatenate([jnp.zeros((T, QK_NOPE), F32), s2, tail], axis=1),
                ctt=c2.T, stt=s2.T)


def _dft_tables(n):
    r = 1 << ((n.bit_length() - 1) // 2)
    kk = jnp.arange(n, dtype=jnp.int32)[:, None]
    ang_hi = ((kk * (jnp.arange(n // r, dtype=jnp.int32) * r)[None, :]) & (n - 1)).astype(F32) * (2.0 * math.pi / n)
    ang_lo = ((kk * jnp.arange(r, dtype=jnp.int32)[None, :]) & (n - 1)).astype(F32) * (2.0 * math.pi / n)
    ch, sh = jnp.cos(ang_hi)[:, :, None], jnp.sin(ang_hi)[:, :, None]
    cl, sl = jnp.cos(ang_lo)[:, None, :], jnp.sin(ang_lo)[:, None, :]
    c = (ch * cl - sh * sl).reshape(n, n).astype(BF16)
    sneg = (-(sh * cl + ch * sl)).reshape(n, n).astype(BF16)
    m = FNET_GROUP_DIM
    jk = np.outer(np.arange(m), np.arange(m)) % m
    angc = jk.astype(np.float64) * (2.0 * math.pi / m)
    cs = jnp.asarray(np.concatenate([np.cos(angc), np.sin(angc)], axis=1), F32).astype(BF16)
    return c, sneg, cs


def _tile(n, want):
    t = min(n, want)
    assert n % t == 0
    return t


def kernel(x, positions, attn_norm, w_in, q_a_norm, w_uq, kv_a_norm, w_ukv, q_norm, k_norm, sgu_ln_g, sgu_ln_b,
           sgu_w, sgu_b, w_proj_attn, w_proj_sgu, w_proj_fnet, w_out, ffn_norm, w_group, b_group, w_router,
           b_router, w1, w3, w2):
    batch, seq, _ = x.shape
    depth = w_in.shape[0]
    assert seq & (seq - 1) == 0 and seq % CHUNK == 0
    T = batch * seq
    tm = _tile(seq, 512)
    tmx = tm
    ridx = jnp.arange(tm, dtype=jnp.int32)
    utri = (ridx[:, None] < ridx[None, :]).astype(F32)
    rope = _rope_slabs(positions)
    dft_c, dft_sneg, cs = _dft_tables(seq)
    x2 = x.reshape(T, D_MODEL)
    for l in range(depth):
        p = _layer_params(l, attn_norm, w_in, q_a_norm, w_uq, kv_a_norm, w_ukv, q_norm, k_norm, sgu_ln_g,
                          sgu_ln_b, sgu_w, sgu_b, w_proj_attn, w_proj_sgu, w_proj_fnet, w_out, ffn_norm,
                          w_group, b_group, w_router, b_router, w1, w3, w2)
        cqn, ckvn, ckvt, krt, u, v, fa, fb, gates = _inproj(
            x2, p["attn_norm"], p["w_ext"], p["q_a_norm"], p["kv_a_norm"], p["ln_g"], p["ln_b"], cs,
            batch, seq, _tile(seq, 256))
        q, k, vv = _mla_prep(cqn, ckvn, ckvt, krt, rope, p, batch, seq, tm)
        ya = _attention(p["safe"], q, k, vv, _tile(seq, 1024)).reshape(T, N_HEADS * V_HEAD)
        yf = _seq_dft(dft_c, dft_sneg, fa, fb, _tile(seq, 1024), _tile(batch * FNET_WIDTH, 1024), _tile(seq, 1024))
        xr, meta, cnt = _merge_route(
            x2, ya, u, v, yf, gates, p["sgu_w"], p["sgu_bias"], p["wpa"], p["wps"], p["wpf"], p["wo"],
            p["ffn_norm"], p["w_rg_hi"], p["w_rg_lo"], p["b_rg"], utri, seq, tm)
        dest, tt = _assign(cnt, meta, tmx)
        dest = dest.reshape(T)
        tt = tt.reshape(META_ROWS * LANES)
        xs = _dispatch(dest, tt, xr, T + N_GROUPS * tmx, tm)
        ys = _experts(tt, xs, p["ffn_norm"], p["w1"], p["w3"], p["w2"], tmx)
        x2 = _unsort(dest, ys, T, tm)
    return x2.reshape(batch, seq, D_MODEL)
```

```python
import functools
import math

import numpy as np
import jax
import jax.numpy as jnp
from jax import lax
from jax.experimental import pallas as pl
from jax.experimental.pallas import tpu as pltpu

F32 = jnp.float32
BF16 = jnp.bfloat16
F8 = jnp.float8_e4m3fn

D_MODEL = 1024
N_HEADS = 8
QK_NOPE = 64
QK_ROPE = 32
QK_HEAD = QK_NOPE + QK_ROPE
V_HEAD = 64
Q_LORA = 384
KV_LORA = 256
ROPE_THETA = 10000.0
CHUNK = 128
SGU_GROUPS = 8
SGU_GROUP_DIM = 64
SGU_WIDTH = SGU_GROUPS * SGU_GROUP_DIM
FNET_GROUPS = 4
FNET_GROUP_DIM = 128
FNET_WIDTH = FNET_GROUPS * FNET_GROUP_DIM
N_BRANCH = 3
N_GROUPS = 4
EXPERTS_PER_GROUP = 8
N_EXPERTS = N_GROUPS * EXPERTS_PER_GROUP
D_EXPERT = 256
EPS = 1e-6

LANES = 128
HEAD_SLAB = LANES
SHIFT_LANE = QK_HEAD
O_CQ = 0
O_CKV = O_CQ + Q_LORA
O_KR = O_CKV + KV_LORA
O_SGU = O_KR + HEAD_SLAB
O_FN = O_SGU + 2 * SGU_WIDTH
O_GATE = O_FN + FNET_WIDTH
N_IN_EXT = O_GATE + N_BRANCH * D_MODEL

V7X_VMEM_BYTES = 64 * 1024 * 1024


def _cparams(dims, vmem_mb):
    return pltpu.CompilerParams(dimension_semantics=dims,
                                vmem_limit_bytes=min(vmem_mb * 1024 * 1024, V7X_VMEM_BYTES - (6 << 20)))


def _const_spec(shape):
    nd = len(shape)
    return pl.BlockSpec(shape, lambda *_: (0,) * nd)


def _inproj_kernel(x_ref, g_ref, w_ref, qan_ref, kvan_ref, lng_ref, lnb_ref, cs_ref,
                   cq_ref, ckv_ref, ckvt_ref, krt_ref, u_ref, v_ref, fa_ref, fb_ref, gate_ref):
    x = x_ref[...]
    h = (x * lax.rsqrt(jnp.mean(x * x, axis=-1, keepdims=True) + EPS) * g_ref[...]).astype(BF16)

    def seg(a, n):
        return jnp.dot(h, w_ref[:, a:a + n], preferred_element_type=F32)

    cq = seg(O_CQ, Q_LORA)
    cq_ref[...] = (cq * lax.rsqrt(jnp.mean(cq * cq, axis=-1, keepdims=True) + EPS) * qan_ref[...]).astype(BF16)
    ckv = seg(O_CKV, KV_LORA)
    ckvn = ckv * lax.rsqrt(jnp.mean(ckv * ckv, axis=-1, keepdims=True) + EPS) * kvan_ref[...]
    ckv_ref[...] = ckvn.astype(BF16)
    ckvt_ref[...] = ckvn.T.astype(BF16)
    krt_ref[...] = seg(O_KR, HEAD_SLAB).T.astype(BF16)

    u = jax.nn.gelu(seg(O_SGU, SGU_WIDTH))
    u_ref[...] = u.astype(BF16)
    v = jax.nn.gelu(seg(O_SGU + SGU_WIDTH, SGU_WIDTH))
    mu = jnp.mean(v, axis=-1, keepdims=True)
    vc = v - mu
    var = jnp.mean(vc * vc, axis=-1, keepdims=True)
    v_ref[...] = (vc * lax.rsqrt(var + EPS) * lng_ref[...] + lnb_ref[...]).astype(BF16)

    zf = seg(O_FN, FNET_WIDTH).astype(BF16)
    for g in range(FNET_GROUPS):
        sl = slice(g * FNET_GROUP_DIM, (g + 1) * FNET_GROUP_DIM)
        ab = jnp.dot(zf[:, sl], cs_ref[...], preferred_element_type=F32)
        fa_ref[:, sl] = ab[:, :FNET_GROUP_DIM].astype(BF16)
        fb_ref[:, sl] = ab[:, FNET_GROUP_DIM:].astype(BF16)

    for j in range(N_BRANCH):
        gl = seg(O_GATE + j * D_MODEL, D_MODEL)
        gate_ref[:, j * D_MODEL:(j + 1) * D_MODEL] = (1.0 / (1.0 + jnp.exp(-gl))).astype(BF16)


def _inproj(x2, g, w_ext, qan, kvan, lng, lnb, cs, batch, seq, tm):
    T = x2.shape[0]
    n_s = seq // tm
    row = lambda n: pl.BlockSpec((tm, n), lambda i: (i, 0))
    sb = pl.BlockSpec((tm, FNET_WIDTH), lambda i: (i % n_s, i // n_s))
    col = lambda n: pl.BlockSpec((n, tm), lambda i: (0, i))
    outs = [
        jax.ShapeDtypeStruct((T, Q_LORA), BF16),
        jax.ShapeDtypeStruct((T, KV_LORA), BF16),
        jax.ShapeDtypeStruct((KV_LORA, T), BF16),
        jax.ShapeDtypeStruct((HEAD_SLAB, T), BF16),
        jax.ShapeDtypeStruct((T, SGU_WIDTH), BF16),
        jax.ShapeDtypeStruct((T, SGU_WIDTH), BF16),
        jax.ShapeDtypeStruct((seq, batch * FNET_WIDTH), BF16),
        jax.ShapeDtypeStruct((seq, batch * FNET_WIDTH), BF16),
        jax.ShapeDtypeStruct((T, N_BRANCH * D_MODEL), BF16),
    ]
    return pl.pallas_call(
        _inproj_kernel,
        grid=(T // tm,),
        in_specs=[row(D_MODEL), _const_spec((1, D_MODEL)), _const_spec((D_MODEL, N_IN_EXT)),
                  _const_spec((1, Q_LORA)), _const_spec((1, KV_LORA)),
                  _const_spec((1, SGU_WIDTH)), _const_spec((1, SGU_WIDTH)),
                  _const_spec((FNET_GROUP_DIM, 2 * FNET_GROUP_DIM))],
        out_specs=[row(Q_LORA), row(KV_LORA), col(KV_LORA), col(HEAD_SLAB), row(SGU_WIDTH), row(SGU_WIDTH),
                   sb, sb, row(N_BRANCH * D_MODEL)],
        out_shape=outs,
        compiler_params=_cparams(("arbitrary",), 52),
        name="inproj",
    )(x2, g, w_ext, qan, kvan, lng, lnb, cs)


ROPE_ROWS = slice(QK_NOPE, QK_HEAD)
SWAP_ROWS = slice(QK_HEAD, HEAD_SLAB)


def _mla_prep_kernel(cq_ref, ckv_ref, ckvt_ref, krt_ref, ct_ref, st_ref, ctt_ref, stt_ref,
                     wq_ref, wqs_ref, wkt_ref, wv_ref, gq_ref, gkt_ref, q_ref, k_ref, v_ref):
    cq = cq_ref[...]
    tm = cq.shape[0]
    reps = tm // LANES
    q_all = jnp.dot(cq, wq_ref[...], preferred_element_type=F32)
    q_swp = jnp.dot(cq, wqs_ref[...], preferred_element_type=F32)
    a_q = gq_ref[0:1, :] * ct_ref[...]
    b_q = gq_ref[2:3, :] * st_ref[...]
    for h in range(N_HEADS):
        sl = slice(h * HEAD_SLAB, (h + 1) * HEAD_SLAB)
        y = q_all[:, sl]
        r = lax.rsqrt(jnp.sum(y * y, axis=-1, keepdims=True) * (1.0 / QK_HEAD) + EPS)
        q_ref[0, h] = ((y * a_q + q_swp[:, sl] * b_q) * r + gq_ref[1:2, :]).astype(BF16)

    kt_all = jnp.dot(wkt_ref[...], ckvt_ref[...], preferred_element_type=F32)
    krt = krt_ref[...].astype(F32)
    gains = jnp.concatenate([gkt_ref[...]] * reps, axis=1)
    cos_t = ctt_ref[...]
    sin_t = stt_ref[...]
    row = lax.broadcasted_iota(jnp.int32, (HEAD_SLAB - QK_HEAD, tm), 0)
    tail = jnp.where(row == 0, 1.0, 0.0)
    for h in range(N_HEADS):
        y = kt_all[h * HEAD_SLAB:(h + 1) * HEAD_SLAB, :] + krt
        head = y[0:QK_HEAD, :]
        r = lax.rsqrt(jnp.sum(head * head, axis=0, keepdims=True) * (1.0 / QK_HEAD) + EPS)
        yg = y * gains * r
        rope = yg[ROPE_ROWS, :] * cos_t + yg[SWAP_ROWS, :] * sin_t
        k_ref[0, h] = jnp.concatenate([yg[0:QK_NOPE, :], rope, tail], axis=0).astype(BF16)

    v_all = jnp.dot(ckv_ref[...], wv_ref[...], preferred_element_type=F32)
    is_v = lax.broadcasted_iota(jnp.int32, (tm, HEAD_SLAB), 1) < V_HEAD
    for h in range(N_HEADS):
        v_ref[0, h] = jnp.where(is_v, v_all[:, h * HEAD_SLAB:(h + 1) * HEAD_SLAB], 1.0).astype(BF16)


def _mla_prep(cqn, ckvn, ckvt, krt, rope, p, batch, seq, tm):
    T = cqn.shape[0]
    n_s = seq // tm
    row = lambda n: pl.BlockSpec((tm, n), lambda i: (i, 0))
    col = lambda n: pl.BlockSpec((n, tm), lambda i: (0, i))
    hb = pl.BlockSpec((1, N_HEADS, tm, HEAD_SLAB), lambda i: (i // n_s, 0, i % n_s, 0))
    shp = jax.ShapeDtypeStruct((batch, N_HEADS, seq, HEAD_SLAB), BF16)
    wide = N_HEADS * HEAD_SLAB
    return pl.pallas_call(
        _mla_prep_kernel,
        grid=(T // tm,),
        in_specs=[row(Q_LORA), row(KV_LORA), col(KV_LORA), col(HEAD_SLAB),
                  row(HEAD_SLAB), row(HEAD_SLAB), col(QK_ROPE), col(QK_ROPE),
                  _const_spec((Q_LORA, wide)), _const_spec((Q_LORA, wide)), _const_spec((wide, KV_LORA)),
                  _const_spec((KV_LORA, wide)), _const_spec((3, HEAD_SLAB)), _const_spec((HEAD_SLAB, LANES))],
        out_specs=[hb, pl.BlockSpec((1, N_HEADS, HEAD_SLAB, tm), lambda i: (i // n_s, 0, 0, i % n_s)), hb],
        out_shape=[shp, jax.ShapeDtypeStruct((batch, N_HEADS, HEAD_SLAB, seq), BF16), shp],
        compiler_params=_cparams(("arbitrary",), 40),
        name="mla_prep",
    )(cqn, ckvn, ckvt, krt, rope["ct"], rope["st"], rope["ctt"], rope["stt"],
      p["wq"], p["wq_swap"], p["wkt"], p["wv"], p["gq"], p["gkt"])


HEADS_PER_STEP = LANES // V_HEAD


ROW_CHAINS = 2
SAFE_SHIFT_LIMIT = 60.0
FP8_SHIFT_LIMIT = 32.0
MODE_ROW_MAX, MODE_BOUND_SHIFT, MODE_FP8_SCORES = 0, 1, 2


def _attn_kernel(safe_ref, q_ref, kt_ref, v_ref, o_ref):
    tq = q_ref.shape[2]
    rows_per_chain = tq // ROW_CHAINS
    lane = lax.broadcasted_iota(jnp.int32, (rows_per_chain, LANES), 1)
    low = lane < V_HEAD

    def finish(rows, outs):
        outs = [o / jnp.where(low, pltpu.roll(o, V_HEAD, 1), 1.0) for o in outs]
        o_ref[0, rows, :] = jnp.where(low, outs[0], pltpu.roll(outs[1], V_HEAD, 1)).astype(BF16)

    def bound_shifted(dtype):
        kts = [kt_ref[0, j].astype(dtype) for j in range(HEADS_PER_STEP)]
        for r in range(ROW_CHAINS):
            rows = slice(r * rows_per_chain, (r + 1) * rows_per_chain)
            ps = [jnp.exp2(jnp.dot(q_ref[0, j, rows, :].astype(dtype), kts[j],
                                   preferred_element_type=F32).astype(BF16)) for j in range(HEADS_PER_STEP)]
            finish(rows, [jnp.dot(ps[j], v_ref[0, j], preferred_element_type=F32) for j in range(HEADS_PER_STEP)])

    @pl.when(safe_ref[0] == MODE_FP8_SCORES)
    def _():
        bound_shifted(F8)

    @pl.when(safe_ref[0] == MODE_BOUND_SHIFT)
    def _():
        bound_shifted(BF16)

    @pl.when(safe_ref[0] == MODE_ROW_MAX)
    def _():
        for r in range(ROW_CHAINS):
            rows = slice(r * rows_per_chain, (r + 1) * rows_per_chain)
            ss = [jnp.dot(q_ref[0, j, rows, :], kt_ref[0, j], preferred_element_type=F32)
                  for j in range(HEADS_PER_STEP)]
            ps = [jnp.exp2((s - jnp.max(s, axis=-1, keepdims=True)).astype(BF16)) for s in ss]
            finish(rows, [jnp.dot(ps[j], v_ref[0, j], preferred_element_type=F32) for j in range(HEADS_PER_STEP)])


def _attention(safe, q, k, v, tq):
    batch, _, seq, _ = q.shape
    n_hp = N_HEADS // HEADS_PER_STEP
    return pl.pallas_call(
        _attn_kernel,
        grid_spec=pltpu.PrefetchScalarGridSpec(
            num_scalar_prefetch=1, grid=(batch, n_hp, seq // tq),
            in_specs=[pl.BlockSpec((1, HEADS_PER_STEP, tq, HEAD_SLAB), lambda b, h, i, f: (b, h, i, 0)),
                      pl.BlockSpec((1, HEADS_PER_STEP, HEAD_SLAB, seq), lambda b, h, i, f: (b, h, 0, 0)),
                      pl.BlockSpec((1, HEADS_PER_STEP, seq, HEAD_SLAB), lambda b, h, i, f: (b, h, 0, 0))],
            out_specs=pl.BlockSpec((1, tq, LANES), lambda b, h, i, f: (b, i, h))),
        out_shape=jax.ShapeDtypeStruct((batch, seq, N_HEADS * V_HEAD), BF16),
        compiler_params=_cparams(("arbitrary", "arbitrary", "arbitrary"), 52),
        name="attention",
    )(safe, q, k, v)


def _dft_kernel(c_ref, s_ref, a_ref, b_ref, o_ref, acc_ref, *, scale):
    kk = pl.program_id(2)

    @pl.when(kk == 0)
    def _():
        acc_ref[...] = jnp.zeros_like(acc_ref)

    acc_ref[...] += (jnp.dot(c_ref[...], a_ref[...], preferred_element_type=F32)
                     + jnp.dot(s_ref[...], b_ref[...], preferred_element_type=F32))

    @pl.when(kk == pl.num_programs(2) - 1)
    def _():
        o_ref[...] = (acc_ref[...] * scale).astype(BF16)


def _seq_dft(ctab, sneg, fa, fb, tm, tn, tk):
    seq, width = fa.shape
    scale = 1.0 / math.sqrt(seq * FNET_GROUP_DIM)
    return pl.pallas_call(
        functools.partial(_dft_kernel, scale=scale),
        grid=(seq // tm, width // tn, seq // tk),
        in_specs=[pl.BlockSpec((tm, tk), lambda i, j, k: (i, k)),
                  pl.BlockSpec((tm, tk), lambda i, j, k: (i, k)),
                  pl.BlockSpec((tk, tn), lambda i, j, k: (k, j)),
                  pl.BlockSpec((tk, tn), lambda i, j, k: (k, j))],
        out_specs=pl.BlockSpec((tm, tn), lambda i, j, k: (i, j)),
        out_shape=jax.ShapeDtypeStruct((seq, width), BF16),
        scratch_shapes=[pltpu.VMEM((tm, tn), F32)],
        compiler_params=_cparams(("arbitrary", "arbitrary", "arbitrary"), 40),
        name="seq_dft",
    )(ctab, sneg, fa, fb)


def _merge_kernel(x_ref, ya_ref, u_ref, v_ref, yf_ref, gate_ref, ws_ref, bias_ref,
                  wpa_ref, wps_ref, wpf_ref, wo_ref, gn_ref, whi_ref, wlo_ref, brg_ref, utri_ref,
                  xr_ref, meta_ref, cnt_ref, ysgu_ref):
    tm = x_ref.shape[0]
    lane = lax.broadcasted_iota(jnp.int32, (CHUNK, LANES), 1)
    low = lane < SGU_GROUP_DIM
    groups_per_slab = LANES // SGU_GROUP_DIM
    for c in range(tm // CHUNK):
        rows = slice(c * CHUNK, (c + 1) * CHUNK)
        for j in range(SGU_WIDTH // LANES):
            cols = slice(j * LANES, (j + 1) * LANES)
            vb = v_ref[rows, cols]
            ma = jnp.dot(ws_ref[groups_per_slab * j], vb, preferred_element_type=F32)
            mb = jnp.dot(ws_ref[groups_per_slab * j + 1], vb, preferred_element_type=F32)
            mixed = jnp.where(low, ma, mb) + bias_ref[:, cols]
            ysgu_ref[rows, cols] = (u_ref[rows, cols].astype(F32) * mixed).astype(BF16)
    pa = jnp.dot(ya_ref[...], wpa_ref[...], preferred_element_type=F32)
    merged = gate_ref[:, 0:D_MODEL].astype(F32) * pa
    ps = jnp.dot(ysgu_ref[...], wps_ref[...], preferred_element_type=F32)
    merged += gate_ref[:, D_MODEL:2 * D_MODEL].astype(F32) * ps
    pf = jnp.dot(yf_ref[...], wpf_ref[...], preferred_element_type=F32)
    merged += gate_ref[:, 2 * D_MODEL:3 * D_MODEL].astype(F32) * pf
    x_mid = x_ref[...] + jnp.dot(merged.astype(BF16), wo_ref[...], preferred_element_type=F32)
    _route_tile(x_mid, gn_ref, whi_ref, wlo_ref, brg_ref, utri_ref, xr_ref, meta_ref, cnt_ref)


def _merge_route(x2, ya, u, v, yf, gates, ws, bias, wpa, wps, wpf, wo, gn, whi, wlo, brg, utri, seq, tm):
    T = x2.shape[0]
    n_s = seq // tm
    n_tiles = T // tm
    row = lambda n: pl.BlockSpec((tm, n), lambda i: (i, 0))
    return pl.pallas_call(
        _merge_kernel,
        grid=(n_tiles,),
        in_specs=[row(D_MODEL), row(N_HEADS * V_HEAD), row(SGU_WIDTH), row(SGU_WIDTH),
                  pl.BlockSpec((tm, FNET_WIDTH), lambda i: (i % n_s, i // n_s)),
                  row(N_BRANCH * D_MODEL),
                  _const_spec((SGU_GROUPS, CHUNK, CHUNK)), _const_spec((CHUNK, SGU_WIDTH)),
                  _const_spec((N_HEADS * V_HEAD, D_MODEL)), _const_spec((SGU_WIDTH, D_MODEL)),
                  _const_spec((FNET_WIDTH, D_MODEL)), _const_spec((D_MODEL, D_MODEL)),
                  _const_spec((1, D_MODEL)), _const_spec((D_MODEL, LANES)), _const_spec((D_MODEL, LANES)),
                  _const_spec((1, LANES)), _const_spec((tm, tm))],
        out_specs=[row(ROW_EXT), pl.BlockSpec((1, META_ROWS, tm), lambda i: (i, 0, 0)),
                   pl.BlockSpec((1, META_ROWS, LANES), lambda i: (i, 0, 0))],
        out_shape=[jax.ShapeDtypeStruct((T, ROW_EXT), F32),
                   jax.ShapeDtypeStruct((n_tiles, META_ROWS, tm), jnp.int32),
                   jax.ShapeDtypeStruct((n_tiles, META_ROWS, LANES), jnp.int32)],
        scratch_shapes=[pltpu.VMEM((tm, SGU_WIDTH), BF16)],
        compiler_params=_cparams(("arbitrary",), 52),
        name="merge_route",
    )(x2, ya, u, v, yf, gates, ws, bias, wpa, wps, wpf, wo, gn, whi, wlo, brg, utri)


O_GLOGIT = N_EXPERTS


META_ROWS = 8
ROW_EXT = D_MODEL + LANES


def _route_tile(x, g_ref, whi_ref, wlo_ref, b_ref, utri_ref, xr_ref, meta_ref, cnt_ref):
    tm = x.shape[0]
    h = x * lax.rsqrt(jnp.mean(x * x, axis=-1, keepdims=True) + EPS) * g_ref[...]
    hhi = h.astype(BF16)
    hlo = (h - hhi.astype(F32)).astype(BF16)
    logits = (jnp.dot(hhi, whi_ref[...], preferred_element_type=F32)
              + jnp.dot(hlo, whi_ref[...], preferred_element_type=F32)
              + jnp.dot(hhi, wlo_ref[...], preferred_element_type=F32)) + b_ref[...]
    lane = lax.broadcasted_iota(jnp.int32, logits.shape, 1)
    lane_f = lane.astype(F32)
    neg = -jnp.inf
    far = float(2 * LANES)

    is_g = (lane >= O_GLOGIT) & (lane < O_GLOGIT + N_GROUPS)
    gl = jnp.where(is_g, logits, neg)
    gmax = jnp.max(gl, axis=-1, keepdims=True)
    gsel = jnp.min(jnp.where(gl == gmax, lane_f, far), axis=-1, keepdims=True) - float(O_GLOGIT)
    gw = 1.0 / jnp.sum(jnp.where(is_g, jnp.exp(gl - gmax), 0.0), axis=-1, keepdims=True)

    lane_grp = lax.shift_right_logical(lane, 3).astype(F32)
    in_grp = (lane < N_EXPERTS) & (lane_grp == gsel)
    el = jnp.where(in_grp, logits, neg)
    e1 = jnp.max(el, axis=-1, keepdims=True)
    i1 = jnp.min(jnp.where(el == e1, lane_f, far), axis=-1, keepdims=True)
    el2 = jnp.where(lane_f == i1, neg, el)
    e2 = jnp.max(el2, axis=-1, keepdims=True)
    i2 = jnp.min(jnp.where(el2 == e2, lane_f, far), axis=-1, keepdims=True)
    d = jnp.exp(e2 - e1)
    w1 = gw / (1.0 + d)
    w2 = w1 * d
    xr_ref[:, :D_MODEL] = x
    xr_ref[:, D_MODEL:] = jnp.where(lane_f == i1, w1, jnp.where(lane_f == i2, w2, 0.0))

    onehot_t = jnp.where((lane_f == gsel) & (lane < N_GROUPS), 1.0, 0.0).T[0:META_ROWS, :]
    before = jnp.dot(onehot_t, utri_ref[...], preferred_element_type=F32)
    gid = lax.broadcasted_iota(jnp.int32, (META_ROWS, tm), 0).astype(F32)
    gsel_t = jnp.sum(onehot_t * gid, axis=0, keepdims=True)
    rank_t = jnp.sum(onehot_t * before, axis=0, keepdims=True)
    row_id = lax.broadcasted_iota(jnp.int32, (META_ROWS, tm), 0)
    meta_ref[0] = jnp.where(row_id == 0, gsel_t, jnp.where(row_id == 1, rank_t, 0.0)).astype(jnp.int32)
    cnt = jnp.sum(onehot_t, axis=1, keepdims=True)
    cnt_ref[0] = jnp.broadcast_to(cnt, (META_ROWS, LANES)).astype(jnp.int32)


TT_GROUP, TT_USED, TT_PAD_LO, TT_PAD_HI = 0, 1, 2, 3


def _assign_kernel(cnt_ref, meta_ref, dest_ref, tt_ref, *, tmx):
    n_tiles, _, tm = meta_ref.shape
    shift = tmx.bit_length() - 1
    cnt = cnt_ref[...]
    tot = jnp.sum(cnt, axis=0)
    padded = lax.shift_left(lax.shift_right_logical(tot + (tmx - 1), shift), shift)
    starts, ends = [], []
    run = jnp.zeros((1, LANES), jnp.int32)
    for g in range(N_GROUPS):
        starts.append(run)
        run = run + padded[g:g + 1, :]
        ends.append(run)

    def widen(v):
        return jnp.concatenate([v] * (tm // LANES), axis=1)

    def body(i, seen):
        gs = meta_ref[i, 0:1, :]
        dest = meta_ref[i, 1:2, :]
        for g in range(N_GROUPS):
            dest = dest + jnp.where(gs == g, widen(starts[g] + seen[g]), 0)
        dest_ref[i] = dest
        c = cnt_ref[i]
        return tuple(seen[g] + c[g:g + 1, :] for g in range(N_GROUPS))

    lax.fori_loop(0, n_tiles, body, tuple(jnp.zeros((1, LANES), jnp.int32) for _ in range(N_GROUPS)))

    lane = lax.broadcasted_iota(jnp.int32, (1, LANES), 1)
    first_row = lane * tmx
    grp = jnp.zeros((1, LANES), jnp.int32)
    for g in range(N_GROUPS - 1):
        grp = grp + jnp.where(first_row >= ends[g], 1, 0)
    used = lax.shift_right_logical(ends[-1], shift)
    pad_lo = jnp.zeros((1, LANES), jnp.int32)
    pad_hi = jnp.zeros((1, LANES), jnp.int32)
    for g in range(N_GROUPS):
        pad_lo = pad_lo + jnp.where(lane == g, starts[g] + tot[g:g + 1, :], 0)
        pad_hi = pad_hi + jnp.where(lane == g, ends[g], 0)
    n_rows = (n_tiles * tm // tmx + N_GROUPS) * tmx
    pad_lo = pad_lo + jnp.where(lane == N_GROUPS, ends[-1], 0)
    pad_hi = pad_hi + jnp.where(lane == N_GROUPS, n_rows, 0)
    row_id = lax.broadcasted_iota(jnp.int32, (META_ROWS, LANES), 0)
    tt_ref[...] = jnp.where(row_id == TT_GROUP, grp,
                            jnp.where(row_id == TT_USED, used,
                                      jnp.where(row_id == TT_PAD_LO, pad_lo,
                                                jnp.where(row_id == TT_PAD_HI, pad_hi, 0))))


def _assign(cnt, meta, tmx):
    n_tiles, _, tm = meta.shape
    return pl.pallas_call(
        functools.partial(_assign_kernel, tmx=tmx),
        grid=(1,),
        in_specs=[_const_spec(cnt.shape), _const_spec(meta.shape)],
        out_specs=[_const_spec((n_tiles, 1, tm)), _const_spec((META_ROWS, LANES))],
        out_shape=[jax.ShapeDtypeStruct((n_tiles, 1, tm), jnp.int32),
                   jax.ShapeDtypeStruct((META_ROWS, LANES), jnp.int32)],
        compiler_params=_cparams(("arbitrary",), 32),
        name="assign",
    )(cnt, meta)


def _dispatch_kernel(dest_ref, tt_ref, xr_ref, zero_ref, xs_hbm, sem, *, tm):
    base = pl.program_id(0) * tm

    for r in range(tm):
        pltpu.make_async_copy(xr_ref.at[pl.ds(r, 1)], xs_hbm.at[pl.ds(dest_ref[base + r], 1)],
                              sem).start(priority=r % 2)
    pltpu.make_async_copy(xr_ref, xs_hbm.at[pl.ds(0, tm)], sem).wait()

    @pl.when(pl.program_id(0) == 0)
    def _():
        for g in range(N_GROUPS + 1):
            lo = tt_ref[TT_PAD_LO * LANES + g]
            hi = tt_ref[TT_PAD_HI * LANES + g]

            def fill(r, c):
                pltpu.make_async_copy(zero_ref.at[pl.ds(0, 1)], xs_hbm.at[pl.ds(r, 1)], sem).start()
                return c

            lax.fori_loop(lo, hi, fill, 0)

            def drain(r, c):
                pltpu.make_async_copy(zero_ref.at[pl.ds(0, 1)], xs_hbm.at[pl.ds(0, 1)], sem).wait()
                return c

            lax.fori_loop(lo, hi, drain, 0)


def _dispatch(dest, tt, xr, n_rows, tm):
    T, width = xr.shape
    zero = jnp.zeros((META_ROWS, width), xr.dtype)
    return pl.pallas_call(
        functools.partial(_dispatch_kernel, tm=tm),
        grid_spec=pltpu.PrefetchScalarGridSpec(
            num_scalar_prefetch=2, grid=(T // tm,),
            in_specs=[pl.BlockSpec((tm, width), lambda i, d, t: (i, 0)),
                      pl.BlockSpec((META_ROWS, width), lambda i, d, t: (0, 0))],
            out_specs=pl.BlockSpec(memory_space=pl.ANY),
            scratch_shapes=[pltpu.SemaphoreType.DMA(())]),
        out_shape=jax.ShapeDtypeStruct((n_rows, width), xr.dtype),
        compiler_params=_cparams(("arbitrary",), 16),
        name="dispatch",
    )(dest, tt, xr, zero)


def _unsort_kernel(dest_ref, ys_hbm, y_ref, sem, *, tm):
    base = pl.program_id(0) * tm

    for r in range(tm):
        pltpu.make_async_copy(ys_hbm.at[pl.ds(dest_ref[base + r], 1)], y_ref.at[pl.ds(r, 1)],
                              sem).start(priority=r % 2)
    pltpu.make_async_copy(ys_hbm.at[pl.ds(0, tm)], y_ref, sem).wait()


def _unsort(dest, ys, T, tm):
    width = ys.shape[1]
    return pl.pallas_call(
        functools.partial(_unsort_kernel, tm=tm),
        grid_spec=pltpu.PrefetchScalarGridSpec(
            num_scalar_prefetch=1, grid=(T // tm,), in_specs=[pl.BlockSpec(memory_space=pl.ANY)],
            out_specs=pl.BlockSpec((tm, width), lambda i, d: (i, 0)),
            scratch_shapes=[pltpu.SemaphoreType.DMA(())]),
        out_shape=jax.ShapeDtypeStruct((T, width), ys.dtype),
        compiler_params=_cparams(("arbitrary",), 16),
        name="unsort",
    )(dest, ys)


def _experts_kernel(tt_ref, xs_ref, g_ref, w1_ref, w3_ref, w2_ref, o_ref):
    j = pl.program_id(0)

    @pl.when(j < tt_ref[TT_USED * LANES])
    def _():
        grp = tt_ref[TT_GROUP * LANES + j]
        x = xs_ref[:, :D_MODEL]
        cw = xs_ref[:, D_MODEL:]
        h = (x * lax.rsqrt(jnp.mean(x * x, axis=-1, keepdims=True) + EPS) * g_ref[...]).astype(BF16)
        lane = lax.broadcasted_iota(jnp.int32, cw.shape, 1)
        acc = x
        for e in range(EXPERTS_PER_GROUP):
            a = jnp.dot(h, w1_ref[e], preferred_element_type=F32)
            b = jnp.dot(h, w3_ref[e], preferred_element_type=F32)
            col = jnp.sum(jnp.where(lane == grp * EXPERTS_PER_GROUP + e, cw, 0.0), axis=-1, keepdims=True)
            hid = (a / (1.0 + jnp.exp(-a))) * b * col
            acc = acc + jnp.dot(hid.astype(BF16), w2_ref[e], preferred_element_type=F32)
        o_ref[...] = acc

    @pl.when(j >= tt_ref[TT_USED * LANES])
    def _():
        o_ref[...] = jnp.zeros_like(o_ref)


def _experts(tt, xs, g, w1, w3, w2, tmx):
    n_rows = xs.shape[0]
    wspec = lambda a, b: pl.BlockSpec((EXPERTS_PER_GROUP, a, b), lambda j, tt: (tt[TT_GROUP * LANES + j], 0, 0))
    return pl.pallas_call(
        _experts_kernel,
        grid_spec=pltpu.PrefetchScalarGridSpec(
            num_scalar_prefetch=1, grid=(n_rows // tmx,),
            in_specs=[pl.BlockSpec((tmx, ROW_EXT), lambda j, tt: (j, 0)),
                      pl.BlockSpec((1, D_MODEL), lambda j, tt: (0, 0)),
                      wspec(D_MODEL, D_EXPERT), wspec(D_MODEL, D_EXPERT), wspec(D_EXPERT, D_MODEL)],
            out_specs=pl.BlockSpec((tmx, D_MODEL), lambda j, tt: (j, 0))),
        out_shape=jax.ShapeDtypeStruct((n_rows, D_MODEL), F32),
        compiler_params=_cparams(("arbitrary",), 52),
        name="experts",
    )(tt, xs, g, w1, w3, w2)


def _half_swap(w):
    half = QK_ROPE // 2
    return jnp.concatenate([w[..., half:], w[..., :half]], axis=-1)


def _rope_swap(w):
    return jnp.concatenate([w, _half_swap(w)], axis=-1)


def _score_shift(q_norm, k_norm):
    gq = q_norm.astype(F32) * (QK_HEAD ** -0.5 * math.log2(math.e))
    gk = k_norm.astype(F32)
    c = (1.0 + 1.0 / 16.0) ** 2 * 1.01 * QK_HEAD * jnp.max(jnp.abs(gq)) * jnp.max(jnp.abs(gk))
    zeros = lambda n: jnp.zeros((n,), F32)
    tail = HEAD_SLAB - QK_HEAD
    gq_rows = jnp.stack([
        jnp.concatenate([gq, zeros(tail)]),
        jnp.where(jnp.arange(LANES) == SHIFT_LANE, -c, 0.0),
        jnp.concatenate([zeros(QK_NOPE), _half_swap(gq[QK_NOPE:]), zeros(tail)])])
    gkt = jnp.broadcast_to(jnp.concatenate([gk, _half_swap(gk[QK_NOPE:])])[:, None], (HEAD_SLAB, LANES))
    return dict(gq=gq_rows, gkt=gkt,
                safe=jnp.where(c < FP8_SHIFT_LIMIT, MODE_FP8_SCORES,
                               jnp.where(c < SAFE_SHIFT_LIMIT, MODE_BOUND_SHIFT, MODE_ROW_MAX)
                               ).astype(jnp.int32).reshape(1))


def _layer_params(l, attn_norm, w_in, q_a_norm, w_uq, kv_a_norm, w_ukv, q_norm, k_norm, sgu_ln_g, sgu_ln_b,
                  sgu_w, sgu_b, w_proj_attn, w_proj_sgu, w_proj_fnet, w_out, ffn_norm, w_group, b_group,
                  w_router, b_router, w1, w3, w2):
    wi = w_in[l]
    c0 = Q_LORA + KV_LORA
    c1 = c0 + QK_ROPE
    w_ext = jnp.concatenate(
        [wi[:, :c0], jnp.zeros((D_MODEL, QK_NOPE), F32), _rope_swap(wi[:, c0:c1]), wi[:, c1:]], axis=1).astype(BF16)
    wq3 = w_uq[l].reshape(Q_LORA, N_HEADS, QK_HEAD)
    qz = lambda n: jnp.zeros((Q_LORA, N_HEADS, n), F32)
    wq = jnp.concatenate([wq3, qz(HEAD_SLAB - QK_HEAD)], axis=-1)
    wq_swap = jnp.concatenate([qz(QK_NOPE), _half_swap(wq3[..., QK_NOPE:]), qz(HEAD_SLAB - QK_HEAD)], axis=-1)
    wq = wq.reshape(Q_LORA, N_HEADS * HEAD_SLAB).astype(BF16)
    wq_swap = wq_swap.reshape(Q_LORA, N_HEADS * HEAD_SLAB).astype(BF16)
    wkv = w_ukv[l].reshape(KV_LORA, N_HEADS, QK_NOPE + V_HEAD)
    zpad = jnp.zeros((KV_LORA, N_HEADS, HEAD_SLAB - QK_NOPE), F32)
    wkt = jnp.concatenate([wkv[..., :QK_NOPE], zpad], axis=-1).reshape(KV_LORA, N_HEADS * HEAD_SLAB).T.astype(BF16)
    wv = jnp.concatenate([wkv[..., QK_NOPE:], zpad], axis=-1).reshape(KV_LORA, N_HEADS * HEAD_SLAB).astype(BF16)
    w_rg = jnp.concatenate([w_router[l], w_group[l], jnp.zeros((D_MODEL, LANES - N_EXPERTS - N_GROUPS), F32)], axis=1)
    w_rg_hi = w_rg.astype(BF16)
    w_rg_lo = (w_rg - w_rg_hi.astype(F32)).astype(BF16)
    b_rg = jnp.concatenate([b_router[l], b_group[l], jnp.zeros((LANES - N_EXPERTS - N_GROUPS,), F32)])[None, :]
    return dict(
        attn_norm=attn_norm[l][None, :], w_ext=w_ext,
        q_a_norm=q_a_norm[l][None, :], kv_a_norm=kv_a_norm[l][None, :],
        ln_g=sgu_ln_g[l][None, :], ln_b=sgu_ln_b[l][None, :],
        wq=wq, wq_swap=wq_swap, wkt=wkt, wv=wv, **_score_shift(q_norm[l], k_norm[l]),
        sgu_w=sgu_w[l].astype(BF16), sgu_bias=jnp.repeat(sgu_b[l].T, SGU_GROUP_DIM, axis=1),
        wpa=w_proj_attn[l].astype(BF16), wps=w_proj_sgu[l].astype(BF16), wpf=w_proj_fnet[l].astype(BF16),
        wo=w_out[l].astype(BF16),
        ffn_norm=ffn_norm[l][None, :], w_rg_hi=w_rg_hi, w_rg_lo=w_rg_lo, b_rg=b_rg,
        w1=w1[l].astype(BF16), w3=w3[l].astype(BF16), w2=w2[l].astype(BF16),
    )


def _rope_slabs(positions):
    inv = 1.0 / (ROPE_THETA ** (jnp.arange(0, QK_ROPE, 2, dtype=F32) / QK_ROPE))
    ang = positions.astype(F32).reshape(-1)[:, None] * inv
    cos, sin = jnp.cos(ang), jnp.sin(ang)
    T = ang.shape[0]
    c2 = jnp.concatenate([cos, cos], axis=1)
    s2 = jnp.concatenate([-sin, sin], axis=1)
    tail = jnp.zeros((T, HEAD_SLAB - QK_HEAD), F32)
    return dict(ct=jnp.concatenate([jnp.ones((T, QK_NOPE), F32), c2, tail], axis=1),
                st=jnp.concatenate([jnp.zeros((T, QK_NOPE), F32), s2, tail], axis=1),
                ctt=c2.T, stt=s2.T)


def _dft_tables(n):
    r = 1 << ((n.bit_length() - 1) // 2)
    kk = jnp.arange(n, dtype=jnp.int32)[:, None]
    ang_hi = ((kk * (jnp.arange(n // r, dtype=jnp.int32) * r)[None, :]) & (n - 1)).astype(F32) * (2.0 * math.pi / n)
    ang_lo = ((kk * jnp.arange(r, dtype=jnp.int32)[None, :]) & (n - 1)).astype(F32) * (2.0 * math.pi / n)
    ch, sh = jnp.cos(ang_hi)[:, :, None], jnp.sin(ang_hi)[:, :, None]
    cl, sl = jnp.cos(ang_lo)[:, None, :], jnp.sin(ang_lo)[:, None, :]
    c = (ch * cl - sh * sl).reshape(n, n).astype(BF16)
    sneg = (-(sh * cl + ch * sl)).reshape(n, n).astype(BF16)
    m = FNET_GROUP_DIM
    jk = np.outer(np.arange(m), np.arange(m)) % m
    angc = jk.astype(np.float64) * (2.0 * math.pi / m)
    cs = jnp.asarray(np.concatenate([np.cos(angc), np.sin(angc)], axis=1), F32).astype(BF16)
    return c, sneg, cs


def _tile(n, want):
    t = min(n, want)
    assert n % t == 0
    return t


def kernel(x, positions, attn_norm, w_in, q_a_norm, w_uq, kv_a_norm, w_ukv, q_norm, k_norm, sgu_ln_g, sgu_ln_b,
           sgu_w, sgu_b, w_proj_attn, w_proj_sgu, w_proj_fnet, w_out, ffn_norm, w_group, b_group, w_router,
           b_router, w1, w3, w2):
    batch, seq, _ = x.shape
    depth = w_in.shape[0]
    assert seq & (seq - 1) == 0 and seq % CHUNK == 0
    T = batch * seq
    tm = _tile(seq, 512)
    tmx = tm
    ridx = jnp.arange(tm, dtype=jnp.int32)
    utri = (ridx[:, None] < ridx[None, :]).astype(F32)
    rope = _rope_slabs(positions)
    dft_c, dft_sneg, cs = _dft_tables(seq)
    x2 = x.reshape(T, D_MODEL)
    for l in range(depth):
        p = _layer_params(l, attn_norm, w_in, q_a_norm, w_uq, kv_a_norm, w_ukv, q_norm, k_norm, sgu_ln_g,
                          sgu_ln_b, sgu_w, sgu_b, w_proj_attn, w_proj_sgu, w_proj_fnet, w_out, ffn_norm,
                          w_group, b_group, w_router, b_router, w1, w3, w2)
        cqn, ckvn, ckvt, krt, u, v, fa, fb, gates = _inproj(
            x2, p["attn_norm"], p["w_ext"], p["q_a_norm"], p["kv_a_norm"], p["ln_g"], p["ln_b"], cs,
            batch, seq, _tile(seq, 256))
        q, k, vv = _mla_prep(cqn, ckvn, ckvt, krt, rope, p, batch, seq, tm)
        ya = _attention(p["safe"], q, k, vv, _tile(seq, 1024)).reshape(T, N_HEADS * V_HEAD)
        yf = _seq_dft(dft_c, dft_sneg, fa, fb, _tile(seq, 1024), _tile(batch * FNET_WIDTH, 1024), _tile(seq, 1024))
        xr, meta, cnt = _merge_route(
            x2, ya, u, v, yf, gates, p["sgu_w"], p["sgu_bias"], p["wpa"], p["wps"], p["wpf"], p["wo"],
            p["ffn_norm"], p["w_rg_hi"], p["w_rg_lo"], p["b_rg"], utri, seq, tm)
        dest, tt = _assign(cnt, meta, tmx)
        dest = dest.reshape(T)
        tt = tt.reshape(META_ROWS * LANES)
        xs = _dispatch(dest, tt, xr, T + N_GROUPS * tmx, tm)
        ys = _experts(tt, xs, p["ffn_norm"], p["w1"], p["w3"], p["w2"], tmx)
        x2 = _unsort(dest, ys, T, tm)
    return x2.reshape(batch, seq, D_MODEL)
```

```python
import functools
import math

import numpy as np
import jax
import jax.numpy as jnp
from jax import lax
from jax.experimental import pallas as pl
from jax.experimental.pallas import tpu as pltpu

F32 = jnp.float32
BF16 = jnp.bfloat16
F8 = jnp.float8_e4m3fn

D_MODEL = 1024
N_HEADS = 8
QK_NOPE = 64
QK_ROPE = 32
QK_HEAD = QK_NOPE + QK_ROPE
V_HEAD = 64
Q_LORA = 384
KV_LORA = 256
ROPE_THETA = 10000.0
CHUNK = 128
SGU_GROUPS = 8
SGU_GROUP_DIM = 64
SGU_WIDTH = SGU_GROUPS * SGU_GROUP_DIM
FNET_GROUPS = 4
FNET_GROUP_DIM = 128
FNET_WIDTH = FNET_GROUPS * FNET_GROUP_DIM
N_BRANCH = 3
N_GROUPS = 4
EXPERTS_PER_GROUP = 8
N_EXPERTS = N_GROUPS * EXPERTS_PER_GROUP
D_EXPERT = 256
EPS = 1e-6

LANES = 128
HEAD_SLAB = LANES
SHIFT_LANE = QK_HEAD
O_CQ = 0
O_CKV = O_CQ + Q_LORA
O_KR = O_CKV + KV_LORA
O_SGU = O_KR + HEAD_SLAB
O_FN = O_SGU + 2 * SGU_WIDTH
O_GATE = O_FN + FNET_WIDTH
N_IN_EXT = O_GATE + N_BRANCH * D_MODEL

V7X_VMEM_BYTES = 64 * 1024 * 1024


def _cparams(dims, vmem_mb):
    return pltpu.CompilerParams(dimension_semantics=dims,
                                vmem_limit_bytes=min(vmem_mb * 1024 * 1024, V7X_VMEM_BYTES - (6 << 20)))


def _const_spec(shape):
    nd = len(shape)
    return pl.BlockSpec(shape, lambda *_: (0,) * nd)


def _inproj_kernel(x_ref, g_ref, w_ref, qan_ref, kvan_ref, lng_ref, lnb_ref, cs_ref,
                   cq_ref, ckv_ref, ckvt_ref, krt_ref, u_ref, v_ref, fa_ref, fb_ref, gate_ref):
    x = x_ref[...]
    h = (x * lax.rsqrt(jnp.mean(x * x, axis=-1, keepdims=True) + EPS) * g_ref[...]).astype(BF16)

    def seg(a, n):
        return jnp.dot(h, w_ref[:, a:a + n], preferred_element_type=F32)

    cq = seg(O_CQ, Q_LORA)
    cq_ref[...] = (cq * lax.rsqrt(jnp.mean(cq * cq, axis=-1, keepdims=True) + EPS) * qan_ref[...]).astype(BF16)
    ckv = seg(O_CKV, KV_LORA)
    ckvn = ckv * lax.rsqrt(jnp.mean(ckv * ckv, axis=-1, keepdims=True) + EPS) * kvan_ref[...]
    ckv_ref[...] = ckvn.astype(BF16)
    ckvt_ref[...] = ckvn.T.astype(BF16)
    krt_ref[...] = seg(O_KR, HEAD_SLAB).T.astype(BF16)

    u = jax.nn.gelu(seg(O_SGU, SGU_WIDTH))
    u_ref[...] = u.astype(BF16)
    v = jax.nn.gelu(seg(O_SGU + SGU_WIDTH, SGU_WIDTH))
    mu = jnp.mean(v, axis=-1, keepdims=True)
    vc = v - mu
    var = jnp.mean(vc * vc, axis=-1, keepdims=True)
    v_ref[...] = (vc * lax.rsqrt(var + EPS) * lng_ref[...] + lnb_ref[...]).astype(BF16)

    zf = seg(O_FN, FNET_WIDTH).astype(BF16)
    for g in range(FNET_GROUPS):
        sl = slice(g * FNET_GROUP_DIM, (g + 1) * FNET_GROUP_DIM)
        ab = jnp.dot(zf[:, sl], cs_ref[...], preferred_element_type=F32)
        fa_ref[:, sl] = ab[:, :FNET_GROUP_DIM].astype(BF16)
        fb_ref[:, sl] = ab[:, FNET_GROUP_DIM:].astype(BF16)

    for j in range(N_BRANCH):
        gl = seg(O_GATE + j * D_MODEL, D_MODEL)
        gate_ref[:, j * D_MODEL:(j + 1) * D_MODEL] = (1.0 / (1.0 + jnp.exp(-gl))).astype(BF16)


def _inproj(x2, g, w_ext, qan, kvan, lng, lnb, cs, batch, seq, tm):
    T = x2.shape[0]
    n_s = seq // tm
    row = lambda n: pl.BlockSpec((tm, n), lambda i: (i, 0))
    sb = pl.BlockSpec((tm, FNET_WIDTH), lambda i: (i % n_s, i // n_s))
    col = lambda n: pl.BlockSpec((n, tm), lambda i: (0, i))
    outs = [
        jax.ShapeDtypeStruct((T, Q_LORA), BF16),
        jax.ShapeDtypeStruct((T, KV_LORA), BF16),
        jax.ShapeDtypeStruct((KV_LORA, T), BF16),
        jax.ShapeDtypeStruct((HEAD_SLAB, T), BF16),
        jax.ShapeDtypeStruct((T, SGU_WIDTH), BF16),
        jax.ShapeDtypeStruct((T, SGU_WIDTH), BF16),
        jax.ShapeDtypeStruct((seq, batch * FNET_WIDTH), BF16),
        jax.ShapeDtypeStruct((seq, batch * FNET_WIDTH), BF16),
        jax.ShapeDtypeStruct((T, N_BRANCH * D_MODEL), BF16),
    ]
    return pl.pallas_call(
        _inproj_kernel,
        grid=(T // tm,),
        in_specs=[row(D_MODEL), _const_spec((1, D_MODEL)), _const_spec((D_MODEL, N_IN_EXT)),
                  _const_spec((1, Q_LORA)), _const_spec((1, KV_LORA)),
                  _const_spec((1, SGU_WIDTH)), _const_spec((1, SGU_WIDTH)),
                  _const_spec((FNET_GROUP_DIM, 2 * FNET_GROUP_DIM))],
        out_specs=[row(Q_LORA), row(KV_LORA), col(KV_LORA), col(HEAD_SLAB), row(SGU_WIDTH), row(SGU_WIDTH),
                   sb, sb, row(N_BRANCH * D_MODEL)],
        out_shape=outs,
        compiler_params=_cparams(("arbitrary",), 52),
        name="inproj",
    )(x2, g, w_ext, qan, kvan, lng, lnb, cs)


ROPE_ROWS = slice(QK_NOPE, QK_HEAD)
SWAP_ROWS = slice(QK_HEAD, HEAD_SLAB)


def _mla_prep_kernel(cq_ref, ckv_ref, ckvt_ref, krt_ref, ct_ref, st_ref, ctt_ref, stt_ref,
                     wq_ref, wqs_ref, wkt_ref, wv_ref, gq_ref, gkt_ref, q_ref, k_ref, v_ref):
    cq = cq_ref[...]
    tm = cq.shape[0]
    reps = tm // LANES
    q_all = jnp.dot(cq, wq_ref[...], preferred_element_type=F32)
    q_swp = jnp.dot(cq, wqs_ref[...], preferred_element_type=F32)
    a_q = gq_ref[0:1, :] * ct_ref[...]
    b_q = gq_ref[2:3, :] * st_ref[...]
    for h in range(N_HEADS):
        sl = slice(h * HEAD_SLAB, (h + 1) * HEAD_SLAB)
        y = q_all[:, sl]
        r = lax.rsqrt(jnp.sum(y * y, axis=-1, keepdims=True) * (1.0 / QK_HEAD) + EPS)
        q_ref[0, h] = ((y * a_q + q_swp[:, sl] * b_q) * r + gq_ref[1:2, :]).astype(BF16)

    kt_all = jnp.dot(wkt_ref[...], ckvt_ref[...], preferred_element_type=F32)
    krt = krt_ref[...].astype(F32)
    gains = jnp.concatenate([gkt_ref[...]] * reps, axis=1)
    cos_t = ctt_ref[...]
    sin_t = stt_ref[...]
    row = lax.broadcasted_iota(jnp.int32, (HEAD_SLAB - QK_HEAD, tm), 0)
    tail = jnp.where(row == 0, 1.0, 0.0)
    for h in range(N_HEADS):
        y = kt_all[h * HEAD_SLAB:(h + 1) * HEAD_SLAB, :] + krt
        head = y[0:QK_HEAD, :]
        r = lax.rsqrt(jnp.sum(head * head, axis=0, keepdims=True) * (1.0 / QK_HEAD) + EPS)
        yg = y * gains * r
        rope = yg[ROPE_ROWS, :] * cos_t + yg[SWAP_ROWS, :] * sin_t
        k_ref[0, h] = jnp.concatenate([yg[0:QK_NOPE, :], rope, tail], axis=0).astype(BF16)

    v_all = jnp.dot(ckv_ref[...], wv_ref[...], preferred_element_type=F32)
    is_v = lax.broadcasted_iota(jnp.int32, (tm, HEAD_SLAB), 1) < V_HEAD
    for h in range(N_HEADS):
        v_ref[0, h] = jnp.where(is_v, v_all[:, h * HEAD_SLAB:(h + 1) * HEAD_SLAB], 1.0).astype(BF16)


def _mla_prep(cqn, ckvn, ckvt, krt, rope, p, batch, seq, tm):
    T = cqn.shape[0]
    n_s = seq // tm
    row = lambda n: pl.BlockSpec((tm, n), lambda i: (i, 0))
    col = lambda n: pl.BlockSpec((n, tm), lambda i: (0, i))
    hb = pl.BlockSpec((1, N_HEADS, tm, HEAD_SLAB), lambda i: (i // n_s, 0, i % n_s, 0))
    shp = jax.ShapeDtypeStruct((batch, N_HEADS, seq, HEAD_SLAB), BF16)
    wide = N_HEADS * HEAD_SLAB
    return pl.pallas_call(
        _mla_prep_kernel,
        grid=(T // tm,),
        in_specs=[row(Q_LORA), row(KV_LORA), col(KV_LORA), col(HEAD_SLAB),
                  row(HEAD_SLAB), row(HEAD_SLAB), col(QK_ROPE), col(QK_ROPE),
                  _const_spec((Q_LORA, wide)), _const_spec((Q_LORA, wide)), _const_spec((wide, KV_LORA)),
                  _const_spec((KV_LORA, wide)), _const_spec((3, HEAD_SLAB)), _const_spec((HEAD_SLAB, LANES))],
        out_specs=[hb, pl.BlockSpec((1, N_HEADS, HEAD_SLAB, tm), lambda i: (i // n_s, 0, 0, i % n_s)), hb],
        out_shape=[shp, jax.ShapeDtypeStruct((batch, N_HEADS, HEAD_SLAB, seq), BF16), shp],
        compiler_params=_cparams(("arbitrary",), 40),
        name="mla_prep",
    )(cqn, ckvn, ckvt, krt, rope["ct"], rope["st"], rope["ctt"], rope["stt"],
      p["wq"], p["wq_swap"], p["wkt"], p["wv"], p["gq"], p["gkt"])


HEADS_PER_STEP = LANES // V_HEAD


ROW_CHAINS = 2
SAFE_SHIFT_LIMIT = 60.0
FP8_SHIFT_LIMIT = 32.0
MODE_ROW_MAX, MODE_BOUND_SHIFT, MODE_FP8_SCORES = 0, 1, 2


def _attn_kernel(safe_ref, q_ref, kt_ref, v_ref, o_ref):
    tq = q_ref.shape[2]
    rows_per_chain = tq // ROW_CHAINS
    lane = lax.broadcasted_iota(jnp.int32, (rows_per_chain, LANES), 1)
    low = lane < V_HEAD

    def finish(rows, outs):
        outs = [o / jnp.where(low, pltpu.roll(o, V_HEAD, 1), 1.0) for o in outs]
        o_ref[0, rows, :] = jnp.where(low, outs[0], pltpu.roll(outs[1], V_HEAD, 1)).astype(BF16)

    def bound_shifted(dtype):
        kts = [kt_ref[0, j].astype(dtype) for j in range(HEADS_PER_STEP)]
        for r in range(ROW_CHAINS):
            rows = slice(r * rows_per_chain, (r + 1) * rows_per_chain)
            ps = [jnp.exp2(jnp.dot(q_ref[0, j, rows, :].astype(dtype), kts[j],
                                   preferred_element_type=F32).astype(BF16)) for j in range(HEADS_PER_STEP)]
            finish(rows, [jnp.dot(ps[j], v_ref[0, j], preferred_element_type=F32) for j in range(HEADS_PER_STEP)])

    @pl.when(safe_ref[0] == MODE_FP8_SCORES)
    def _():
        bound_shifted(F8)

    @pl.when(safe_ref[0] == MODE_BOUND_SHIFT)
    def _():
        bound_shifted(BF16)

    @pl.when(safe_ref[0] == MODE_ROW_MAX)
    def _():
        for r in range(ROW_CHAINS):
            rows = slice(r * rows_per_chain, (r + 1) * rows_per_chain)
            ss = [jnp.dot(q_ref[0, j, rows, :], kt_ref[0, j], preferred_element_type=F32)
                  for j in range(HEADS_PER_STEP)]
            ps = [jnp.exp2((s - jnp.max(s, axis=-1, keepdims=True)).astype(BF16)) for s in ss]
            finish(rows, [jnp.dot(ps[j], v_ref[0, j], preferred_element_type=F32) for j in range(HEADS_PER_STEP)])


def _attention(safe, q, k, v, tq):
    batch, _, seq, _ = q.shape
    n_hp = N_HEADS // HEADS_PER_STEP
    return pl.pallas_call(
        _attn_kernel,
        grid_spec=pltpu.PrefetchScalarGridSpec(
            num_scalar_prefetch=1, grid=(batch, n_hp, seq // tq),
            in_specs=[pl.BlockSpec((1, HEADS_PER_STEP, tq, HEAD_SLAB), lambda b, h, i, f: (b, h, i, 0)),
                      pl.BlockSpec((1, HEADS_PER_STEP, HEAD_SLAB, seq), lambda b, h, i, f: (b, h, 0, 0)),
                      pl.BlockSpec((1, HEADS_PER_STEP, seq, HEAD_SLAB), lambda b, h, i, f: (b, h, 0, 0))],
            out_specs=pl.BlockSpec((1, tq, LANES), lambda b, h, i, f: (b, i, h))),
        out_shape=jax.ShapeDtypeStruct((batch, seq, N_HEADS * V_HEAD), BF16),
        compiler_params=_cparams(("arbitrary", "arbitrary", "arbitrary"), 52),
        name="attention",
    )(safe, q, k, v)


def _dft_kernel(c_ref, s_ref, a_ref, b_ref, o_ref, acc_ref, *, scale):
    kk = pl.program_id(2)

    @pl.when(kk == 0)
    def _():
        acc_ref[...] = jnp.zeros_like(acc_ref)

    acc_ref[...] += (jnp.dot(c_ref[...], a_ref[...], preferred_element_type=F32)
                     + jnp.dot(s_ref[...], b_ref[...], preferred_element_type=F32))

    @pl.when(kk == pl.num_programs(2) - 1)
    def _():
        o_ref[...] = (acc_ref[...] * scale).astype(BF16)


def _seq_dft(ctab, sneg, fa, fb, tm, tn, tk):
    seq, width = fa.shape
    scale = 1.0 / math.sqrt(seq * FNET_GROUP_DIM)
    return pl.pallas_call(
        functools.partial(_dft_kernel, scale=scale),
        grid=(seq // tm, width // tn, seq // tk),
        in_specs=[pl.BlockSpec((tm, tk), lambda i, j, k: (i, k)),
                  pl.BlockSpec((tm, tk), lambda i, j, k: (i, k)),
                  pl.BlockSpec((tk, tn), lambda i, j, k: (k, j)),
                  pl.BlockSpec((tk, tn), lambda i, j, k: (k, j))],
        out_specs=pl.BlockSpec((tm, tn), lambda i, j, k: (i, j)),
        out_shape=jax.ShapeDtypeStruct((seq, width), BF16),
        scratch_shapes=[pltpu.VMEM((tm, tn), F32)],
        compiler_params=_cparams(("arbitrary", "arbitrary", "arbitrary"), 40),
        name="seq_dft",
    )(ctab, sneg, fa, fb)


def _merge_kernel(x_ref, ya_ref, u_ref, v_ref, yf_ref, gate_ref, ws_ref, bias_ref,
                  wpa_ref, wps_ref, wpf_ref, wo_ref, gn_ref, whi_ref, wlo_ref, brg_ref, utri_ref,
                  xr_ref, meta_ref, cnt_ref, ysgu_ref):
    tm = x_ref.shape[0]
    lane = lax.broadcasted_iota(jnp.int32, (CHUNK, LANES), 1)
    low = lane < SGU_GROUP_DIM
    groups_per_slab = LANES // SGU_GROUP_DIM
    for c in range(tm // CHUNK):
        rows = slice(c * CHUNK, (c + 1) * CHUNK)
        for j in range(SGU_WIDTH // LANES):
            cols = slice(j * LANES, (j + 1) * LANES)
            vb = v_ref[rows, cols]
            ma = jnp.dot(ws_ref[groups_per_slab * j], vb, preferred_element_type=F32)
            mb = jnp.dot(ws_ref[groups_per_slab * j + 1], vb, preferred_element_type=F32)
            mixed = jnp.where(low, ma, mb) + bias_ref[:, cols]
            ysgu_ref[rows, cols] = (u_ref[rows, cols].astype(F32) * mixed).astype(BF16)
    pa = jnp.dot(ya_ref[...], wpa_ref[...], preferred_element_type=F32)
    merged = gate_ref[:, 0:D_MODEL].astype(F32) * pa
    ps = jnp.dot(ysgu_ref[...], wps_ref[...], preferred_element_type=F32)
    merged += gate_ref[:, D_MODEL:2 * D_MODEL].astype(F32) * ps
    pf = jnp.dot(yf_ref[...], wpf_ref[...], preferred_element_type=F32)
    merged += gate_ref[:, 2 * D_MODEL:3 * D_MODEL].astype(F32) * pf
    x_mid = x_ref[...] + jnp.dot(merged.astype(BF16), wo_ref[...], preferred_element_type=F32)
    _route_tile(x_mid, gn_ref, whi_ref, wlo_ref, brg_ref, utri_ref, xr_ref, meta_ref, cnt_ref)


def _merge_route(x2, ya, u, v, yf, gates, ws, bias, wpa, wps, wpf, wo, gn, whi, wlo, brg, utri, seq, tm):
    T = x2.shape[0]
    n_s = seq // tm
    n_tiles = T // tm
    row = lambda n: pl.BlockSpec((tm, n), lambda i: (i, 0))
    return pl.pallas_call(
        _merge_kernel,
        grid=(n_tiles,),
        in_specs=[row(D_MODEL), row(N_HEADS * V_HEAD), row(SGU_WIDTH), row(SGU_WIDTH),
                  pl.BlockSpec((tm, FNET_WIDTH), lambda i: (i % n_s, i // n_s)),
                  row(N_BRANCH * D_MODEL),
                  _const_spec((SGU_GROUPS, CHUNK, CHUNK)), _const_spec((CHUNK, SGU_WIDTH)),
                  _const_spec((N_HEADS * V_HEAD, D_MODEL)), _const_spec((SGU_WIDTH, D_MODEL)),
                  _const_spec((FNET_WIDTH, D_MODEL)), _const_spec((D_MODEL, D_MODEL)),
                  _const_spec((1, D_MODEL)), _const_spec((D_MODEL, LANES)), _const_spec((D_MODEL, LANES)),
                  _const_spec((1, LANES)), _const_spec((tm, tm))],
        out_specs=[row(ROW_EXT), pl.BlockSpec((1, META_ROWS, tm), lambda i: (i, 0, 0)),
                   pl.BlockSpec((1, META_ROWS, LANES), lambda i: (i, 0, 0))],
        out_shape=[jax.ShapeDtypeStruct((T, ROW_EXT), F32),
                   jax.ShapeDtypeStruct((n_tiles, META_ROWS, tm), jnp.int32),
                   jax.ShapeDtypeStruct((n_tiles, META_ROWS, LANES), jnp.int32)],
        scratch_shapes=[pltpu.VMEM((tm, SGU_WIDTH), BF16)],
        compiler_params=_cparams(("arbitrary",), 52),
        name="merge_route",
    )(x2, ya, u, v, yf, gates, ws, bias, wpa, wps, wpf, wo, gn, whi, wlo, brg, utri)


O_GLOGIT = N_EXPERTS


META_ROWS = 8
ROW_EXT = D_MODEL + LANES


def _route_tile(x, g_ref, whi_ref, wlo_ref, b_ref, utri_ref, xr_ref, meta_ref, cnt_ref):
    tm = x.shape[0]
    h = x * lax.rsqrt(jnp.mean(x * x, axis=-1, keepdims=True) + EPS) * g_ref[...]
    hhi = h.astype(BF16)
    hlo = (h - hhi.astype(F32)).astype(BF16)
    logits = (jnp.dot(hhi, whi_ref[...], preferred_element_type=F32)
              + jnp.dot(hlo, whi_ref[...], preferred_element_type=F32)
              + jnp.dot(hhi, wlo_ref[...], preferred_element_type=F32)) + b_ref[...]
    lane = lax.broadcasted_iota(jnp.int32, logits.shape, 1)
    lane_f = lane.astype(F32)
    neg = -jnp.inf
    far = float(2 * LANES)

    is_g = (lane >= O_GLOGIT) & (lane < O_GLOGIT + N_GROUPS)
    gl = jnp.where(is_g, logits, neg)
    gmax = jnp.max(gl, axis=-1, keepdims=True)
    gsel = jnp.min(jnp.where(gl == gmax, lane_f, far), axis=-1, keepdims=True) - float(O_GLOGIT)
    gw = 1.0 / jnp.sum(jnp.where(is_g, jnp.exp(gl - gmax), 0.0), axis=-1, keepdims=True)

    lane_grp = lax.shift_right_logical(lane, 3).astype(F32)
    in_grp = (lane < N_EXPERTS) & (lane_grp == gsel)
    el = jnp.where(in_grp, logits, neg)
    e1 = jnp.max(el, axis=-1, keepdims=True)
    i1 = jnp.min(jnp.where(el == e1, lane_f, far), axis=-1, keepdims=True)
    el2 = jnp.where(lane_f == i1, neg, el)
    e2 = jnp.max(el2, axis=-1, keepdims=True)
    i2 = jnp.min(jnp.where(el2 == e2, lane_f, far), axis=-1, keepdims=True)
    d = jnp.exp(e2 - e1)
    w1 = gw / (1.0 + d)
    w2 = w1 * d
    xr_ref[:, :D_MODEL] = x
    xr_ref[:, D_MODEL:] = jnp.where(lane_f == i1, w1, jnp.where(lane_f == i2, w2, 0.0))

    onehot_t = jnp.where((lane_f == gsel) & (lane < N_GROUPS), 1.0, 0.0).T[0:META_ROWS, :]
    before = jnp.dot(onehot_t, utri_ref[...], preferred_element_type=F32)
    gid = lax.broadcasted_iota(jnp.int32, (META_ROWS, tm), 0).astype(F32)
    gsel_t = jnp.sum(onehot_t * gid, axis=0, keepdims=True)
    rank_t = jnp.sum(onehot_t * before, axis=0, keepdims=True)
    row_id = lax.broadcasted_iota(jnp.int32, (META_ROWS, tm), 0)
    meta_ref[0] = jnp.where(row_id == 0, gsel_t, jnp.where(row_id == 1, rank_t, 0.0)).astype(jnp.int32)
    cnt = jnp.sum(onehot_t, axis=1, keepdims=True)
    cnt_ref[0] = jnp.broadcast_to(cnt, (META_ROWS, LANES)).astype(jnp.int32)


TT_GROUP, TT_USED, TT_PAD_LO, TT_PAD_HI = 0, 1, 2, 3


def _assign_kernel(cnt_ref, meta_ref, dest_ref, tt_ref, *, tmx):
    n_tiles, _, tm = meta_ref.shape
    shift = tmx.bit_length() - 1
    cnt = cnt_ref[...]
    tot = jnp.sum(cnt, axis=0)
    padded = lax.shift_left(lax.shift_right_logical(tot + (tmx - 1), shift), shift)
    starts, ends = [], []
    run = jnp.zeros((1, LANES), jnp.int32)
    for g in range(N_GROUPS):
        starts.append(run)
        run = run + padded[g:g + 1, :]
        ends.append(run)

    def widen(v):
        return jnp.concatenate([v] * (tm // LANES), axis=1)

    def body(i, seen):
        gs = meta_ref[i, 0:1, :]
        dest = meta_ref[i, 1:2, :]
        for g in range(N_GROUPS):
            dest = dest + jnp.where(gs == g, widen(starts[g] + seen[g]), 0)
        dest_ref[i] = dest
        c = cnt_ref[i]
        return tuple(seen[g] + c[g:g + 1, :] for g in range(N_GROUPS))

    lax.fori_loop(0, n_tiles, body, tuple(jnp.zeros((1, LANES), jnp.int32) for _ in range(N_GROUPS)))

    lane = lax.broadcasted_iota(jnp.int32, (1, LANES), 1)
    first_row = lane * tmx
    grp = jnp.zeros((1, LANES), jnp.int32)
    for g in range(N_GROUPS - 1):
        grp = grp + jnp.where(first_row >= ends[g], 1, 0)
    used = lax.shift_right_logical(ends[-1], shift)
    pad_lo = jnp.zeros((1, LANES), jnp.int32)
    pad_hi = jnp.zeros((1, LANES), jnp.int32)
    for g in range(N_GROUPS):
        pad_lo = pad_lo + jnp.where(lane == g, starts[g] + tot[g:g + 1, :], 0)
        pad_hi = pad_hi + jnp.where(lane == g, ends[g], 0)
    n_rows = (n_tiles * tm // tmx + N_GROUPS) * tmx
    pad_lo = pad_lo + jnp.where(lane == N_GROUPS, ends[-1], 0)
    pad_hi = pad_hi + jnp.where(lane == N_GROUPS, n_rows, 0)
    row_id = lax.broadcasted_iota(jnp.int32, (META_ROWS, LANES), 0)
    tt_ref[...] = jnp.where(row_id == TT_GROUP, grp,
                            jnp.where(row_id == TT_USED, used,
                                      jnp.where(row_id == TT_PAD_LO, pad_lo,
                                                jnp.where(row_id == TT_PAD_HI, pad_hi, 0))))


def _assign(cnt, meta, tmx):
    n_tiles, _, tm = meta.shape
    return pl.pallas_call(
        functools.partial(_assign_kernel, tmx=tmx),
        grid=(1,),
        in_specs=[_const_spec(cnt.shape), _const_spec(meta.shape)],
        out_specs=[_const_spec((n_tiles, 1, tm)), _const_spec((META_ROWS, LANES))],
        out_shape=[jax.ShapeDtypeStruct((n_tiles, 1, tm), jnp.int32),
                   jax.ShapeDtypeStruct((META_ROWS, LANES), jnp.int32)],
        compiler_params=_cparams(("arbitrary",), 32),
        name="assign",
    )(cnt, meta)


def _dispatch_kernel(dest_ref, tt_ref, xr_ref, zero_ref, xs_hbm, sem, *, tm):
    base = pl.program_id(0) * tm

    for r in range(tm):
        pltpu.make_async_copy(xr_ref.at[pl.ds(r, 1)], xs_hbm.at[pl.ds(dest_ref[base + r], 1)],
                              sem).start(priority=r % 2)
    pltpu.make_async_copy(xr_ref, xs_hbm.at[pl.ds(0, tm)], sem).wait()

    @pl.when(pl.program_id(0) == 0)
    def _():
        for g in range(N_GROUPS + 1):
            lo = tt_ref[TT_PAD_LO * LANES + g]
            hi = tt_ref[TT_PAD_HI * LANES + g]

            def fill(r, c):
                pltpu.make_async_copy(zero_ref.at[pl.ds(0, 1)], xs_hbm.at[pl.ds(r, 1)], sem).start()
                return c

            lax.fori_loop(lo, hi, fill, 0)

            def drain(r, c):
                pltpu.make_async_copy(zero_ref.at[pl.ds(0, 1)], xs_hbm.at[pl.ds(0, 1)], sem).wait()
                return c

            lax.fori_loop(lo, hi, drain, 0)


def _dispatch(dest, tt, xr, n_rows, tm):
    T, width = xr.shape
    zero = jnp.zeros((META_ROWS, width), xr.dtype)
    return pl.pallas_call(
        functools.partial(_dispatch_kernel, tm=tm),
        grid_spec=pltpu.PrefetchScalarGridSpec(
            num_scalar_prefetch=2, grid=(T // tm,),
            in_specs=[pl.BlockSpec((tm, width), lambda i, d, t: (i, 0)),
                      pl.BlockSpec((META_ROWS, width), lambda i, d, t: (0, 0))],
            out_specs=pl.BlockSpec(memory_space=pl.ANY),
            scratch_shapes=[pltpu.SemaphoreType.DMA(())]),
        out_shape=jax.ShapeDtypeStruct((n_rows, width), xr.dtype),
        compiler_params=_cparams(("arbitrary",), 16),
        name="dispatch",
    )(dest, tt, xr, zero)


def _unsort_kernel(dest_ref, ys_hbm, y_ref, sem, *, tm):
    base = pl.program_id(0) * tm

    for r in range(tm):
        pltpu.make_async_copy(ys_hbm.at[pl.ds(dest_ref[base + r], 1)], y_ref.at[pl.ds(r, 1)],
                              sem).start(priority=r % 2)
    pltpu.make_async_copy(ys_hbm.at[pl.ds(0, tm)], y_ref, sem).wait()


def _unsort(dest, ys, T, tm):
    width = ys.shape[1]
    return pl.pallas_call(
        functools.partial(_unsort_kernel, tm=tm),
        grid_spec=pltpu.PrefetchScalarGridSpec(
            num_scalar_prefetch=1, grid=(T // tm,), in_specs=[pl.BlockSpec(memory_space=pl.ANY)],
            out_specs=pl.BlockSpec((tm, width), lambda i, d: (i, 0)),
            scratch_shapes=[pltpu.SemaphoreType.DMA(())]),
        out_shape=jax.ShapeDtypeStruct((T, width), ys.dtype),
        compiler_params=_cparams(("arbitrary",), 16),
        name="unsort",
    )(dest, ys)


def _experts_kernel(tt_ref, xs_ref, g_ref, w1_ref, w3_ref, w2_ref, o_ref):
    j = pl.program_id(0)

    @pl.when(j < tt_ref[TT_USED * LANES])
    def _():
        grp = tt_ref[TT_GROUP * LANES + j]
        x = xs_ref[:, :D_MODEL]
        cw = xs_ref[:, D_MODEL:]
        h = (x * lax.rsqrt(jnp.mean(x * x, axis=-1, keepdims=True) + EPS) * g_ref[...]).astype(BF16)
        lane = lax.broadcasted_iota(jnp.int32, cw.shape, 1)
        acc = x
        for e in range(EXPERTS_PER_GROUP):
            a = jnp.dot(h, w1_ref[0, e], preferred_element_type=F32)
            b = jnp.dot(h, w3_ref[0, e], preferred_element_type=F32)
            col = jnp.sum(jnp.where(lane == grp * EXPERTS_PER_GROUP + e, cw, 0.0), axis=-1, keepdims=True)
            hid = (a / (1.0 + jnp.exp(-a))) * b * col
            acc = acc + jnp.dot(hid.astype(BF16), w2_ref[0, e], preferred_element_type=F32)
        o_ref[...] = acc

    @pl.when(j >= tt_ref[TT_USED * LANES])
    def _():
        o_ref[...] = jnp.zeros_like(o_ref)


def _experts(tt, xs, g, w1, w3, w2, layer, tmx):
    n_rows = xs.shape[0]
    wspec = lambda a, b: pl.BlockSpec((1, EXPERTS_PER_GROUP, a, b),
                                      lambda j, tt: (layer, tt[TT_GROUP * LANES + j], 0, 0))
    return pl.pallas_call(
        _experts_kernel,
        grid_spec=pltpu.PrefetchScalarGridSpec(
            num_scalar_prefetch=1, grid=(n_rows // tmx,),
            in_specs=[pl.BlockSpec((tmx, ROW_EXT), lambda j, tt: (j, 0)),
                      pl.BlockSpec((1, D_MODEL), lambda j, tt: (0, 0)),
                      wspec(D_MODEL, D_EXPERT), wspec(D_MODEL, D_EXPERT), wspec(D_EXPERT, D_MODEL)],
            out_specs=pl.BlockSpec((tmx, D_MODEL), lambda j, tt: (j, 0))),
        out_shape=jax.ShapeDtypeStruct((n_rows, D_MODEL), F32),
        compiler_params=_cparams(("arbitrary",), 52),
        name="experts",
    )(tt, xs, g, w1, w3, w2)


def _half_swap(w):
    half = QK_ROPE // 2
    return jnp.concatenate([w[..., half:], w[..., :half]], axis=-1)


def _rope_swap(w):
    return jnp.concatenate([w, _half_swap(w)], axis=-1)


def _score_shift(q_norm, k_norm):
    gq = q_norm.astype(F32) * (QK_HEAD ** -0.5 * math.log2(math.e))
    gk = k_norm.astype(F32)
    c = (1.0 + 1.0 / 16.0) ** 2 * 1.01 * QK_HEAD * jnp.max(jnp.abs(gq)) * jnp.max(jnp.abs(gk))
    zeros = lambda n: jnp.zeros((n,), F32)
    tail = HEAD_SLAB - QK_HEAD
    gq_rows = jnp.stack([
        jnp.concatenate([gq, zeros(tail)]),
        jnp.where(jnp.arange(LANES) == SHIFT_LANE, -c, 0.0),
        jnp.concatenate([zeros(QK_NOPE), _half_swap(gq[QK_NOPE:]), zeros(tail)])])
    gkt = jnp.broadcast_to(jnp.concatenate([gk, _half_swap(gk[QK_NOPE:])])[:, None], (HEAD_SLAB, LANES))
    return dict(gq=gq_rows, gkt=gkt,
                safe=jnp.where(c < FP8_SHIFT_LIMIT, MODE_FP8_SCORES,
                               jnp.where(c < SAFE_SHIFT_LIMIT, MODE_BOUND_SHIFT, MODE_ROW_MAX)
                               ).astype(jnp.int32).reshape(1))


def _layer_params(l, attn_norm, w_in, q_a_norm, w_uq, kv_a_norm, w_ukv, q_norm, k_norm, sgu_ln_g, sgu_ln_b,
                  sgu_w, sgu_b, w_proj_attn, w_proj_sgu, w_proj_fnet, w_out, ffn_norm, w_group, b_group,
                  w_router, b_router):
    wi = w_in[l]
    c0 = Q_LORA + KV_LORA
    c1 = c0 + QK_ROPE
    w_ext = jnp.concatenate(
        [wi[:, :c0], jnp.zeros((D_MODEL, QK_NOPE), F32), _rope_swap(wi[:, c0:c1]), wi[:, c1:]], axis=1).astype(BF16)
    wq3 = w_uq[l].reshape(Q_LORA, N_HEADS, QK_HEAD)
    qz = lambda n: jnp.zeros((Q_LORA, N_HEADS, n), F32)
    wq = jnp.concatenate([wq3, qz(HEAD_SLAB - QK_HEAD)], axis=-1)
    wq_swap = jnp.concatenate([qz(QK_NOPE), _half_swap(wq3[..., QK_NOPE:]), qz(HEAD_SLAB - QK_HEAD)], axis=-1)
    wq = wq.reshape(Q_LORA, N_HEADS * HEAD_SLAB).astype(BF16)
    wq_swap = wq_swap.reshape(Q_LORA, N_HEADS * HEAD_SLAB).astype(BF16)
    wkv = w_ukv[l].reshape(KV_LORA, N_HEADS, QK_NOPE + V_HEAD)
    zpad = jnp.zeros((KV_LORA, N_HEADS, HEAD_SLAB - QK_NOPE), F32)
    wkt = jnp.concatenate([wkv[..., :QK_NOPE], zpad], axis=-1).reshape(KV_LORA, N_HEADS * HEAD_SLAB).T.astype(BF16)
    wv = jnp.concatenate([wkv[..., QK_NOPE:], zpad], axis=-1).reshape(KV_LORA, N_HEADS * HEAD_SLAB).astype(BF16)
    w_rg = jnp.concatenate([w_router[l], w_group[l], jnp.zeros((D_MODEL, LANES - N_EXPERTS - N_GROUPS), F32)], axis=1)
    w_rg_hi = w_rg.astype(BF16)
    w_rg_lo = (w_rg - w_rg_hi.astype(F32)).astype(BF16)
    b_rg = jnp.concatenate([b_router[l], b_group[l], jnp.zeros((LANES - N_EXPERTS - N_GROUPS,), F32)])[None, :]
    return dict(
        attn_norm=attn_norm[l][None, :], w_ext=w_ext,
        q_a_norm=q_a_norm[l][None, :], kv_a_norm=kv_a_norm[l][None, :],
        ln_g=sgu_ln_g[l][None, :], ln_b=sgu_ln_b[l][None, :],
        wq=wq, wq_swap=wq_swap, wkt=wkt, wv=wv, **_score_shift(q_norm[l], k_norm[l]),
        sgu_w=sgu_w[l].astype(BF16), sgu_bias=jnp.repeat(sgu_b[l].T, SGU_GROUP_DIM, axis=1),
        wpa=w_proj_attn[l].astype(BF16), wps=w_proj_sgu[l].astype(BF16), wpf=w_proj_fnet[l].astype(BF16),
        wo=w_out[l].astype(BF16),
        ffn_norm=ffn_norm[l][None, :], w_rg_hi=w_rg_hi, w_rg_lo=w_rg_lo, b_rg=b_rg,
    )


def _rope_slabs(positions):
    inv = 1.0 / (ROPE_THETA ** (jnp.arange(0, QK_ROPE, 2, dtype=F32) / QK_ROPE))
    ang = positions.astype(F32).reshape(-1)[:, None] * inv
    cos, sin = jnp.cos(ang), jnp.sin(ang)
    T = ang.shape[0]
    c2 = jnp.concatenate([cos, cos], axis=1)
    s2 = jnp.concatenate([-sin, sin], axis=1)
    tail = jnp.zeros((T, HEAD_SLAB - QK_HEAD), F32)
    return dict(ct=jnp.concatenate([jnp.ones((T, QK_NOPE), F32), c2, tail], axis=1),
                st=jnp.concatenate([jnp.zeros((T, QK_NOPE), F32), s2, tail], axis=1),
                ctt=c2.T, stt=s2.T)


def _dft_tables(n):
    r = min(LANES, n)
    kk = jnp.arange(n, dtype=jnp.int32)[:, None]
    ang_hi = ((kk * (jnp.arange(n // r, dtype=jnp.int32) * r)[None, :]) & (n - 1)).astype(F32) * (2.0 * math.pi / n)
    ang_lo = ((kk * jnp.arange(r, dtype=jnp.int32)[None, :]) & (n - 1)).astype(F32) * (2.0 * math.pi / n)
    ch, sh = jnp.cos(ang_hi)[:, :, None], jnp.sin(ang_hi)[:, :, None]
    cl, sl = jnp.cos(ang_lo)[:, None, :], jnp.sin(ang_lo)[:, None, :]
    c = (ch * cl - sh * sl).reshape(n, n).astype(BF16)
    sneg = (-(sh * cl + ch * sl)).reshape(n, n).astype(BF16)
    m = FNET_GROUP_DIM
    jk = np.outer(np.arange(m), np.arange(m)) % m
    angc = jk.astype(np.float64) * (2.0 * math.pi / m)
    cs = jnp.asarray(np.concatenate([np.cos(angc), np.sin(angc)], axis=1), F32).astype(BF16)
    return c, sneg, cs


def _tile(n, want):
    t = min(n, want)
    assert n % t == 0
    return t


def kernel(x, positions, attn_norm, w_in, q_a_norm, w_uq, kv_a_norm, w_ukv, q_norm, k_norm, sgu_ln_g, sgu_ln_b,
           sgu_w, sgu_b, w_proj_attn, w_proj_sgu, w_proj_fnet, w_out, ffn_norm, w_group, b_group, w_router,
           b_router, w1, w3, w2):
    batch, seq, _ = x.shape
    depth = w_in.shape[0]
    assert seq & (seq - 1) == 0 and seq % CHUNK == 0
    T = batch * seq
    tm = _tile(seq, 512)
    tmx = tm
    ridx = jnp.arange(tm, dtype=jnp.int32)
    utri = (ridx[:, None] < ridx[None, :]).astype(F32)
    rope = _rope_slabs(positions)
    dft_c, dft_sneg, cs = _dft_tables(seq)
    x2 = x.reshape(T, D_MODEL)
    w1b, w3b, w2b = w1.astype(BF16), w3.astype(BF16), w2.astype(BF16)
    for l in range(depth):
        p = _layer_params(l, attn_norm, w_in, q_a_norm, w_uq, kv_a_norm, w_ukv, q_norm, k_norm, sgu_ln_g,
                          sgu_ln_b, sgu_w, sgu_b, w_proj_attn, w_proj_sgu, w_proj_fnet, w_out, ffn_norm,
                          w_group, b_group, w_router, b_router)
        cqn, ckvn, ckvt, krt, u, v, fa, fb, gates = _inproj(
            x2, p["attn_norm"], p["w_ext"], p["q_a_norm"], p["kv_a_norm"], p["ln_g"], p["ln_b"], cs,
            batch, seq, _tile(seq, 256))
        q, k, vv = _mla_prep(cqn, ckvn, ckvt, krt, rope, p, batch, seq, tm)
        ya = _attention(p["safe"], q, k, vv, _tile(seq, 2048)).reshape(T, N_HEADS * V_HEAD)
        yf = _seq_dft(dft_c, dft_sneg, fa, fb, _tile(seq, 1024), _tile(batch * FNET_WIDTH, 1024), _tile(seq, 1024))
        xr, meta, cnt = _merge_route(
            x2, ya, u, v, yf, gates, p["sgu_w"], p["sgu_bias"], p["wpa"], p["wps"], p["wpf"], p["wo"],
            p["ffn_norm"], p["w_rg_hi"], p["w_rg_lo"], p["b_rg"], utri, seq, tm)
        dest, tt = _assign(cnt, meta, tmx)
        dest = dest.reshape(T)
        tt = tt.reshape(META_ROWS * LANES)
        xs = _dispatch(dest, tt, xr, T + N_GROUPS * tmx, tm)
        ys = _experts(tt, xs, p["ffn_norm"], w1b, w3b, w2b, l, tmx)
        x2 = _unsort(dest, ys, T, tm)
    return x2.reshape(batch, seq, D_MODEL)
```

```python
import functools
import math

import numpy as np
import jax
import jax.numpy as jnp
from jax import lax
from jax.experimental import pallas as pl
from jax.experimental.pallas import tpu as pltpu

F32 = jnp.float32
BF16 = jnp.bfloat16
F8 = jnp.float8_e4m3fn

D_MODEL = 1024
N_HEADS = 8
QK_NOPE = 64
QK_ROPE = 32
QK_HEAD = QK_NOPE + QK_ROPE
V_HEAD = 64
Q_LORA = 384
KV_LORA = 256
ROPE_THETA = 10000.0
CHUNK = 128
SGU_GROUPS = 8
SGU_GROUP_DIM = 64
SGU_WIDTH = SGU_GROUPS * SGU_GROUP_DIM
FNET_GROUPS = 4
FNET_GROUP_DIM = 128
FNET_WIDTH = FNET_GROUPS * FNET_GROUP_DIM
N_BRANCH = 3
N_GROUPS = 4
EXPERTS_PER_GROUP = 8
N_EXPERTS = N_GROUPS * EXPERTS_PER_GROUP
D_EXPERT = 256
EPS = 1e-6

LANES = 128
HEAD_SLAB = LANES
SHIFT_LANE = QK_HEAD
O_CQ = 0
O_CKV = O_CQ + Q_LORA
O_KR = O_CKV + KV_LORA
O_SGU = O_KR + HEAD_SLAB
O_FN = O_SGU + 2 * SGU_WIDTH
O_GATE = O_FN + FNET_WIDTH
N_IN_EXT = O_GATE + N_BRANCH * D_MODEL

V7X_VMEM_BYTES = 64 * 1024 * 1024


def _cparams(dims, vmem_mb):
    return pltpu.CompilerParams(dimension_semantics=dims,
                                vmem_limit_bytes=min(vmem_mb * 1024 * 1024, V7X_VMEM_BYTES - (6 << 20)))


def _const_spec(shape):
    nd = len(shape)
    return pl.BlockSpec(shape, lambda *_: (0,) * nd)


def _inproj_kernel(x_ref, g_ref, w_ref, qan_ref, kvan_ref, lng_ref, lnb_ref, cs_ref,
                   cq_ref, ckv_ref, ckvt_ref, krt_ref, u_ref, v_ref, fa_ref, fb_ref, gate_ref):
    x = x_ref[...]
    h = (x * lax.rsqrt(jnp.mean(x * x, axis=-1, keepdims=True) + EPS) * g_ref[...]).astype(BF16)

    def seg(a, n):
        return jnp.dot(h, w_ref[:, a:a + n], preferred_element_type=F32)

    cq = seg(O_CQ, Q_LORA)
    cq_ref[...] = (cq * lax.rsqrt(jnp.mean(cq * cq, axis=-1, keepdims=True) + EPS) * qan_ref[...]).astype(BF16)
    ckv = seg(O_CKV, KV_LORA)
    ckvn = ckv * lax.rsqrt(jnp.mean(ckv * ckv, axis=-1, keepdims=True) + EPS) * kvan_ref[...]
    ckv_ref[...] = ckvn.astype(BF16)
    ckvt_ref[...] = ckvn.T.astype(BF16)
    krt_ref[...] = seg(O_KR, HEAD_SLAB).T.astype(BF16)

    u = jax.nn.gelu(seg(O_SGU, SGU_WIDTH))
    u_ref[...] = u.astype(BF16)
    v = jax.nn.gelu(seg(O_SGU + SGU_WIDTH, SGU_WIDTH))
    mu = jnp.mean(v, axis=-1, keepdims=True)
    vc = v - mu
    var = jnp.mean(vc * vc, axis=-1, keepdims=True)
    v_ref[...] = (vc * lax.rsqrt(var + EPS) * lng_ref[...] + lnb_ref[...]).astype(BF16)

    zf = seg(O_FN, FNET_WIDTH).astype(BF16)
    for g in range(FNET_GROUPS):
        sl = slice(g * FNET_GROUP_DIM, (g + 1) * FNET_GROUP_DIM)
        ab = jnp.dot(zf[:, sl], cs_ref[...], preferred_element_type=F32)
        fa_ref[:, sl] = ab[:, :FNET_GROUP_DIM].astype(BF16)
        fb_ref[:, sl] = ab[:, FNET_GROUP_DIM:].astype(BF16)

    for j in range(N_BRANCH):
        gl = seg(O_GATE + j * D_MODEL, D_MODEL)
        gate_ref[:, j * D_MODEL:(j + 1) * D_MODEL] = (1.0 / (1.0 + jnp.exp(-gl))).astype(BF16)


def _inproj(x2, g, w_ext, qan, kvan, lng, lnb, cs, batch, seq, tm):
    T = x2.shape[0]
    n_s = seq // tm
    row = lambda n: pl.BlockSpec((tm, n), lambda i: (i, 0))
    sb = pl.BlockSpec((tm, FNET_WIDTH), lambda i: (i % n_s, i // n_s))
    col = lambda n: pl.BlockSpec((n, tm), lambda i: (0, i))
    outs = [
        jax.ShapeDtypeStruct((T, Q_LORA), BF16),
        jax.ShapeDtypeStruct((T, KV_LORA), BF16),
        jax.ShapeDtypeStruct((KV_LORA, T), BF16),
        jax.ShapeDtypeStruct((HEAD_SLAB, T), BF16),
        jax.ShapeDtypeStruct((T, SGU_WIDTH), BF16),
        jax.ShapeDtypeStruct((T, SGU_WIDTH), BF16),
        jax.ShapeDtypeStruct((seq, batch * FNET_WIDTH), BF16),
        jax.ShapeDtypeStruct((seq, batch * FNET_WIDTH), BF16),
        jax.ShapeDtypeStruct((T, N_BRANCH * D_MODEL), BF16),
    ]
    return pl.pallas_call(
        _inproj_kernel,
        grid=(T // tm,),
        in_specs=[row(D_MODEL), _const_spec((1, D_MODEL)), _const_spec((D_MODEL, N_IN_EXT)),
                  _const_spec((1, Q_LORA)), _const_spec((1, KV_LORA)),
                  _const_spec((1, SGU_WIDTH)), _const_spec((1, SGU_WIDTH)),
                  _const_spec((FNET_GROUP_DIM, 2 * FNET_GROUP_DIM))],
        out_specs=[row(Q_LORA), row(KV_LORA), col(KV_LORA), col(HEAD_SLAB), row(SGU_WIDTH), row(SGU_WIDTH),
                   sb, sb, row(N_BRANCH * D_MODEL)],
        out_shape=outs,
        compiler_params=_cparams(("arbitrary",), 52),
        name="inproj",
    )(x2, g, w_ext, qan, kvan, lng, lnb, cs)


ROPE_ROWS = slice(QK_NOPE, QK_HEAD)
SWAP_ROWS = slice(QK_HEAD, HEAD_SLAB)


def _mla_prep_kernel(cq_ref, ckv_ref, ckvt_ref, krt_ref, ct_ref, st_ref, ctt_ref, stt_ref,
                     wq_ref, wqs_ref, wkt_ref, wv_ref, gq_ref, gkt_ref, q_ref, k_ref, v_ref):
    cq = cq_ref[...]
    tm = cq.shape[0]
    reps = tm // LANES
    q_all = jnp.dot(cq, wq_ref[...], preferred_element_type=F32)
    q_swp = jnp.dot(cq, wqs_ref[...], preferred_element_type=F32)
    a_q = gq_ref[0:1, :] * ct_ref[...]
    b_q = gq_ref[2:3, :] * st_ref[...]
    for h in range(N_HEADS):
        sl = slice(h * HEAD_SLAB, (h + 1) * HEAD_SLAB)
        y = q_all[:, sl]
        r = lax.rsqrt(jnp.sum(y * y, axis=-1, keepdims=True) * (1.0 / QK_HEAD) + EPS)
        q_ref[0, h] = ((y * a_q + q_swp[:, sl] * b_q) * r + gq_ref[1:2, :]).astype(BF16)

    kt_all = jnp.dot(wkt_ref[...], ckvt_ref[...], preferred_element_type=F32)
    krt = krt_ref[...].astype(F32)
    gains = jnp.concatenate([gkt_ref[...]] * reps, axis=1)
    cos_t = ctt_ref[...]
    sin_t = stt_ref[...]
    row = lax.broadcasted_iota(jnp.int32, (HEAD_SLAB - QK_HEAD, tm), 0)
    tail = jnp.where(row == 0, 1.0, 0.0)
    for h in range(N_HEADS):
        y = kt_all[h * HEAD_SLAB:(h + 1) * HEAD_SLAB, :] + krt
        head = y[0:QK_HEAD, :]
        r = lax.rsqrt(jnp.sum(head * head, axis=0, keepdims=True) * (1.0 / QK_HEAD) + EPS)
        yg = y * gains * r
        rope = yg[ROPE_ROWS, :] * cos_t + yg[SWAP_ROWS, :] * sin_t
        k_ref[0, h] = jnp.concatenate([yg[0:QK_NOPE, :], rope, tail], axis=0).astype(BF16)

    v_all = jnp.dot(ckv_ref[...], wv_ref[...], preferred_element_type=F32)
    is_v = lax.broadcasted_iota(jnp.int32, (tm, HEAD_SLAB), 1) < V_HEAD
    for h in range(N_HEADS):
        v_ref[0, h] = jnp.where(is_v, v_all[:, h * HEAD_SLAB:(h + 1) * HEAD_SLAB], 1.0).astype(BF16)


def _mla_prep(cqn, ckvn, ckvt, krt, rope, p, batch, seq, tm):
    T = cqn.shape[0]
    n_s = seq // tm
    row = lambda n: pl.BlockSpec((tm, n), lambda i: (i, 0))
    col = lambda n: pl.BlockSpec((n, tm), lambda i: (0, i))
    hb = pl.BlockSpec((1, N_HEADS, tm, HEAD_SLAB), lambda i: (i // n_s, 0, i % n_s, 0))
    shp = jax.ShapeDtypeStruct((batch, N_HEADS, seq, HEAD_SLAB), BF16)
    wide = N_HEADS * HEAD_SLAB
    return pl.pallas_call(
        _mla_prep_kernel,
        grid=(T // tm,),
        in_specs=[row(Q_LORA), row(KV_LORA), col(KV_LORA), col(HEAD_SLAB),
                  row(HEAD_SLAB), row(HEAD_SLAB), col(QK_ROPE), col(QK_ROPE),
                  _const_spec((Q_LORA, wide)), _const_spec((Q_LORA, wide)), _const_spec((wide, KV_LORA)),
                  _const_spec((KV_LORA, wide)), _const_spec((3, HEAD_SLAB)), _const_spec((HEAD_SLAB, LANES))],
        out_specs=[hb, pl.BlockSpec((1, N_HEADS, HEAD_SLAB, tm), lambda i: (i // n_s, 0, 0, i % n_s)), hb],
        out_shape=[shp, jax.ShapeDtypeStruct((batch, N_HEADS, HEAD_SLAB, seq), BF16), shp],
        compiler_params=_cparams(("arbitrary",), 40),
        name="mla_prep",
    )(cqn, ckvn, ckvt, krt, rope["ct"], rope["st"], rope["ctt"], rope["stt"],
      p["wq"], p["wq_swap"], p["wkt"], p["wv"], p["gq"], p["gkt"])


HEADS_PER_STEP = LANES // V_HEAD


ROW_CHAINS = 2
SAFE_SHIFT_LIMIT = 60.0
FP8_SHIFT_LIMIT = 32.0
MODE_ROW_MAX, MODE_BOUND_SHIFT, MODE_FP8_SCORES = 0, 1, 2


def _attn_kernel(safe_ref, q_ref, kt_ref, v_ref, o_ref):
    tq = q_ref.shape[2]
    rows_per_chain = tq // ROW_CHAINS
    lane = lax.broadcasted_iota(jnp.int32, (rows_per_chain, LANES), 1)
    low = lane < V_HEAD

    def finish(rows, outs):
        outs = [o / jnp.where(low, pltpu.roll(o, V_HEAD, 1), 1.0) for o in outs]
        o_ref[0, rows, :] = jnp.where(low, outs[0], pltpu.roll(outs[1], V_HEAD, 1)).astype(BF16)

    def bound_shifted(dtype):
        kts = [kt_ref[0, j].astype(dtype) for j in range(HEADS_PER_STEP)]
        for r in range(ROW_CHAINS):
            rows = slice(r * rows_per_chain, (r + 1) * rows_per_chain)
            ps = [jnp.exp2(jnp.dot(q_ref[0, j, rows, :].astype(dtype), kts[j],
                                   preferred_element_type=F32).astype(BF16)) for j in range(HEADS_PER_STEP)]
            finish(rows, [jnp.dot(ps[j], v_ref[0, j], preferred_element_type=F32) for j in range(HEADS_PER_STEP)])

    @pl.when(safe_ref[0] == MODE_FP8_SCORES)
    def _():
        bound_shifted(F8)

    @pl.when(safe_ref[0] == MODE_BOUND_SHIFT)
    def _():
        bound_shifted(BF16)

    @pl.when(safe_ref[0] == MODE_ROW_MAX)
    def _():
        for r in range(ROW_CHAINS):
            rows = slice(r * rows_per_chain, (r + 1) * rows_per_chain)
            ss = [jnp.dot(q_ref[0, j, rows, :], kt_ref[0, j], preferred_element_type=F32)
                  for j in range(HEADS_PER_STEP)]
            ps = [jnp.exp2((s - jnp.max(s, axis=-1, keepdims=True)).astype(BF16)) for s in ss]
            finish(rows, [jnp.dot(ps[j], v_ref[0, j], preferred_element_type=F32) for j in range(HEADS_PER_STEP)])


def _attention(safe, q, k, v, tq):
    batch, _, seq, _ = q.shape
    n_hp = N_HEADS // HEADS_PER_STEP
    return pl.pallas_call(
        _attn_kernel,
        grid_spec=pltpu.PrefetchScalarGridSpec(
            num_scalar_prefetch=1, grid=(batch, n_hp, seq // tq),
            in_specs=[pl.BlockSpec((1, HEADS_PER_STEP, tq, HEAD_SLAB), lambda b, h, i, f: (b, h, i, 0)),
                      pl.BlockSpec((1, HEADS_PER_STEP, HEAD_SLAB, seq), lambda b, h, i, f: (b, h, 0, 0)),
                      pl.BlockSpec((1, HEADS_PER_STEP, seq, HEAD_SLAB), lambda b, h, i, f: (b, h, 0, 0))],
            out_specs=pl.BlockSpec((1, tq, LANES), lambda b, h, i, f: (b, i, h))),
        out_shape=jax.ShapeDtypeStruct((batch, seq, N_HEADS * V_HEAD), BF16),
        compiler_params=_cparams(("arbitrary", "arbitrary", "arbitrary"), 52),
        name="attention",
    )(safe, q, k, v)


def _dft_kernel(c_ref, s_ref, a_ref, b_ref, o_ref, acc_ref, *, scale):
    kk = pl.program_id(2)

    @pl.when(kk == 0)
    def _():
        acc_ref[...] = jnp.zeros_like(acc_ref)

    acc_ref[...] += (jnp.dot(c_ref[...], a_ref[...], preferred_element_type=F32)
                     + jnp.dot(s_ref[...], b_ref[...], preferred_element_type=F32))

    @pl.when(kk == pl.num_programs(2) - 1)
    def _():
        o_ref[...] = (acc_ref[...] * scale).astype(BF16)


def _seq_dft(ctab, sneg, fa, fb, tm, tn, tk):
    seq, width = fa.shape
    scale = 1.0 / math.sqrt(seq * FNET_GROUP_DIM)
    return pl.pallas_call(
        functools.partial(_dft_kernel, scale=scale),
        grid=(seq // tm, width // tn, seq // tk),
        in_specs=[pl.BlockSpec((tm, tk), lambda i, j, k: (i, k)),
                  pl.BlockSpec((tm, tk), lambda i, j, k: (i, k)),
                  pl.BlockSpec((tk, tn), lambda i, j, k: (k, j)),
                  pl.BlockSpec((tk, tn), lambda i, j, k: (k, j))],
        out_specs=pl.BlockSpec((tm, tn), lambda i, j, k: (i, j)),
        out_shape=jax.ShapeDtypeStruct((seq, width), BF16),
        scratch_shapes=[pltpu.VMEM((tm, tn), F32)],
        compiler_params=_cparams(("arbitrary", "arbitrary", "arbitrary"), 40),
        name="seq_dft",
    )(ctab, sneg, fa, fb)


def _merge_kernel(x_ref, ya_ref, u_ref, v_ref, yf_ref, gate_ref, ws_ref, bias_ref,
                  wpa_ref, wps_ref, wpf_ref, wo_ref, gn_ref, whi_ref, wlo_ref, brg_ref, utri_ref,
                  xr_ref, meta_ref, cnt_ref, ysgu_ref):
    tm = x_ref.shape[0]
    lane = lax.broadcasted_iota(jnp.int32, (CHUNK, LANES), 1)
    low = lane < SGU_GROUP_DIM
    groups_per_slab = LANES // SGU_GROUP_DIM
    for c in range(tm // CHUNK):
        rows = slice(c * CHUNK, (c + 1) * CHUNK)
        for j in range(SGU_WIDTH // LANES):
            cols = slice(j * LANES, (j + 1) * LANES)
            vb = v_ref[rows, cols]
            ma = jnp.dot(ws_ref[groups_per_slab * j], vb, preferred_element_type=F32)
            mb = jnp.dot(ws_ref[groups_per_slab * j + 1], vb, preferred_element_type=F32)
            mixed = jnp.where(low, ma, mb) + bias_ref[:, cols]
            ysgu_ref[rows, cols] = (u_ref[rows, cols].astype(F32) * mixed).astype(BF16)
    pa = jnp.dot(ya_ref[...], wpa_ref[...], preferred_element_type=F32)
    merged = gate_ref[:, 0:D_MODEL].astype(F32) * pa
    ps = jnp.dot(ysgu_ref[...], wps_ref[...], preferred_element_type=F32)
    merged += gate_ref[:, D_MODEL:2 * D_MODEL].astype(F32) * ps
    pf = jnp.dot(yf_ref[...], wpf_ref[...], preferred_element_type=F32)
    merged += gate_ref[:, 2 * D_MODEL:3 * D_MODEL].astype(F32) * pf
    x_mid = x_ref[...] + jnp.dot(merged.astype(BF16), wo_ref[...], preferred_element_type=F32)
    _route_tile(x_mid, gn_ref, whi_ref, wlo_ref, brg_ref, utri_ref, xr_ref, meta_ref, cnt_ref)


def _merge_route(x2, ya, u, v, yf, gates, ws, bias, wpa, wps, wpf, wo, gn, whi, wlo, brg, utri, seq, tm):
    T = x2.shape[0]
    n_s = seq // tm
    n_tiles = T // tm
    row = lambda n: pl.BlockSpec((tm, n), lambda i: (i, 0))
    return pl.pallas_call(
        _merge_kernel,
        grid=(n_tiles,),
        in_specs=[row(D_MODEL), row(N_HEADS * V_HEAD), row(SGU_WIDTH), row(SGU_WIDTH),
                  pl.BlockSpec((tm, FNET_WIDTH), lambda i: (i % n_s, i // n_s)),
                  row(N_BRANCH * D_MODEL),
                  _const_spec((SGU_GROUPS, CHUNK, CHUNK)), _const_spec((CHUNK, SGU_WIDTH)),
                  _const_spec((N_HEADS * V_HEAD, D_MODEL)), _const_spec((SGU_WIDTH, D_MODEL)),
                  _const_spec((FNET_WIDTH, D_MODEL)), _const_spec((D_MODEL, D_MODEL)),
                  _const_spec((1, D_MODEL)), _const_spec((D_MODEL, LANES)), _const_spec((D_MODEL, LANES)),
                  _const_spec((1, LANES)), _const_spec((tm, tm))],
        out_specs=[row(ROW_EXT), pl.BlockSpec((1, META_ROWS, tm), lambda i: (i, 0, 0)),
                   pl.BlockSpec((1, META_ROWS, LANES), lambda i: (i, 0, 0))],
        out_shape=[jax.ShapeDtypeStruct((T, ROW_EXT), F32),
                   jax.ShapeDtypeStruct((n_tiles, META_ROWS, tm), jnp.int32),
                   jax.ShapeDtypeStruct((n_tiles, META_ROWS, LANES), jnp.int32)],
        scratch_shapes=[pltpu.VMEM((tm, SGU_WIDTH), BF16)],
        compiler_params=_cparams(("arbitrary",), 52),
        name="merge_route",
    )(x2, ya, u, v, yf, gates, ws, bias, wpa, wps, wpf, wo, gn, whi, wlo, brg, utri)


O_GLOGIT = N_EXPERTS


META_ROWS = 8
ROW_EXT = D_MODEL + LANES


def _route_tile(x, g_ref, whi_ref, wlo_ref, b_ref, utri_ref, xr_ref, meta_ref, cnt_ref):
    tm = x.shape[0]
    h = x * lax.rsqrt(jnp.mean(x * x, axis=-1, keepdims=True) + EPS) * g_ref[...]
    hhi = h.astype(BF16)
    hlo = (h - hhi.astype(F32)).astype(BF16)
    logits = (jnp.dot(hhi, whi_ref[...], preferred_element_type=F32)
              + jnp.dot(hlo, whi_ref[...], preferred_element_type=F32)
              + jnp.dot(hhi, wlo_ref[...], preferred_element_type=F32)) + b_ref[...]
    lane = lax.broadcasted_iota(jnp.int32, logits.shape, 1)
    lane_f = lane.astype(F32)
    neg = -jnp.inf
    far = float(2 * LANES)

    is_g = (lane >= O_GLOGIT) & (lane < O_GLOGIT + N_GROUPS)
    gl = jnp.where(is_g, logits, neg)
    gmax = jnp.max(gl, axis=-1, keepdims=True)
    gsel = jnp.min(jnp.where(gl == gmax, lane_f, far), axis=-1, keepdims=True) - float(O_GLOGIT)
    gw = 1.0 / jnp.sum(jnp.where(is_g, jnp.exp(gl - gmax), 0.0), axis=-1, keepdims=True)

    lane_grp = lax.shift_right_logical(lane, 3).astype(F32)
    in_grp = (lane < N_EXPERTS) & (lane_grp == gsel)
    el = jnp.where(in_grp, logits, neg)
    e1 = jnp.max(el, axis=-1, keepdims=True)
    i1 = jnp.min(jnp.where(el == e1, lane_f, far), axis=-1, keepdims=True)
    el2 = jnp.where(lane_f == i1, neg, el)
    e2 = jnp.max(el2, axis=-1, keepdims=True)
    i2 = jnp.min(jnp.where(el2 == e2, lane_f, far), axis=-1, keepdims=True)
    d = jnp.exp(e2 - e1)
    w1 = gw / (1.0 + d)
    w2 = w1 * d
    xr_ref[:, :D_MODEL] = x
    xr_ref[:, D_MODEL:] = jnp.where(lane_f == i1, w1, jnp.where(lane_f == i2, w2, 0.0))

    onehot_t = jnp.where((lane_f == gsel) & (lane < N_GROUPS), 1.0, 0.0).T[0:META_ROWS, :]
    before = jnp.dot(onehot_t, utri_ref[...], preferred_element_type=F32)
    gid = lax.broadcasted_iota(jnp.int32, (META_ROWS, tm), 0).astype(F32)
    gsel_t = jnp.sum(onehot_t * gid, axis=0, keepdims=True)
    rank_t = jnp.sum(onehot_t * before, axis=0, keepdims=True)
    row_id = lax.broadcasted_iota(jnp.int32, (META_ROWS, tm), 0)
    meta_ref[0] = jnp.where(row_id == 0, gsel_t, jnp.where(row_id == 1, rank_t, 0.0)).astype(jnp.int32)
    cnt = jnp.sum(onehot_t, axis=1, keepdims=True)
    cnt_ref[0] = jnp.broadcast_to(cnt, (META_ROWS, LANES)).astype(jnp.int32)


TT_GROUP, TT_USED, TT_PAD_LO, TT_PAD_HI = 0, 1, 2, 3


def _assign_kernel(cnt_ref, meta_ref, dest_ref, tt_ref, *, tmx):
    n_tiles, _, tm = meta_ref.shape
    shift = tmx.bit_length() - 1
    cnt = cnt_ref[...]
    tot = jnp.sum(cnt, axis=0)
    padded = lax.shift_left(lax.shift_right_logical(tot + (tmx - 1), shift), shift)
    starts, ends = [], []
    run = jnp.zeros((1, LANES), jnp.int32)
    for g in range(N_GROUPS):
        starts.append(run)
        run = run + padded[g:g + 1, :]
        ends.append(run)

    def widen(v):
        return jnp.concatenate([v] * (tm // LANES), axis=1)

    def body(i, seen):
        gs = meta_ref[i, 0:1, :]
        dest = meta_ref[i, 1:2, :]
        for g in range(N_GROUPS):
            dest = dest + jnp.where(gs == g, widen(starts[g] + seen[g]), 0)
        dest_ref[i] = dest
        c = cnt_ref[i]
        return tuple(seen[g] + c[g:g + 1, :] for g in range(N_GROUPS))

    lax.fori_loop(0, n_tiles, body, tuple(jnp.zeros((1, LANES), jnp.int32) for _ in range(N_GROUPS)))

    lane = lax.broadcasted_iota(jnp.int32, (1, LANES), 1)
    first_row = lane * tmx
    grp = jnp.zeros((1, LANES), jnp.int32)
    for g in range(N_GROUPS - 1):
        grp = grp + jnp.where(first_row >= ends[g], 1, 0)
    used = lax.shift_right_logical(ends[-1], shift)
    pad_lo = jnp.zeros((1, LANES), jnp.int32)
    pad_hi = jnp.zeros((1, LANES), jnp.int32)
    for g in range(N_GROUPS):
        pad_lo = pad_lo + jnp.where(lane == g, starts[g] + tot[g:g + 1, :], 0)
        pad_hi = pad_hi + jnp.where(lane == g, ends[g], 0)
    n_rows = (n_tiles * tm // tmx + N_GROUPS) * tmx
    pad_lo = pad_lo + jnp.where(lane == N_GROUPS, ends[-1], 0)
    pad_hi = pad_hi + jnp.where(lane == N_GROUPS, n_rows, 0)
    row_id = lax.broadcasted_iota(jnp.int32, (META_ROWS, LANES), 0)
    tt_ref[...] = jnp.where(row_id == TT_GROUP, grp,
                            jnp.where(row_id == TT_USED, used,
                                      jnp.where(row_id == TT_PAD_LO, pad_lo,
                                                jnp.where(row_id == TT_PAD_HI, pad_hi, 0))))


def _assign(cnt, meta, tmx):
    n_tiles, _, tm = meta.shape
    return pl.pallas_call(
        functools.partial(_assign_kernel, tmx=tmx),
        grid=(1,),
        in_specs=[_const_spec(cnt.shape), _const_spec(meta.shape)],
        out_specs=[_const_spec((n_tiles, 1, tm)), _const_spec((META_ROWS, LANES))],
        out_shape=[jax.ShapeDtypeStruct((n_tiles, 1, tm), jnp.int32),
                   jax.ShapeDtypeStruct((META_ROWS, LANES), jnp.int32)],
        compiler_params=_cparams(("arbitrary",), 32),
        name="assign",
    )(cnt, meta)


def _dispatch_kernel(dest_ref, tt_ref, xr_ref, zero_ref, xs_hbm, sem, *, tm):
    base = pl.program_id(0) * tm

    for r in range(tm):
        pltpu.make_async_copy(xr_ref.at[pl.ds(r, 1)], xs_hbm.at[pl.ds(dest_ref[base + r], 1)],
                              sem).start(priority=r % 2)
    pltpu.make_async_copy(xr_ref, xs_hbm.at[pl.ds(0, tm)], sem).wait()

    @pl.when(pl.program_id(0) == 0)
    def _():
        for g in range(N_GROUPS + 1):
            lo = tt_ref[TT_PAD_LO * LANES + g]
            hi = tt_ref[TT_PAD_HI * LANES + g]

            def fill(r, c):
                pltpu.make_async_copy(zero_ref.at[pl.ds(0, 1)], xs_hbm.at[pl.ds(r, 1)], sem).start()
                return c

            lax.fori_loop(lo, hi, fill, 0)

            def drain(r, c):
                pltpu.make_async_copy(zero_ref.at[pl.ds(0, 1)], xs_hbm.at[pl.ds(0, 1)], sem).wait()
                return c

            lax.fori_loop(lo, hi, drain, 0)


def _dispatch(dest, tt, xr, n_rows, tm):
    T, width = xr.shape
    zero = jnp.zeros((META_ROWS, width), xr.dtype)
    return pl.pallas_call(
        functools.partial(_dispatch_kernel, tm=tm),
        grid_spec=pltpu.PrefetchScalarGridSpec(
            num_scalar_prefetch=2, grid=(T // tm,),
            in_specs=[pl.BlockSpec((tm, width), lambda i, d, t: (i, 0)),
                      pl.BlockSpec((META_ROWS, width), lambda i, d, t: (0, 0))],
            out_specs=pl.BlockSpec(memory_space=pl.ANY),
            scratch_shapes=[pltpu.SemaphoreType.DMA(())]),
        out_shape=jax.ShapeDtypeStruct((n_rows, width), xr.dtype),
        compiler_params=_cparams(("arbitrary",), 16),
        name="dispatch",
    )(dest, tt, xr, zero)


def _unsort_kernel(dest_ref, ys_hbm, y_ref, sem, *, tm):
    base = pl.program_id(0) * tm

    for r in range(tm):
        pltpu.make_async_copy(ys_hbm.at[pl.ds(dest_ref[base + r], 1)], y_ref.at[pl.ds(r, 1)],
                              sem).start(priority=r % 2)
    pltpu.make_async_copy(ys_hbm.at[pl.ds(0, tm)], y_ref, sem).wait()


def _unsort(dest, ys, T, tm):
    width = ys.shape[1]
    return pl.pallas_call(
        functools.partial(_unsort_kernel, tm=tm),
        grid_spec=pltpu.PrefetchScalarGridSpec(
            num_scalar_prefetch=1, grid=(T // tm,), in_specs=[pl.BlockSpec(memory_space=pl.ANY)],
            out_specs=pl.BlockSpec((tm, width), lambda i, d: (i, 0)),
            scratch_shapes=[pltpu.SemaphoreType.DMA(())]),
        out_shape=jax.ShapeDtypeStruct((T, width), ys.dtype),
        compiler_params=_cparams(("arbitrary",), 16),
        name="unsort",
    )(dest, ys)


def _experts_kernel(tt_ref, xs_ref, g_ref, w1_ref, w3_ref, w2_ref, o_ref):
    j = pl.program_id(0)

    @pl.when(j < tt_ref[TT_USED * LANES])
    def _():
        grp = tt_ref[TT_GROUP * LANES + j]
        x = xs_ref[:, :D_MODEL]
        cw = xs_ref[:, D_MODEL:]
        h = (x * lax.rsqrt(jnp.mean(x * x, axis=-1, keepdims=True) + EPS) * g_ref[...]).astype(BF16)
        lane = lax.broadcasted_iota(jnp.int32, cw.shape, 1)
        acc = x
        for e in range(EXPERTS_PER_GROUP):
            a = jnp.dot(h, w1_ref[0, e], preferred_element_type=F32)
            b = jnp.dot(h, w3_ref[0, e], preferred_element_type=F32)
            col = jnp.sum(jnp.where(lane == grp * EXPERTS_PER_GROUP + e, cw, 0.0), axis=-1, keepdims=True)
            hid = (a / (1.0 + jnp.exp(-a))) * b * col
            acc = acc + jnp.dot(hid.astype(BF16), w2_ref[0, e], preferred_element_type=F32)
        o_ref[...] = acc

    @pl.when(j >= tt_ref[TT_USED * LANES])
    def _():
        o_ref[...] = jnp.zeros_like(o_ref)


def _experts(tt, xs, g, w1, w3, w2, layer, tmx):
    n_rows = xs.shape[0]
    wspec = lambda a, b: pl.BlockSpec((1, EXPERTS_PER_GROUP, a, b),
                                      lambda j, tt: (layer, tt[TT_GROUP * LANES + j], 0, 0))
    return pl.pallas_call(
        _experts_kernel,
        grid_spec=pltpu.PrefetchScalarGridSpec(
            num_scalar_prefetch=1, grid=(n_rows // tmx,),
            in_specs=[pl.BlockSpec((tmx, ROW_EXT), lambda j, tt: (j, 0)),
                      pl.BlockSpec((1, D_MODEL), lambda j, tt: (0, 0)),
                      wspec(D_MODEL, D_EXPERT), wspec(D_MODEL, D_EXPERT), wspec(D_EXPERT, D_MODEL)],
            out_specs=pl.BlockSpec((tmx, D_MODEL), lambda j, tt: (j, 0))),
        out_shape=jax.ShapeDtypeStruct((n_rows, D_MODEL), F32),
        compiler_params=_cparams(("arbitrary",), 52),
        name="experts",
    )(tt, xs, g, w1, w3, w2)


def _half_swap(w):
    half = QK_ROPE // 2
    return jnp.concatenate([w[..., half:], w[..., :half]], axis=-1)


def _rope_swap(w):
    return jnp.concatenate([w, _half_swap(w)], axis=-1)


def _score_shift(q_norm, k_norm):
    gq = q_norm.astype(F32) * (QK_HEAD ** -0.5 * math.log2(math.e))
    gk = k_norm.astype(F32)
    c = (1.0 + 1.0 / 16.0) ** 2 * 1.01 * QK_HEAD * jnp.max(jnp.abs(gq)) * jnp.max(jnp.abs(gk))
    zeros = lambda n: jnp.zeros((n,), F32)
    tail = HEAD_SLAB - QK_HEAD
    gq_rows = jnp.stack([
        jnp.concatenate([gq, zeros(tail)]),
        jnp.where(jnp.arange(LANES) == SHIFT_LANE, -c, 0.0),
        jnp.concatenate([zeros(QK_NOPE), _half_swap(gq[QK_NOPE:]), zeros(tail)])])
    gkt = jnp.broadcast_to(jnp.concatenate([gk, _half_swap(gk[QK_NOPE:])])[:, None], (HEAD_SLAB, LANES))
    return dict(gq=gq_rows, gkt=gkt,
                safe=jnp.where(c < FP8_SHIFT_LIMIT, MODE_FP8_SCORES,
                               jnp.where(c < SAFE_SHIFT_LIMIT, MODE_BOUND_SHIFT, MODE_ROW_MAX)
                               ).astype(jnp.int32).reshape(1))


def _layer_params(l, attn_norm, w_in, q_a_norm, w_uq, kv_a_norm, w_ukv, q_norm, k_norm, sgu_ln_g, sgu_ln_b,
                  sgu_w, sgu_b, w_proj_attn, w_proj_sgu, w_proj_fnet, w_out, ffn_norm, w_group, b_group,
                  w_router, b_router):
    wi = w_in[l]
    c0 = Q_LORA + KV_LORA
    c1 = c0 + QK_ROPE
    w_ext = jnp.concatenate(
        [wi[:, :c0], jnp.zeros((D_MODEL, QK_NOPE), F32), _rope_swap(wi[:, c0:c1]), wi[:, c1:]], axis=1).astype(BF16)
    wq3 = w_uq[l].reshape(Q_LORA, N_HEADS, QK_HEAD)
    qz = lambda n: jnp.zeros((Q_LORA, N_HEADS, n), F32)
    wq = jnp.concatenate([wq3, qz(HEAD_SLAB - QK_HEAD)], axis=-1)
    wq_swap = jnp.concatenate([qz(QK_NOPE), _half_swap(wq3[..., QK_NOPE:]), qz(HEAD_SLAB - QK_HEAD)], axis=-1)
    wq = wq.reshape(Q_LORA, N_HEADS * HEAD_SLAB).astype(BF16)
    wq_swap = wq_swap.reshape(Q_LORA, N_HEADS * HEAD_SLAB).astype(BF16)
    wkv = w_ukv[l].reshape(KV_LORA, N_HEADS, QK_NOPE + V_HEAD)
    zpad = jnp.zeros((KV_LORA, N_HEADS, HEAD_SLAB - QK_NOPE), F32)
    wkt = jnp.concatenate([wkv[..., :QK_NOPE], zpad], axis=-1).reshape(KV_LORA, N_HEADS * HEAD_SLAB).T.astype(BF16)
    wv = jnp.concatenate([wkv[..., QK_NOPE:], zpad], axis=-1).reshape(KV_LORA, N_HEADS * HEAD_SLAB).astype(BF16)
    w_rg = jnp.concatenate([w_router[l], w_group[l], jnp.zeros((D_MODEL, LANES - N_EXPERTS - N_GROUPS), F32)], axis=1)
    w_rg_hi = w_rg.astype(BF16)
    w_rg_lo = (w_rg - w_rg_hi.astype(F32)).astype(BF16)
    b_rg = jnp.concatenate([b_router[l], b_group[l], jnp.zeros((LANES - N_EXPERTS - N_GROUPS,), F32)])[None, :]
    return dict(
        attn_norm=attn_norm[l][None, :], w_ext=w_ext,
        q_a_norm=q_a_norm[l][None, :], kv_a_norm=kv_a_norm[l][None, :],
        ln_g=sgu_ln_g[l][None, :], ln_b=sgu_ln_b[l][None, :],
        wq=wq, wq_swap=wq_swap, wkt=wkt, wv=wv, **_score_shift(q_norm[l], k_norm[l]),
        sgu_w=sgu_w[l].astype(BF16), sgu_bias=jnp.repeat(sgu_b[l].T, SGU_GROUP_DIM, axis=1),
        wpa=w_proj_attn[l].astype(BF16), wps=w_proj_sgu[l].astype(BF16), wpf=w_proj_fnet[l].astype(BF16),
        wo=w_out[l].astype(BF16),
        ffn_norm=ffn_norm[l][None, :], w_rg_hi=w_rg_hi, w_rg_lo=w_rg_lo, b_rg=b_rg,
    )


def _rope_slabs(positions):
    inv = 1.0 / (ROPE_THETA ** (jnp.arange(0, QK_ROPE, 2, dtype=F32) / QK_ROPE))
    ang = positions.astype(F32).reshape(-1)[:, None] * inv
    cos, sin = jnp.cos(ang), jnp.sin(ang)
    T = ang.shape[0]
    c2 = jnp.concatenate([cos, cos], axis=1)
    s2 = jnp.concatenate([-sin, sin], axis=1)
    tail = jnp.zeros((T, HEAD_SLAB - QK_HEAD), F32)
    return dict(ct=jnp.concatenate([jnp.ones((T, QK_NOPE), F32), c2, tail], axis=1),
                st=jnp.concatenate([jnp.zeros((T, QK_NOPE), F32), s2, tail], axis=1),
                ctt=c2.T, stt=s2.T)


def _dft_table_kernel(ch_ref, sh_ref, cl_ref, sl_ref, c_ref, s_ref):
    cl = cl_ref[...]
    sl = sl_ref[...]
    for hi in range(ch_ref.shape[1]):
        cols = slice(hi * LANES, (hi + 1) * LANES)
        ch = ch_ref[:, hi:hi + 1]
        sh = sh_ref[:, hi:hi + 1]
        c_ref[:, cols] = (ch * cl - sh * sl).astype(BF16)
        s_ref[:, cols] = (-(sh * cl + ch * sl)).astype(BF16)


def _dft_tables(n):
    assert n % LANES == 0
    n_hi = n // LANES
    kk = jnp.arange(n, dtype=jnp.int32)[:, None]
    ang_hi = ((kk * (jnp.arange(n_hi, dtype=jnp.int32) * LANES)[None, :]) & (n - 1)).astype(F32) * (2.0 * math.pi / n)
    ang_lo = ((kk * jnp.arange(LANES, dtype=jnp.int32)[None, :]) & (n - 1)).astype(F32) * (2.0 * math.pi / n)
    tr = _tile(n, 256)
    small = lambda w: pl.BlockSpec((tr, w), lambda i: (i, 0))
    c, sneg = pl.pallas_call(
        _dft_table_kernel,
        grid=(n // tr,),
        in_specs=[small(n_hi), small(n_hi), small(LANES), small(LANES)],
        out_specs=[small(n), small(n)],
        out_shape=[jax.ShapeDtypeStruct((n, n), BF16)] * 2,
        compiler_params=_cparams(("arbitrary",), 32),
        name="dft_tables",
    )(jnp.cos(ang_hi), jnp.sin(ang_hi), jnp.cos(ang_lo), jnp.sin(ang_lo))
    m = FNET_GROUP_DIM
    jk = np.outer(np.arange(m), np.arange(m)) % m
    angc = jk.astype(np.float64) * (2.0 * math.pi / m)
    cs = jnp.asarray(np.concatenate([np.cos(angc), np.sin(angc)], axis=1), F32).astype(BF16)
    return c, sneg, cs


def _tile(n, want):
    t = min(n, want)
    assert n % t == 0
    return t


def kernel(x, positions, attn_norm, w_in, q_a_norm, w_uq, kv_a_norm, w_ukv, q_norm, k_norm, sgu_ln_g, sgu_ln_b,
           sgu_w, sgu_b, w_proj_attn, w_proj_sgu, w_proj_fnet, w_out, ffn_norm, w_group, b_group, w_router,
           b_router, w1, w3, w2):
    batch, seq, _ = x.shape
    depth = w_in.shape[0]
    assert seq & (seq - 1) == 0 and seq % CHUNK == 0
    T = batch * seq
    tm = _tile(seq, 512)
    tmx = tm
    ridx = jnp.arange(tm, dtype=jnp.int32)
    utri = (ridx[:, None] < ridx[None, :]).astype(F32)
    rope = _rope_slabs(positions)
    dft_c, dft_sneg, cs = _dft_tables(seq)
    x2 = x.reshape(T, D_MODEL)
    w1b, w3b, w2b = w1.astype(BF16), w3.astype(BF16), w2.astype(BF16)
    for l in range(depth):
        p = _layer_params(l, attn_norm, w_in, q_a_norm, w_uq, kv_a_norm, w_ukv, q_norm, k_norm, sgu_ln_g,
                          sgu_ln_b, sgu_w, sgu_b, w_proj_attn, w_proj_sgu, w_proj_fnet, w_out, ffn_norm,
                          w_group, b_group, w_router, b_router)
        cqn, ckvn, ckvt, krt, u, v, fa, fb, gates = _inproj(
            x2, p["attn_norm"], p["w_ext"], p["q_a_norm"], p["kv_a_norm"], p["ln_g"], p["ln_b"], cs,
            batch, seq, _tile(seq, 256))
        q, k, vv = _mla_prep(cqn, ckvn, ckvt, krt, rope, p, batch, seq, tm)
        ya = _attention(p["safe"], q, k, vv, _tile(seq, 1024)).reshape(T, N_HEADS * V_HEAD)
        yf = _seq_dft(dft_c, dft_sneg, fa, fb, _tile(seq, 1024), _tile(batch * FNET_WIDTH, 1024), _tile(seq, 1024))
        xr, meta, cnt = _merge_route(
            x2, ya, u, v, yf, gates, p["sgu_w"], p["sgu_bias"], p["wpa"], p["wps"], p["wpf"], p["wo"],
            p["ffn_norm"], p["w_rg_hi"], p["w_rg_lo"], p["b_rg"], utri, seq, tm)
        dest, tt = _assign(cnt, meta, tmx)
        dest = dest.reshape(T)
        tt = tt.reshape(META_ROWS * LANES)
        xs = _dispatch(dest, tt, xr, T + N_GROUPS * tmx, tm)
        ys = _experts(tt, xs, p["ffn_norm"], w1b, w3b, w2b, l, tmx)
        x2 = _unsort(dest, ys, T, tm)
    return x2.reshape(batch, seq, D_MODEL)
```

```python
import functools
import math

import numpy as np
import jax
import jax.numpy as jnp
from jax import lax
from jax.experimental import pallas as pl
from jax.experimental.pallas import tpu as pltpu

F32 = jnp.float32
BF16 = jnp.bfloat16
F8 = jnp.float8_e4m3fn

D_MODEL = 1024
N_HEADS = 8
QK_NOPE = 64
QK_ROPE = 32
QK_HEAD = QK_NOPE + QK_ROPE
V_HEAD = 64
Q_LORA = 384
KV_LORA = 256
ROPE_THETA = 10000.0
CHUNK = 128
SGU_GROUPS = 8
SGU_GROUP_DIM = 64
SGU_WIDTH = SGU_GROUPS * SGU_GROUP_DIM
FNET_GROUPS = 4
FNET_GROUP_DIM = 128
FNET_WIDTH = FNET_GROUPS * FNET_GROUP_DIM
N_BRANCH = 3
N_GROUPS = 4
EXPERTS_PER_GROUP = 8
N_EXPERTS = N_GROUPS * EXPERTS_PER_GROUP
D_EXPERT = 256
EPS = 1e-6

LANES = 128
HEAD_SLAB = LANES
SHIFT_LANE = QK_HEAD
O_CQ = 0
O_CKV = O_CQ + Q_LORA
O_KR = O_CKV + KV_LORA
O_SGU = O_KR + HEAD_SLAB
O_FN = O_SGU + 2 * SGU_WIDTH
O_GATE = O_FN + FNET_WIDTH
N_IN_EXT = O_GATE + N_BRANCH * D_MODEL

V7X_VMEM_BYTES = 64 * 1024 * 1024


def _cparams(dims, vmem_mb):
    return pltpu.CompilerParams(dimension_semantics=dims,
                                vmem_limit_bytes=min(vmem_mb * 1024 * 1024, V7X_VMEM_BYTES - (6 << 20)))


def _const_spec(shape):
    nd = len(shape)
    return pl.BlockSpec(shape, lambda *_: (0,) * nd)


def _inproj_kernel(x_ref, g_ref, w_ref, qan_ref, kvan_ref, lng_ref, lnb_ref, cs_ref,
                   cq_ref, ckv_ref, ckvt_ref, krt_ref, u_ref, v_ref, fa_ref, fb_ref, gate_ref):
    x = x_ref[...]
    h = (x * lax.rsqrt(jnp.mean(x * x, axis=-1, keepdims=True) + EPS) * g_ref[...]).astype(BF16)

    def seg(a, n):
        return jnp.dot(h, w_ref[:, a:a + n], preferred_element_type=F32)

    cq = seg(O_CQ, Q_LORA)
    cq_ref[...] = (cq * lax.rsqrt(jnp.mean(cq * cq, axis=-1, keepdims=True) + EPS) * qan_ref[...]).astype(BF16)
    ckv = seg(O_CKV, KV_LORA)
    ckvn = ckv * lax.rsqrt(jnp.mean(ckv * ckv, axis=-1, keepdims=True) + EPS) * kvan_ref[...]
    ckv_ref[...] = ckvn.astype(BF16)
    ckvt_ref[...] = ckvn.T.astype(BF16)
    krt_ref[...] = seg(O_KR, HEAD_SLAB).T.astype(BF16)

    u = jax.nn.gelu(seg(O_SGU, SGU_WIDTH))
    u_ref[...] = u.astype(BF16)
    v = jax.nn.gelu(seg(O_SGU + SGU_WIDTH, SGU_WIDTH))
    mu = jnp.mean(v, axis=-1, keepdims=True)
    vc = v - mu
    var = jnp.mean(vc * vc, axis=-1, keepdims=True)
    v_ref[...] = (vc * lax.rsqrt(var + EPS) * lng_ref[...] + lnb_ref[...]).astype(BF16)

    zf = seg(O_FN, FNET_WIDTH).astype(BF16)
    for g in range(FNET_GROUPS):
        sl = slice(g * FNET_GROUP_DIM, (g + 1) * FNET_GROUP_DIM)
        ab = jnp.dot(zf[:, sl], cs_ref[...], preferred_element_type=F32)
        fa_ref[:, sl] = ab[:, :FNET_GROUP_DIM].astype(BF16)
        fb_ref[:, sl] = ab[:, FNET_GROUP_DIM:].astype(BF16)

    for j in range(N_BRANCH):
        gl = seg(O_GATE + j * D_MODEL, D_MODEL)
        gate_ref[:, j * D_MODEL:(j + 1) * D_MODEL] = (1.0 / (1.0 + jnp.exp(-gl))).astype(BF16)


def _inproj(x2, g, w_ext, qan, kvan, lng, lnb, cs, batch, seq, tm):
    T = x2.shape[0]
    n_s = seq // tm
    row = lambda n: pl.BlockSpec((tm, n), lambda i: (i, 0))
    sb = pl.BlockSpec((tm, FNET_WIDTH), lambda i: (i % n_s, i // n_s))
    col = lambda n: pl.BlockSpec((n, tm), lambda i: (0, i))
    outs = [
        jax.ShapeDtypeStruct((T, Q_LORA), BF16),
        jax.ShapeDtypeStruct((T, KV_LORA), BF16),
        jax.ShapeDtypeStruct((KV_LORA, T), BF16),
        jax.ShapeDtypeStruct((HEAD_SLAB, T), BF16),
        jax.ShapeDtypeStruct((T, SGU_WIDTH), BF16),
        jax.ShapeDtypeStruct((T, SGU_WIDTH), BF16),
        jax.ShapeDtypeStruct((seq, batch * FNET_WIDTH), BF16),
        jax.ShapeDtypeStruct((seq, batch * FNET_WIDTH), BF16),
        jax.ShapeDtypeStruct((T, N_BRANCH * D_MODEL), BF16),
    ]
    return pl.pallas_call(
        _inproj_kernel,
        grid=(T // tm,),
        in_specs=[row(D_MODEL), _const_spec((1, D_MODEL)), _const_spec((D_MODEL, N_IN_EXT)),
                  _const_spec((1, Q_LORA)), _const_spec((1, KV_LORA)),
                  _const_spec((1, SGU_WIDTH)), _const_spec((1, SGU_WIDTH)),
                  _const_spec((FNET_GROUP_DIM, 2 * FNET_GROUP_DIM))],
        out_specs=[row(Q_LORA), row(KV_LORA), col(KV_LORA), col(HEAD_SLAB), row(SGU_WIDTH), row(SGU_WIDTH),
                   sb, sb, row(N_BRANCH * D_MODEL)],
        out_shape=outs,
        compiler_params=_cparams(("arbitrary",), 52),
        name="inproj",
    )(x2, g, w_ext, qan, kvan, lng, lnb, cs)


ROPE_ROWS = slice(QK_NOPE, QK_HEAD)
SWAP_ROWS = slice(QK_HEAD, HEAD_SLAB)


def _mla_prep_kernel(cq_ref, ckv_ref, ckvt_ref, krt_ref, ct_ref, st_ref, ctt_ref, stt_ref,
                     wq_ref, wqs_ref, wkt_ref, wv_ref, gq_ref, gkt_ref, q_ref, k_ref, v_ref):
    cq = cq_ref[...]
    tm = cq.shape[0]
    reps = tm // LANES
    q_all = jnp.dot(cq, wq_ref[...], preferred_element_type=F32)
    q_swp = jnp.dot(cq, wqs_ref[...], preferred_element_type=F32)
    a_q = gq_ref[0:1, :] * ct_ref[...]
    b_q = gq_ref[2:3, :] * st_ref[...]
    for h in range(N_HEADS):
        sl = slice(h * HEAD_SLAB, (h + 1) * HEAD_SLAB)
        y = q_all[:, sl]
        r = lax.rsqrt(jnp.sum(y * y, axis=-1, keepdims=True) * (1.0 / QK_HEAD) + EPS)
        q_ref[0, h] = ((y * a_q + q_swp[:, sl] * b_q) * r + gq_ref[1:2, :]).astype(BF16)

    kt_all = jnp.dot(wkt_ref[...], ckvt_ref[...], preferred_element_type=F32)
    krt = krt_ref[...].astype(F32)
    gains = jnp.concatenate([gkt_ref[...]] * reps, axis=1)
    cos_t = ctt_ref[...]
    sin_t = stt_ref[...]
    row = lax.broadcasted_iota(jnp.int32, (HEAD_SLAB - QK_HEAD, tm), 0)
    tail = jnp.where(row == 0, 1.0, 0.0)
    for h in range(N_HEADS):
        y = kt_all[h * HEAD_SLAB:(h + 1) * HEAD_SLAB, :] + krt
        head = y[0:QK_HEAD, :]
        r = lax.rsqrt(jnp.sum(head * head, axis=0, keepdims=True) * (1.0 / QK_HEAD) + EPS)
        yg = y * gains * r
        rope = yg[ROPE_ROWS, :] * cos_t + yg[SWAP_ROWS, :] * sin_t
        k_ref[0, h] = jnp.concatenate([yg[0:QK_NOPE, :], rope, tail], axis=0).astype(BF16)

    v_all = jnp.dot(ckv_ref[...], wv_ref[...], preferred_element_type=F32)
    is_v = lax.broadcasted_iota(jnp.int32, (tm, HEAD_SLAB), 1) < V_HEAD
    for h in range(N_HEADS):
        v_ref[0, h] = jnp.where(is_v, v_all[:, h * HEAD_SLAB:(h + 1) * HEAD_SLAB], 1.0).astype(BF16)


def _mla_prep(cqn, ckvn, ckvt, krt, rope, p, batch, seq, tm):
    T = cqn.shape[0]
    n_s = seq // tm
    row = lambda n: pl.BlockSpec((tm, n), lambda i: (i, 0))
    col = lambda n: pl.BlockSpec((n, tm), lambda i: (0, i))
    hb = pl.BlockSpec((1, N_HEADS, tm, HEAD_SLAB), lambda i: (i // n_s, 0, i % n_s, 0))
    shp = jax.ShapeDtypeStruct((batch, N_HEADS, seq, HEAD_SLAB), BF16)
    wide = N_HEADS * HEAD_SLAB
    return pl.pallas_call(
        _mla_prep_kernel,
        grid=(T // tm,),
        in_specs=[row(Q_LORA), row(KV_LORA), col(KV_LORA), col(HEAD_SLAB),
                  row(HEAD_SLAB), row(HEAD_SLAB), col(QK_ROPE), col(QK_ROPE),
                  _const_spec((Q_LORA, wide)), _const_spec((Q_LORA, wide)), _const_spec((wide, KV_LORA)),
                  _const_spec((KV_LORA, wide)), _const_spec((3, HEAD_SLAB)), _const_spec((HEAD_SLAB, LANES))],
        out_specs=[hb, pl.BlockSpec((1, N_HEADS, HEAD_SLAB, tm), lambda i: (i // n_s, 0, 0, i % n_s)), hb],
        out_shape=[shp, jax.ShapeDtypeStruct((batch, N_HEADS, HEAD_SLAB, seq), BF16), shp],
        compiler_params=_cparams(("arbitrary",), 40),
        name="mla_prep",
    )(cqn, ckvn, ckvt, krt, rope["ct"], rope["st"], rope["ctt"], rope["stt"],
      p["wq"], p["wq_swap"], p["wkt"], p["wv"], p["gq"], p["gkt"])


HEADS_PER_STEP = LANES // V_HEAD


ROW_CHAINS = 2
SAFE_SHIFT_LIMIT = 60.0
FP8_SHIFT_LIMIT = 32.0
MODE_ROW_MAX, MODE_BOUND_SHIFT, MODE_FP8_SCORES = 0, 1, 2


def _attn_kernel(safe_ref, q_ref, kt_ref, v_ref, o_ref):
    tq = q_ref.shape[2]
    rows_per_chain = tq // ROW_CHAINS
    lane = lax.broadcasted_iota(jnp.int32, (rows_per_chain, LANES), 1)
    low = lane < V_HEAD

    def finish(rows, outs):
        outs = [o / jnp.where(low, pltpu.roll(o, V_HEAD, 1), 1.0) for o in outs]
        o_ref[0, rows, :] = jnp.where(low, outs[0], pltpu.roll(outs[1], V_HEAD, 1)).astype(BF16)

    def bound_shifted(dtype):
        kts = [kt_ref[0, j].astype(dtype) for j in range(HEADS_PER_STEP)]
        for r in range(ROW_CHAINS):
            rows = slice(r * rows_per_chain, (r + 1) * rows_per_chain)
            ps = [jnp.exp2(jnp.dot(q_ref[0, j, rows, :].astype(dtype), kts[j],
                                   preferred_element_type=F32).astype(BF16)) for j in range(HEADS_PER_STEP)]
            finish(rows, [jnp.dot(ps[j], v_ref[0, j], preferred_element_type=F32) for j in range(HEADS_PER_STEP)])

    @pl.when(safe_ref[0] == MODE_FP8_SCORES)
    def _():
        bound_shifted(F8)

    @pl.when(safe_ref[0] == MODE_BOUND_SHIFT)
    def _():
        bound_shifted(BF16)

    @pl.when(safe_ref[0] == MODE_ROW_MAX)
    def _():
        for r in range(ROW_CHAINS):
            rows = slice(r * rows_per_chain, (r + 1) * rows_per_chain)
            ss = [jnp.dot(q_ref[0, j, rows, :], kt_ref[0, j], preferred_element_type=F32)
                  for j in range(HEADS_PER_STEP)]
            ps = [jnp.exp2((s - jnp.max(s, axis=-1, keepdims=True)).astype(BF16)) for s in ss]
            finish(rows, [jnp.dot(ps[j], v_ref[0, j], preferred_element_type=F32) for j in range(HEADS_PER_STEP)])


def _attention(safe, q, k, v, tq):
    batch, _, seq, _ = q.shape
    n_hp = N_HEADS // HEADS_PER_STEP
    return pl.pallas_call(
        _attn_kernel,
        grid_spec=pltpu.PrefetchScalarGridSpec(
            num_scalar_prefetch=1, grid=(batch, n_hp, seq // tq),
            in_specs=[pl.BlockSpec((1, HEADS_PER_STEP, tq, HEAD_SLAB), lambda b, h, i, f: (b, h, i, 0)),
                      pl.BlockSpec((1, HEADS_PER_STEP, HEAD_SLAB, seq), lambda b, h, i, f: (b, h, 0, 0)),
                      pl.BlockSpec((1, HEADS_PER_STEP, seq, HEAD_SLAB), lambda b, h, i, f: (b, h, 0, 0))],
            out_specs=pl.BlockSpec((1, tq, LANES), lambda b, h, i, f: (b, i, h))),
        out_shape=jax.ShapeDtypeStruct((batch, seq, N_HEADS * V_HEAD), BF16),
        compiler_params=_cparams(("arbitrary", "arbitrary", "arbitrary"), 52),
        name="attention",
    )(safe, q, k, v)


def _dft_kernel(c_ref, s_ref, a_ref, b_ref, o_ref, acc_ref, *, scale):
    kk = pl.program_id(2)

    @pl.when(kk == 0)
    def _():
        acc_ref[...] = jnp.zeros_like(acc_ref)

    acc_ref[...] += (jnp.dot(c_ref[...], a_ref[...], preferred_element_type=F32)
                     + jnp.dot(s_ref[...], b_ref[...], preferred_element_type=F32))

    @pl.when(kk == pl.num_programs(2) - 1)
    def _():
        o_ref[...] = (acc_ref[...] * scale).astype(BF16)


def _seq_dft(ctab, sneg, fa, fb, tm, tn, tk):
    seq, width = fa.shape
    scale = 1.0 / math.sqrt(seq * FNET_GROUP_DIM)
    return pl.pallas_call(
        functools.partial(_dft_kernel, scale=scale),
        grid=(seq // tm, width // tn, seq // tk),
        in_specs=[pl.BlockSpec((tm, tk), lambda i, j, k: (i, k)),
                  pl.BlockSpec((tm, tk), lambda i, j, k: (i, k)),
                  pl.BlockSpec((tk, tn), lambda i, j, k: (k, j)),
                  pl.BlockSpec((tk, tn), lambda i, j, k: (k, j))],
        out_specs=pl.BlockSpec((tm, tn), lambda i, j, k: (i, j)),
        out_shape=jax.ShapeDtypeStruct((seq, width), BF16),
        scratch_shapes=[pltpu.VMEM((tm, tn), F32)],
        compiler_params=_cparams(("arbitrary", "arbitrary", "arbitrary"), 40),
        name="seq_dft",
    )(ctab, sneg, fa, fb)


def _merge_kernel(x_ref, ya_ref, u_ref, v_ref, yf_ref, gate_ref, ws_ref, bias_ref,
                  wpa_ref, wps_ref, wpf_ref, wo_ref, gn_ref, wrg_ref, brg_ref, utri_ref,
                  xr_ref, meta_ref, cnt_ref, ysgu_ref):
    tm = x_ref.shape[0]
    lane = lax.broadcasted_iota(jnp.int32, (CHUNK, LANES), 1)
    low = lane < SGU_GROUP_DIM
    groups_per_slab = LANES // SGU_GROUP_DIM
    for c in range(tm // CHUNK):
        rows = slice(c * CHUNK, (c + 1) * CHUNK)
        for j in range(SGU_WIDTH // LANES):
            cols = slice(j * LANES, (j + 1) * LANES)
            vb = v_ref[rows, cols]
            ma = jnp.dot(ws_ref[groups_per_slab * j], vb, preferred_element_type=F32)
            mb = jnp.dot(ws_ref[groups_per_slab * j + 1], vb, preferred_element_type=F32)
            mixed = jnp.where(low, ma, mb) + bias_ref[:, cols]
            ysgu_ref[rows, cols] = (u_ref[rows, cols].astype(F32) * mixed).astype(BF16)
    pa = jnp.dot(ya_ref[...], wpa_ref[...], preferred_element_type=F32)
    merged = gate_ref[:, 0:D_MODEL].astype(F32) * pa
    ps = jnp.dot(ysgu_ref[...], wps_ref[...], preferred_element_type=F32)
    merged += gate_ref[:, D_MODEL:2 * D_MODEL].astype(F32) * ps
    pf = jnp.dot(yf_ref[...], wpf_ref[...], preferred_element_type=F32)
    merged += gate_ref[:, 2 * D_MODEL:3 * D_MODEL].astype(F32) * pf
    x_mid = x_ref[...] + jnp.dot(merged.astype(BF16), wo_ref[...], preferred_element_type=F32)
    _route_tile(x_mid, gn_ref, wrg_ref, brg_ref, utri_ref, xr_ref, meta_ref, cnt_ref)


def _merge_route(x2, ya, u, v, yf, gates, ws, bias, wpa, wps, wpf, wo, gn, wrg, brg, utri, seq, tm):
    T = x2.shape[0]
    n_s = seq // tm
    n_tiles = T // tm
    row = lambda n: pl.BlockSpec((tm, n), lambda i: (i, 0))
    return pl.pallas_call(
        _merge_kernel,
        grid=(n_tiles,),
        in_specs=[row(D_MODEL), row(N_HEADS * V_HEAD), row(SGU_WIDTH), row(SGU_WIDTH),
                  pl.BlockSpec((tm, FNET_WIDTH), lambda i: (i % n_s, i // n_s)),
                  row(N_BRANCH * D_MODEL),
                  _const_spec((SGU_GROUPS, CHUNK, CHUNK)), _const_spec((CHUNK, SGU_WIDTH)),
                  _const_spec((N_HEADS * V_HEAD, D_MODEL)), _const_spec((SGU_WIDTH, D_MODEL)),
                  _const_spec((FNET_WIDTH, D_MODEL)), _const_spec((D_MODEL, D_MODEL)),
                  _const_spec((1, D_MODEL)), _const_spec((D_MODEL, 2 * LANES)),
                  _const_spec((1, LANES)), _const_spec((tm, tm))],
        out_specs=[row(ROW_EXT), pl.BlockSpec((1, META_ROWS, tm), lambda i: (i, 0, 0)),
                   pl.BlockSpec((1, META_ROWS, LANES), lambda i: (i, 0, 0))],
        out_shape=[jax.ShapeDtypeStruct((T, ROW_EXT), F32),
                   jax.ShapeDtypeStruct((n_tiles, META_ROWS, tm), jnp.int32),
                   jax.ShapeDtypeStruct((n_tiles, META_ROWS, LANES), jnp.int32)],
        scratch_shapes=[pltpu.VMEM((tm, SGU_WIDTH), BF16)],
        compiler_params=_cparams(("arbitrary",), 52),
        name="merge_route",
    )(x2, ya, u, v, yf, gates, ws, bias, wpa, wps, wpf, wo, gn, wrg, brg, utri)


O_GLOGIT = N_EXPERTS


META_ROWS = 8
ROW_EXT = D_MODEL + LANES


def _route_tile(x, g_ref, wrg_ref, b_ref, utri_ref, xr_ref, meta_ref, cnt_ref):
    tm = x.shape[0]
    h = x * lax.rsqrt(jnp.mean(x * x, axis=-1, keepdims=True) + EPS) * g_ref[...]
    hhi = h.astype(BF16)
    hlo = (h - hhi.astype(F32)).astype(BF16)
    both = jnp.dot(hhi, wrg_ref[...], preferred_element_type=F32)
    logits = (both[:, :LANES] + jnp.dot(hlo, wrg_ref[:, :LANES], preferred_element_type=F32)
              + both[:, LANES:]) + b_ref[...]
    lane = lax.broadcasted_iota(jnp.int32, logits.shape, 1)
    lane_f = lane.astype(F32)
    neg = -jnp.inf
    far = float(2 * LANES)

    is_g = (lane >= O_GLOGIT) & (lane < O_GLOGIT + N_GROUPS)
    gl = jnp.where(is_g, logits, neg)
    gmax = jnp.max(gl, axis=-1, keepdims=True)
    gsel = jnp.min(jnp.where(gl == gmax, lane_f, far), axis=-1, keepdims=True) - float(O_GLOGIT)
    gw = 1.0 / jnp.sum(jnp.where(is_g, jnp.exp(gl - gmax), 0.0), axis=-1, keepdims=True)

    lane_grp = lax.shift_right_logical(lane, 3).astype(F32)
    in_grp = (lane < N_EXPERTS) & (lane_grp == gsel)
    el = jnp.where(in_grp, logits, neg)
    e1 = jnp.max(el, axis=-1, keepdims=True)
    i1 = jnp.min(jnp.where(el == e1, lane_f, far), axis=-1, keepdims=True)
    el2 = jnp.where(lane_f == i1, neg, el)
    e2 = jnp.max(el2, axis=-1, keepdims=True)
    i2 = jnp.min(jnp.where(el2 == e2, lane_f, far), axis=-1, keepdims=True)
    d = jnp.exp(e2 - e1)
    w1 = gw / (1.0 + d)
    w2 = w1 * d
    xr_ref[:, :D_MODEL] = x
    xr_ref[:, D_MODEL:] = jnp.where(lane_f == i1, w1, jnp.where(lane_f == i2, w2, 0.0))

    onehot_t = jnp.where((lane_f == gsel) & (lane < N_GROUPS), 1.0, 0.0).T[0:META_ROWS, :]
    before = jnp.dot(onehot_t, utri_ref[...], preferred_element_type=F32)
    gid = lax.broadcasted_iota(jnp.int32, (META_ROWS, tm), 0).astype(F32)
    gsel_t = jnp.sum(onehot_t * gid, axis=0, keepdims=True)
    rank_t = jnp.sum(onehot_t * before, axis=0, keepdims=True)
    row_id = lax.broadcasted_iota(jnp.int32, (META_ROWS, tm), 0)
    meta_ref[0] = jnp.where(row_id == 0, gsel_t, jnp.where(row_id == 1, rank_t, 0.0)).astype(jnp.int32)
    cnt = jnp.sum(onehot_t, axis=1, keepdims=True)
    cnt_ref[0] = jnp.broadcast_to(cnt, (META_ROWS, LANES)).astype(jnp.int32)


TT_GROUP, TT_USED, TT_PAD_LO, TT_PAD_HI = 0, 1, 2, 3


def _assign_kernel(cnt_ref, meta_ref, dest_ref, tt_ref, *, tmx):
    n_tiles, _, tm = meta_ref.shape
    shift = tmx.bit_length() - 1
    cnt = cnt_ref[...]
    tot = jnp.sum(cnt, axis=0)
    padded = lax.shift_left(lax.shift_right_logical(tot + (tmx - 1), shift), shift)
    starts, ends = [], []
    run = jnp.zeros((1, LANES), jnp.int32)
    for g in range(N_GROUPS):
        starts.append(run)
        run = run + padded[g:g + 1, :]
        ends.append(run)

    def widen(v):
        return jnp.concatenate([v] * (tm // LANES), axis=1)

    def body(i, seen):
        gs = meta_ref[i, 0:1, :]
        dest = meta_ref[i, 1:2, :]
        for g in range(N_GROUPS):
            dest = dest + jnp.where(gs == g, widen(starts[g] + seen[g]), 0)
        dest_ref[i] = dest
        c = cnt_ref[i]
        return tuple(seen[g] + c[g:g + 1, :] for g in range(N_GROUPS))

    lax.fori_loop(0, n_tiles, body, tuple(jnp.zeros((1, LANES), jnp.int32) for _ in range(N_GROUPS)))

    lane = lax.broadcasted_iota(jnp.int32, (1, LANES), 1)
    first_row = lane * tmx
    grp = jnp.zeros((1, LANES), jnp.int32)
    for g in range(N_GROUPS - 1):
        grp = grp + jnp.where(first_row >= ends[g], 1, 0)
    used = lax.shift_right_logical(ends[-1], shift)
    pad_lo = jnp.zeros((1, LANES), jnp.int32)
    pad_hi = jnp.zeros((1, LANES), jnp.int32)
    for g in range(N_GROUPS):
        pad_lo = pad_lo + jnp.where(lane == g, starts[g] + tot[g:g + 1, :], 0)
        pad_hi = pad_hi + jnp.where(lane == g, ends[g], 0)
    n_rows = (n_tiles * tm // tmx + N_GROUPS) * tmx
    pad_lo = pad_lo + jnp.where(lane == N_GROUPS, ends[-1], 0)
    pad_hi = pad_hi + jnp.where(lane == N_GROUPS, n_rows, 0)
    row_id = lax.broadcasted_iota(jnp.int32, (META_ROWS, LANES), 0)
    tt_ref[...] = jnp.where(row_id == TT_GROUP, grp,
                            jnp.where(row_id == TT_USED, used,
                                      jnp.where(row_id == TT_PAD_LO, pad_lo,
                                                jnp.where(row_id == TT_PAD_HI, pad_hi, 0))))


def _assign(cnt, meta, tmx):
    n_tiles, _, tm = meta.shape
    return pl.pallas_call(
        functools.partial(_assign_kernel, tmx=tmx),
        grid=(1,),
        in_specs=[_const_spec(cnt.shape), _const_spec(meta.shape)],
        out_specs=[_const_spec((n_tiles, 1, tm)), _const_spec((META_ROWS, LANES))],
        out_shape=[jax.ShapeDtypeStruct((n_tiles, 1, tm), jnp.int32),
                   jax.ShapeDtypeStruct((META_ROWS, LANES), jnp.int32)],
        compiler_params=_cparams(("arbitrary",), 32),
        name="assign",
    )(cnt, meta)


def _dispatch_kernel(dest_ref, tt_ref, xr_ref, zero_ref, xs_hbm, sem, *, tm):
    base = pl.program_id(0) * tm

    for r in range(tm):
        pltpu.make_async_copy(xr_ref.at[pl.ds(r, 1)], xs_hbm.at[pl.ds(dest_ref[base + r], 1)],
                              sem).start(priority=r % 2)
    pltpu.make_async_copy(xr_ref, xs_hbm.at[pl.ds(0, tm)], sem).wait()

    @pl.when(pl.program_id(0) == 0)
    def _():
        for g in range(N_GROUPS + 1):
            lo = tt_ref[TT_PAD_LO * LANES + g]
            hi = tt_ref[TT_PAD_HI * LANES + g]

            def fill(r, c):
                pltpu.make_async_copy(zero_ref.at[pl.ds(0, 1)], xs_hbm.at[pl.ds(r, 1)], sem).start()
                return c

            lax.fori_loop(lo, hi, fill, 0)

            def drain(r, c):
                pltpu.make_async_copy(zero_ref.at[pl.ds(0, 1)], xs_hbm.at[pl.ds(0, 1)], sem).wait()
                return c

            lax.fori_loop(lo, hi, drain, 0)


def _dispatch(dest, tt, xr, n_rows, tm):
    T, width = xr.shape
    zero = jnp.zeros((META_ROWS, width), xr.dtype)
    return pl.pallas_call(
        functools.partial(_dispatch_kernel, tm=tm),
        grid_spec=pltpu.PrefetchScalarGridSpec(
            num_scalar_prefetch=2, grid=(T // tm,),
            in_specs=[pl.BlockSpec((tm, width), lambda i, d, t: (i, 0)),
                      pl.BlockSpec((META_ROWS, width), lambda i, d, t: (0, 0))],
            out_specs=pl.BlockSpec(memory_space=pl.ANY),
            scratch_shapes=[pltpu.SemaphoreType.DMA(())]),
        out_shape=jax.ShapeDtypeStruct((n_rows, width), xr.dtype),
        compiler_params=_cparams(("arbitrary",), 16),
        name="dispatch",
    )(dest, tt, xr, zero)


def _unsort_kernel(dest_ref, ys_hbm, y_ref, sem, *, tm):
    base = pl.program_id(0) * tm

    for r in range(tm):
        pltpu.make_async_copy(ys_hbm.at[pl.ds(dest_ref[base + r], 1)], y_ref.at[pl.ds(r, 1)],
                              sem).start(priority=r % 2)
    pltpu.make_async_copy(ys_hbm.at[pl.ds(0, tm)], y_ref, sem).wait()


def _unsort(dest, ys, T, tm):
    width = ys.shape[1]
    return pl.pallas_call(
        functools.partial(_unsort_kernel, tm=tm),
        grid_spec=pltpu.PrefetchScalarGridSpec(
            num_scalar_prefetch=1, grid=(T // tm,), in_specs=[pl.BlockSpec(memory_space=pl.ANY)],
            out_specs=pl.BlockSpec((tm, width), lambda i, d: (i, 0)),
            scratch_shapes=[pltpu.SemaphoreType.DMA(())]),
        out_shape=jax.ShapeDtypeStruct((T, width), ys.dtype),
        compiler_params=_cparams(("arbitrary",), 16),
        name="unsort",
    )(dest, ys)


def _experts_kernel(tt_ref, xs_ref, g_ref, w1_ref, w3_ref, w2_ref, o_ref):
    j = pl.program_id(0)

    @pl.when(j < tt_ref[TT_USED * LANES])
    def _():
        grp = tt_ref[TT_GROUP * LANES + j]
        x = xs_ref[:, :D_MODEL]
        cw = xs_ref[:, D_MODEL:]
        h = (x * lax.rsqrt(jnp.mean(x * x, axis=-1, keepdims=True) + EPS) * g_ref[...]).astype(BF16)
        lane = lax.broadcasted_iota(jnp.int32, cw.shape, 1)
        acc = x
        for e in range(EXPERTS_PER_GROUP):
            a = jnp.dot(h, w1_ref[0, e], preferred_element_type=F32)
            b = jnp.dot(h, w3_ref[0, e], preferred_element_type=F32)
            col = jnp.sum(jnp.where(lane == grp * EXPERTS_PER_GROUP + e, cw, 0.0), axis=-1, keepdims=True)
            hid = (a / (1.0 + jnp.exp(-a))) * b * col
            acc = acc + jnp.dot(hid.astype(BF16), w2_ref[0, e], preferred_element_type=F32)
        o_ref[...] = acc

    @pl.when(j >= tt_ref[TT_USED * LANES])
    def _():
        o_ref[...] = jnp.zeros_like(o_ref)


def _experts(tt, xs, g, w1, w3, w2, layer, tmx):
    n_rows = xs.shape[0]
    wspec = lambda a, b: pl.BlockSpec((1, EXPERTS_PER_GROUP, a, b),
                                      lambda j, tt: (layer, tt[TT_GROUP * LANES + j], 0, 0))
    return pl.pallas_call(
        _experts_kernel,
        grid_spec=pltpu.PrefetchScalarGridSpec(
            num_scalar_prefetch=1, grid=(n_rows // tmx,),
            in_specs=[pl.BlockSpec((tmx, ROW_EXT), lambda j, tt: (j, 0)),
                      pl.BlockSpec((1, D_MODEL), lambda j, tt: (0, 0)),
                      wspec(D_MODEL, D_EXPERT), wspec(D_MODEL, D_EXPERT), wspec(D_EXPERT, D_MODEL)],
            out_specs=pl.BlockSpec((tmx, D_MODEL), lambda j, tt: (j, 0))),
        out_shape=jax.ShapeDtypeStruct((n_rows, D_MODEL), F32),
        compiler_params=_cparams(("arbitrary",), 52),
        name="experts",
    )(tt, xs, g, w1, w3, w2)


def _half_swap(w):
    half = QK_ROPE // 2
    return jnp.concatenate([w[..., half:], w[..., :half]], axis=-1)


def _rope_swap(w):
    return jnp.concatenate([w, _half_swap(w)], axis=-1)


def _score_shift(q_norm, k_norm):
    gq = q_norm.astype(F32) * (QK_HEAD ** -0.5 * math.log2(math.e))
    gk = k_norm.astype(F32)
    c = (1.0 + 1.0 / 16.0) ** 2 * 1.01 * QK_HEAD * jnp.max(jnp.abs(gq)) * jnp.max(jnp.abs(gk))
    zeros = lambda n: jnp.zeros((n,), F32)
    tail = HEAD_SLAB - QK_HEAD
    gq_rows = jnp.stack([
        jnp.concatenate([gq, zeros(tail)]),
        jnp.where(jnp.arange(LANES) == SHIFT_LANE, -c, 0.0),
        jnp.concatenate([zeros(QK_NOPE), _half_swap(gq[QK_NOPE:]), zeros(tail)])])
    gkt = jnp.broadcast_to(jnp.concatenate([gk, _half_swap(gk[QK_NOPE:])])[:, None], (HEAD_SLAB, LANES))
    return dict(gq=gq_rows, gkt=gkt,
                safe=jnp.where(c < FP8_SHIFT_LIMIT, MODE_FP8_SCORES,
                               jnp.where(c < SAFE_SHIFT_LIMIT, MODE_BOUND_SHIFT, MODE_ROW_MAX)
                               ).astype(jnp.int32).reshape(1))


def _layer_params(l, attn_norm, w_in, q_a_norm, w_uq, kv_a_norm, w_ukv, q_norm, k_norm, sgu_ln_g, sgu_ln_b,
                  sgu_w, sgu_b, w_proj_attn, w_proj_sgu, w_proj_fnet, w_out, ffn_norm, w_group, b_group,
                  w_router, b_router):
    wi = w_in[l]
    c0 = Q_LORA + KV_LORA
    c1 = c0 + QK_ROPE
    w_ext = jnp.concatenate(
        [wi[:, :c0], jnp.zeros((D_MODEL, QK_NOPE), F32), _rope_swap(wi[:, c0:c1]), wi[:, c1:]], axis=1).astype(BF16)
    wq3 = w_uq[l].reshape(Q_LORA, N_HEADS, QK_HEAD)
    qz = lambda n: jnp.zeros((Q_LORA, N_HEADS, n), F32)
    wq = jnp.concatenate([wq3, qz(HEAD_SLAB - QK_HEAD)], axis=-1)
    wq_swap = jnp.concatenate([qz(QK_NOPE), _half_swap(wq3[..., QK_NOPE:]), qz(HEAD_SLAB - QK_HEAD)], axis=-1)
    wq = wq.reshape(Q_LORA, N_HEADS * HEAD_SLAB).astype(BF16)
    wq_swap = wq_swap.reshape(Q_LORA, N_HEADS * HEAD_SLAB).astype(BF16)
    wkv = w_ukv[l].reshape(KV_LORA, N_HEADS, QK_NOPE + V_HEAD)
    zpad = jnp.zeros((KV_LORA, N_HEADS, HEAD_SLAB - QK_NOPE), F32)
    wkt = jnp.concatenate([wkv[..., :QK_NOPE], zpad], axis=-1).reshape(KV_LORA, N_HEADS * HEAD_SLAB).T.astype(BF16)
    wv = jnp.concatenate([wkv[..., QK_NOPE:], zpad], axis=-1).reshape(KV_LORA, N_HEADS * HEAD_SLAB).astype(BF16)
    w_rg = jnp.concatenate([w_router[l], w_group[l], jnp.zeros((D_MODEL, LANES - N_EXPERTS - N_GROUPS), F32)], axis=1)
    w_rg_hi = w_rg.astype(BF16)
    w_rg_lo = (w_rg - w_rg_hi.astype(F32)).astype(BF16)
    b_rg = jnp.concatenate([b_router[l], b_group[l], jnp.zeros((LANES - N_EXPERTS - N_GROUPS,), F32)])[None, :]
    return dict(
        attn_norm=attn_norm[l][None, :], w_ext=w_ext,
        q_a_norm=q_a_norm[l][None, :], kv_a_norm=kv_a_norm[l][None, :],
        ln_g=sgu_ln_g[l][None, :], ln_b=sgu_ln_b[l][None, :],
        wq=wq, wq_swap=wq_swap, wkt=wkt, wv=wv, **_score_shift(q_norm[l], k_norm[l]),
        sgu_w=sgu_w[l].astype(BF16), sgu_bias=jnp.repeat(sgu_b[l].T, SGU_GROUP_DIM, axis=1),
        wpa=w_proj_attn[l].astype(BF16), wps=w_proj_sgu[l].astype(BF16), wpf=w_proj_fnet[l].astype(BF16),
        wo=w_out[l].astype(BF16),
        ffn_norm=ffn_norm[l][None, :], w_rg=jnp.concatenate([w_rg_hi, w_rg_lo], axis=1), b_rg=b_rg,
    )


def _rope_slabs(positions):
    inv = 1.0 / (ROPE_THETA ** (jnp.arange(0, QK_ROPE, 2, dtype=F32) / QK_ROPE))
    ang = positions.astype(F32).reshape(-1)[:, None] * inv
    cos, sin = jnp.cos(ang), jnp.sin(ang)
    T = ang.shape[0]
    c2 = jnp.concatenate([cos, cos], axis=1)
    s2 = jnp.concatenate([-sin, sin], axis=1)
    tail = jnp.zeros((T, HEAD_SLAB - QK_HEAD), F32)
    return dict(ct=jnp.concatenate([jnp.ones((T, QK_NOPE), F32), c2, tail], axis=1),
                st=jnp.concatenate([jnp.zeros((T, QK_NOPE), F32), s2, tail], axis=1),
                ctt=c2.T, stt=s2.T)


def _dft_table_kernel(ch_ref, sh_ref, cl_ref, sl_ref, c_ref, s_ref):
    cl = cl_ref[...]
    sl = sl_ref[...]
    for hi in range(ch_ref.shape[1]):
        cols = slice(hi * LANES, (hi + 1) * LANES)
        ch = ch_ref[:, hi:hi + 1]
        sh = sh_ref[:, hi:hi + 1]
        c_ref[:, cols] = (ch * cl - sh * sl).astype(BF16)
        s_ref[:, cols] = (-(sh * cl + ch * sl)).astype(BF16)


def _dft_tables(n):
    assert n % LANES == 0
    n_hi = n // LANES
    kk = jnp.arange(n, dtype=jnp.int32)[:, None]
    ang_hi = ((kk * (jnp.arange(n_hi, dtype=jnp.int32) * LANES)[None, :]) & (n - 1)).astype(F32) * (2.0 * math.pi / n)
    ang_lo = ((kk * jnp.arange(LANES, dtype=jnp.int32)[None, :]) & (n - 1)).astype(F32) * (2.0 * math.pi / n)
    tr = _tile(n, 256)
    small = lambda w: pl.BlockSpec((tr, w), lambda i: (i, 0))
    c, sneg = pl.pallas_call(
        _dft_table_kernel,
        grid=(n // tr,),
        in_specs=[small(n_hi), small(n_hi), small(LANES), small(LANES)],
        out_specs=[small(n), small(n)],
        out_shape=[jax.ShapeDtypeStruct((n, n), BF16)] * 2,
        compiler_params=_cparams(("arbitrary",), 32),
        name="dft_tables",
    )(jnp.cos(ang_hi), jnp.sin(ang_hi), jnp.cos(ang_lo), jnp.sin(ang_lo))
    m = FNET_GROUP_DIM
    jk = np.outer(np.arange(m), np.arange(m)) % m
    angc = jk.astype(np.float64) * (2.0 * math.pi / m)
    cs = jnp.asarray(np.concatenate([np.cos(angc), np.sin(angc)], axis=1), F32).astype(BF16)
    return c, sneg, cs


def _tile(n, want):
    t = min(n, want)
    assert n % t == 0
    return t


def kernel(x, positions, attn_norm, w_in, q_a_norm, w_uq, kv_a_norm, w_ukv, q_norm, k_norm, sgu_ln_g, sgu_ln_b,
           sgu_w, sgu_b, w_proj_attn, w_proj_sgu, w_proj_fnet, w_out, ffn_norm, w_group, b_group, w_router,
           b_router, w1, w3, w2):
    batch, seq, _ = x.shape
    depth = w_in.shape[0]
    assert seq & (seq - 1) == 0 and seq % CHUNK == 0
    T = batch * seq
    tm = _tile(seq, 512)
    tmx = tm
    ridx = jnp.arange(tm, dtype=jnp.int32)
    utri = (ridx[:, None] < ridx[None, :]).astype(F32)
    rope = _rope_slabs(positions)
    dft_c, dft_sneg, cs = _dft_tables(seq)
    x2 = x.reshape(T, D_MODEL)
    w1b, w3b, w2b = w1.astype(BF16), w3.astype(BF16), w2.astype(BF16)
    for l in range(depth):
        p = _layer_params(l, attn_norm, w_in, q_a_norm, w_uq, kv_a_norm, w_ukv, q_norm, k_norm, sgu_ln_g,
                          sgu_ln_b, sgu_w, sgu_b, w_proj_attn, w_proj_sgu, w_proj_fnet, w_out, ffn_norm,
                          w_group, b_group, w_router, b_router)
        cqn, ckvn, ckvt, krt, u, v, fa, fb, gates = _inproj(
            x2, p["attn_norm"], p["w_ext"], p["q_a_norm"], p["kv_a_norm"], p["ln_g"], p["ln_b"], cs,
            batch, seq, _tile(seq, 256))
        q, k, vv = _mla_prep(cqn, ckvn, ckvt, krt, rope, p, batch, seq, tm)
        ya = _attention(p["safe"], q, k, vv, _tile(seq, 1024)).reshape(T, N_HEADS * V_HEAD)
        yf = _seq_dft(dft_c, dft_sneg, fa, fb, _tile(seq, 1024), _tile(batch * FNET_WIDTH, 1024), _tile(seq, 1024))
        xr, meta, cnt = _merge_route(
            x2, ya, u, v, yf, gates, p["sgu_w"], p["sgu_bias"], p["wpa"], p["wps"], p["wpf"], p["wo"],
            p["ffn_norm"], p["w_rg"], p["b_rg"], utri, seq, tm)
        dest, tt = _assign(cnt, meta, tmx)
        dest = dest.reshape(T)
        tt = tt.reshape(META_ROWS * LANES)
        xs = _dispatch(dest, tt, xr, T + N_GROUPS * tmx, tm)
        ys = _experts(tt, xs, p["ffn_norm"], w1b, w3b, w2b, l, tmx)
        x2 = _unsort(dest, ys, T, tm)
    return x2.reshape(batch, seq, D_MODEL)
```
